```python
import jax
import jax.numpy as jnp
from jax import lax
import numpy as np


D_MODEL = 1024
BATCH = 2
SEQ = 8192
DEPTH = 2

N_A_LAYERS = DEPTH // 2
N_B_LAYERS = DEPTH - N_A_LAYERS
Q_BLOCK = 128
NEG_INF = -1e30
NORM_EPS = 1e-6

NSA_HEADS = 16
NSA_GROUPS = 4
NSA_HPG = NSA_HEADS // NSA_GROUPS
NSA_DH = 64
CMP_BLOCK = 32
CMP_STRIDE = 16
CMP_HIDDEN = 256
SLC_BLOCK = 64
SLC_TOP = 16
WINDOW = 512
FORCE_SCORE = 1e4
NSA_IN_WIDTH = NSA_HEADS * NSA_DH + 6 * NSA_GROUPS * NSA_DH + 3 * NSA_HEADS

MLA_HEADS = 8
QK_NOPE = 128
QK_ROPE = 64
MLA_V = 128
Q_LORA = 384
KV_LORA = 256
ROPE_THETA = 10000.0

FFN_HIDDEN = -(-8 * D_MODEL // (3 * 256)) * 256

kernel_name = 'yoco_nsa_mla_hybrid'


def rms_norm(x, g):
    xf = x.astype(jnp.float32)
    y = xf * lax.rsqrt(jnp.mean(xf * xf, axis=-1, keepdims=True) + NORM_EPS)
    return (y * g.astype(jnp.float32)).astype(x.dtype)


def masked_softmax(s, mask):
    p = jax.nn.softmax(jnp.where(mask, s, NEG_INF), axis=-1)
    return jnp.where(mask, p, 0.0)


def alibi_slopes(n):
    return 2.0 ** (-8.0 * jnp.arange(1, n + 1, dtype=jnp.float32) / n)


def rope_tables(T):
    inv = ROPE_THETA ** (-jnp.arange(0, QK_ROPE, 2, dtype=jnp.float32) / QK_ROPE)
    ang = jnp.arange(T, dtype=jnp.float32)[:, None] * inv[None, :]
    return jnp.cos(ang), jnp.sin(ang)


def apply_rope(x, cos, sin):
    c = cos[None, :, None, :].astype(x.dtype)
    s = sin[None, :, None, :].astype(x.dtype)
    x1, x2 = jnp.split(x, 2, axis=-1)
    return jnp.concatenate([x1 * c - x2 * s, x1 * s + x2 * c], axis=-1)


def swiglu(h, w_gate_up, w_down):
    gate, up = jnp.split(h @ w_gate_up, 2, axis=-1)
    return (jax.nn.silu(gate) * up) @ w_down


_gather_bg = jax.vmap(jax.vmap(lambda src, idx: src[idx]))


def nsa_mixer(h, w_in, q_norm, kcmp_norm, kslc_norm, kwin_norm, pos_k, pos_v,
              k_w1, k_b1, k_w2, v_w1, v_b1, v_w2, w_out):
    B, T, _ = h.shape
    G, HG, DH = NSA_GROUPS, NSA_HPG, NSA_DH
    f32 = jnp.float32
    qw, kvw = NSA_HEADS * DH, G * DH
    cuts = [qw + i * kvw for i in range(7)]
    parts = jnp.split(h @ w_in, cuts, axis=-1)
    q = rms_norm(parts[0].reshape(B, T, G, HG, DH), q_norm)
    kc, vc, ksl, vsl, kw, vw = [a.reshape(B, T, G, DH) for a in parts[1:7]]
    gates = jax.nn.sigmoid(parts[7].astype(f32)).astype(h.dtype).reshape(B, T, G, HG, 3)

    n_cmp = (T - CMP_BLOCK) // CMP_STRIDE + 1
    cmp_j = jnp.arange(n_cmp)
    cmp_idx = cmp_j[:, None] * CMP_STRIDE + jnp.arange(CMP_BLOCK)[None, :]

    def compress(a, pe, w1, b1, w2):
        blk = a[:, cmp_idx] + pe[:, None, :].astype(a.dtype)
        blk = jnp.swapaxes(blk, 2, 3).reshape(B, n_cmp, G, CMP_BLOCK * DH)
        return jax.nn.gelu(blk @ w1 + b1) @ w2

    k_cmp = rms_norm(compress(kc, pos_k, k_w1, k_b1, k_w2), kcmp_norm)
    v_cmp = compress(vc, pos_v, v_w1, v_b1, v_w2)
    k_sel_src = jnp.swapaxes(rms_norm(ksl, kslc_norm), 1, 2)
    v_sel_src = jnp.swapaxes(vsl, 1, 2)
    pad = ((0, 0), (WINDOW, 0), (0, 0), (0, 0))
    kw_pad = jnp.pad(rms_norm(kw, kwin_norm), pad)
    vw_pad = jnp.pad(vw, pad)

    slopes = alibi_slopes(NSA_HEADS).reshape(G, HG)[:, :, None, None]
    scale = DH ** -0.5
    n_slc = T // SLC_BLOCK
    n_top = min(SLC_TOP, n_slc)
    n_sel = n_top * SLC_BLOCK
    cmp_end = cmp_j * CMP_STRIDE + CMP_BLOCK - 1
    cmp_mid = (cmp_j * CMP_STRIDE).astype(f32) + 0.5 * (CMP_BLOCK - 1)
    slc_i = jnp.arange(n_slc)
    overlap = ((cmp_j[:, None] * CMP_STRIDE <= slc_i[None, :] * SLC_BLOCK + SLC_BLOCK - 1)
               & (cmp_end[:, None] >= slc_i[None, :] * SLC_BLOCK)).astype(f32)
    win_off = jnp.arange(WINDOW + Q_BLOCK)
    in_blk = jnp.arange(SLC_BLOCK)

    def block(s0):
        t = s0 + jnp.arange(Q_BLOCK)
        tf = t.astype(f32)
        qb = lax.dynamic_slice_in_dim(q, s0, Q_BLOCK, axis=1)
        gb = lax.dynamic_slice_in_dim(gates, s0, Q_BLOCK, axis=1)
        s_c = (jnp.einsum('bqghd,bjgd->bghqj', qb, k_cmp).astype(f32) * scale
               - slopes * (tf[:, None] - cmp_mid[None, :]))
        p_c = masked_softmax(s_c, cmp_end[None, :] <= t[:, None])
        o_c = jnp.einsum('bghqj,bjgd->bqghd', p_c.astype(v_cmp.dtype), v_cmp)
        imp = jnp.einsum('bghqj,ji->bgqi', p_c, overlap)
        cur = (t // SLC_BLOCK)[:, None]
        valid = slc_i[None, :] * SLC_BLOCK <= t[:, None]
        forced = (slc_i[None, :] == 0) | (slc_i[None, :] == cur) | (slc_i[None, :] == cur - 1)
        score = jnp.where(valid, imp + jnp.where(forced, FORCE_SCORE, 0.0), -1.0)
        _, top = lax.top_k(score, n_top)
        sel_pos = (top[..., None] * SLC_BLOCK + in_blk).reshape(B, G, Q_BLOCK * n_sel)
        k_s = _gather_bg(k_sel_src, sel_pos).reshape(B, G, Q_BLOCK, n_sel, DH)
        v_s = _gather_bg(v_sel_src, sel_pos).reshape(B, G, Q_BLOCK, n_sel, DH)
        sel_pos = sel_pos.reshape(B, G, 1, Q_BLOCK, n_sel)
        s_s = (jnp.einsum('bqghd,bgqnd->bghqn', qb, k_s).astype(f32) * scale
               - slopes * (tf[:, None] - sel_pos.astype(f32)))
        p_s = masked_softmax(s_s, sel_pos <= t[:, None])
        o_s = jnp.einsum('bghqn,bgqnd->bqghd', p_s.astype(v_s.dtype), v_s)
        k_w = lax.dynamic_slice_in_dim(kw_pad, s0, WINDOW + Q_BLOCK, axis=1)
        v_w = lax.dynamic_slice_in_dim(vw_pad, s0, WINDOW + Q_BLOCK, axis=1)
        kpos = s0 - WINDOW + win_off
        dist = t[:, None] - kpos[None, :]
        s_w = (jnp.einsum('bqghd,bkgd->bghqk', qb, k_w).astype(f32) * scale
               - slopes * dist.astype(f32))
        p_w = masked_softmax(s_w, (dist >= 0) & (dist < WINDOW) & (kpos[None, :] >= 0))
        o_w = jnp.einsum('bghqk,bkgd->bqghd', p_w.astype(v_w.dtype), v_w)
        o = gb[..., 0:1] * o_c + gb[..., 1:2] * o_s + gb[..., 2:3] * o_w
        return o.reshape(B, Q_BLOCK, NSA_HEADS * DH)

    starts = jnp.arange(T // Q_BLOCK, dtype=jnp.int32) * Q_BLOCK
    o = jnp.swapaxes(lax.map(block, starts), 0, 1).reshape(B, T, NSA_HEADS * DH)
    return o @ w_out


def shared_mla_kv(x, kv_norm, kv_w_a, kv_c_norm, kv_w_b, kv_k_norm, cos, sin):
    B, T, _ = x.shape
    kv_a = rms_norm(x, kv_norm) @ kv_w_a
    c_kv = rms_norm(kv_a[..., :KV_LORA], kv_c_norm)
    k_pe = kv_a[..., KV_LORA:]
    kv = (c_kv @ kv_w_b).reshape(B, T, MLA_HEADS, QK_NOPE + MLA_V)
    k = jnp.concatenate([kv[..., :QK_NOPE],
                         jnp.broadcast_to(k_pe[:, :, None, :], (B, T, MLA_HEADS, QK_ROPE))], axis=-1)
    k = rms_norm(k, kv_k_norm)
    k = jnp.concatenate([k[..., :QK_NOPE], apply_rope(k[..., QK_NOPE:], cos, sin)], axis=-1)
    return k, kv[..., QK_NOPE:]


def mla_mixer(h, k, v, w_q_a, q_a_norm, w_q_b, q_norm, w_out, cos, sin):
    B, T, _ = h.shape
    q = (rms_norm(h @ w_q_a, q_a_norm) @ w_q_b).reshape(B, T, MLA_HEADS, QK_NOPE + QK_ROPE)
    q = rms_norm(q, q_norm)
    q = jnp.concatenate([q[..., :QK_NOPE], apply_rope(q[..., QK_NOPE:], cos, sin)], axis=-1)
    scale = (QK_NOPE + QK_ROPE) ** -0.5
    kpos = jnp.arange(T)

    def block(s0):
        t = s0 + jnp.arange(Q_BLOCK)
        qb = lax.dynamic_slice_in_dim(q, s0, Q_BLOCK, axis=1)
        s = jnp.einsum('bqhd,bkhd->bhqk', qb, k).astype(jnp.float32) * scale
        p = masked_softmax(s, kpos[None, :] <= t[:, None])
        o = jnp.einsum('bhqk,bkhd->bqhd', p.astype(v.dtype), v)
        return o.reshape(B, Q_BLOCK, MLA_HEADS * MLA_V)

    starts = jnp.arange(T // Q_BLOCK, dtype=jnp.int32) * Q_BLOCK
    o = jnp.swapaxes(lax.map(block, starts), 0, 1).reshape(B, T, MLA_HEADS * MLA_V)
    return o @ w_out


def setup_inputs(seed: int = 0) -> dict:
    key = jax.random.key(seed)
    ks = iter(jax.random.split(key, 32))
    f32 = jnp.float32

    def w(shape, fan_in):
        return jax.random.normal(next(ks), shape, f32) * (fan_in ** -0.5)

    def gain(shape):
        return 1.0 + 0.02 * jax.random.normal(next(ks), shape, f32)

    def small(shape, s):
        return s * jax.random.normal(next(ks), shape, f32)

    NA, NB, D = N_A_LAYERS, N_B_LAYERS, D_MODEL
    flat = CMP_BLOCK * NSA_DH
    return {
        'x': jax.random.normal(next(ks), (BATCH, SEQ, D), f32),
        'a_attn_norm': gain((NA, D)),
        'a_w_in': w((NA, D, NSA_IN_WIDTH), D),
        'a_q_norm': gain((NA, NSA_DH)),
        'a_kcmp_norm': gain((NA, NSA_DH)),
        'a_kslc_norm': gain((NA, NSA_DH)),
        'a_kwin_norm': gain((NA, NSA_DH)),
        'a_cmp_pos_k': small((NA, CMP_BLOCK, NSA_DH), 0.02),
        'a_cmp_pos_v': small((NA, CMP_BLOCK, NSA_DH), 0.02),
        'a_cmp_k_w1': w((NA, flat, CMP_HIDDEN), flat),
        'a_cmp_k_b1': small((NA, CMP_HIDDEN), 0.01),
        'a_cmp_k_w2': w((NA, CMP_HIDDEN, NSA_DH), CMP_HIDDEN),
        'a_cmp_v_w1': w((NA, flat, CMP_HIDDEN), flat),
        'a_cmp_v_b1': small((NA, CMP_HIDDEN), 0.01),
        'a_cmp_v_w2': w((NA, CMP_HIDDEN, NSA_DH), CMP_HIDDEN),
        'a_w_out': w((NA, NSA_HEADS * NSA_DH, D), NSA_HEADS * NSA_DH),
        'kv_norm': gain((D,)),
        'kv_w_a': w((D, KV_LORA + QK_ROPE), D),
        'kv_c_norm': gain((KV_LORA,)),
        'kv_w_b': w((KV_LORA, MLA_HEADS * (QK_NOPE + MLA_V)), KV_LORA),
        'kv_k_norm': gain((QK_NOPE + QK_ROPE,)),
        'b_attn_norm': gain((NB, D)),
        'b_w_q_a': w((NB, D, Q_LORA), D),
        'b_q_a_norm': gain((NB, Q_LORA)),
        'b_w_q_b': w((NB, Q_LORA, MLA_HEADS * (QK_NOPE + QK_ROPE)), Q_LORA),
        'b_q_norm': gain((NB, QK_NOPE + QK_ROPE)),
        'b_w_out': w((NB, MLA_HEADS * MLA_V, D), MLA_HEADS * MLA_V),
        'ffn_norm': gain((DEPTH, D)),
        'ffn_w_gate_up': w((DEPTH, D, 2 * FFN_HIDDEN), D),
        'ffn_w_down': w((DEPTH, FFN_HIDDEN, D), FFN_HIDDEN),
    }


def reference(x, a_attn_norm, a_w_in, a_q_norm, a_kcmp_norm, a_kslc_norm, a_kwin_norm,
              a_cmp_pos_k, a_cmp_pos_v, a_cmp_k_w1, a_cmp_k_b1, a_cmp_k_w2,
              a_cmp_v_w1, a_cmp_v_b1, a_cmp_v_w2, a_w_out,
              kv_norm, kv_w_a, kv_c_norm, kv_w_b, kv_k_norm,
              b_attn_norm, b_w_q_a, b_q_a_norm, b_w_q_b, b_q_norm, b_w_out,
              ffn_norm, ffn_w_gate_up, ffn_w_down):
    T = x.shape[1]
    cos, sin = rope_tables(T)
    shared_k = None
    shared_v = None
    for layer in range(DEPTH):
        if layer < N_A_LAYERS:
            i = layer
            x = x + nsa_mixer(rms_norm(x, a_attn_norm[i]), a_w_in[i], a_q_norm[i],
                              a_kcmp_norm[i], a_kslc_norm[i], a_kwin_norm[i],
                              a_cmp_pos_k[i], a_cmp_pos_v[i],
                              a_cmp_k_w1[i], a_cmp_k_b1[i], a_cmp_k_w2[i],
                              a_cmp_v_w1[i], a_cmp_v_b1[i], a_cmp_v_w2[i], a_w_out[i])
        else:
            j = layer - N_A_LAYERS
            x = x + mla_mixer(rms_norm(x, b_attn_norm[j]), shared_k, shared_v,
                              b_w_q_a[j], b_q_a_norm[j], b_w_q_b[j], b_q_norm[j],
                              b_w_out[j], cos, sin)
        x = x + swiglu(rms_norm(x, ffn_norm[layer]), ffn_w_gate_up[layer], ffn_w_down[layer])
        if layer == N_A_LAYERS - 1:
            shared_k, shared_v = shared_mla_kv(x, kv_norm, kv_w_a, kv_c_norm, kv_w_b,
                                               kv_k_norm, cos, sin)
    return x
```

```python
import functools

import jax
import jax.numpy as jnp
from jax import lax
from jax.experimental import pallas as pl
from jax.experimental.pallas import tpu as pltpu

F32 = jnp.float32
BF16 = jnp.bfloat16

NORM_EPS = 1e-6
NEG_INF = -1e30
LANES = 128

NSA_HEADS = 16
NSA_GROUPS = 4
NSA_HPG = NSA_HEADS // NSA_GROUPS
NSA_DH = 64
CMP_BLOCK = 32
CMP_STRIDE = 16
CMP_HIDDEN = 256
SLC_BLOCK = 64
SLC_TOP = 16
WINDOW = 512
FORCE_SCORE = 1e4
NSA_Q_TILE = 128
NSA_KEY_TILE = 256
NSA_WIN_TILE = 128

MLA_HEADS = 8
QK_NOPE = 128
QK_ROPE = 64
MLA_QK = QK_NOPE + QK_ROPE
MLA_V = 128
Q_LORA = 384
KV_LORA = 256
ROPE_THETA = 10000.0
MLA_TILE = 256

VMEM_LIMIT = 56 * 1024 * 1024


def _cparams(*sem):
    return pltpu.CompilerParams(dimension_semantics=sem, vmem_limit_bytes=VMEM_LIMIT)


def _rms(x, g):
    return x * lax.rsqrt(jnp.mean(x * x, axis=-1, keepdims=True) + NORM_EPS) * g


def _dot(a, b):
    return jnp.dot(a, b, preferred_element_type=F32)


def _dot_nt(a, b):
    return lax.dot_general(a, b, (((1,), (1,)), ((), ())), preferred_element_type=F32)


def _linear_kernel(*refs, has_gain, has_res):
    it = iter(refs)
    x_ref = next(it)
    w_ref = next(it)
    g_ref = next(it) if has_gain else None
    r_ref = next(it) if has_res else None
    o_ref = next(it)
    x = x_ref[...]
    if has_gain:
        x = _rms(x.astype(F32), g_ref[...])
    acc = _dot(x.astype(BF16), w_ref[...])
    if has_res:
        acc = acc + r_ref[...]
    o_ref[...] = acc.astype(o_ref.dtype)


def _linear(x, w, gain=None, residual=None, *, k_cols=None, tm=512, tn=None, out_dtype=F32):
    n = x.shape[0]
    k, m = w.shape
    if k_cols is None:
        assert x.shape[1] == k
    tn = m if tn is None else tn
    assert n % tm == 0 and m % tn == 0
    in_specs = [pl.BlockSpec((tm, k), lambda i, j: (i, 0)),
                pl.BlockSpec((k, tn), lambda i, j: (0, j))]
    args = [x, w]
    if gain is not None:
        in_specs.append(pl.BlockSpec((1, k), lambda i, j: (0, 0)))
        args.append(gain.reshape(1, k).astype(F32))
    if residual is not None:
        in_specs.append(pl.BlockSpec((tm, tn), lambda i, j: (i, j)))
        args.append(residual)
    return pl.pallas_call(
        functools.partial(_linear_kernel, has_gain=gain is not None, has_res=residual is not None),
        grid=(n // tm, m // tn),
        in_specs=in_specs,
        out_specs=pl.BlockSpec((tm, tn), lambda i, j: (i, j)),
        out_shape=jax.ShapeDtypeStruct((n, m), out_dtype),
        compiler_params=_cparams("parallel", "arbitrary"),
    )(*args)


def _ffn_kernel(x_ref, g_ref, wg_ref, wu_ref, wd_ref, o_ref, hn_sc, acc_sc):
    j = pl.program_id(1)

    @pl.when(j == 0)
    def _():
        hn_sc[...] = _rms(x_ref[...], g_ref[...]).astype(BF16)
        acc_sc[...] = jnp.zeros_like(acc_sc)

    h = hn_sc[...]
    gate = _dot(h, wg_ref[...])
    up = _dot(h, wu_ref[...])
    a = gate * jax.nn.sigmoid(gate) * up
    acc_sc[...] += _dot(a.astype(BF16), wd_ref[...])

    @pl.when(j == pl.num_programs(1) - 1)
    def _():
        o_ref[...] = x_ref[...] + acc_sc[...]


def _ffn(x, gain, w_gate_up, w_down, *, tm=1024, tc=256):
    n, d = x.shape
    hid = w_down.shape[0]
    assert n % tm == 0 and hid % tc == 0
    nc = hid // tc
    return pl.pallas_call(
        _ffn_kernel,
        grid=(n // tm, nc),
        in_specs=[pl.BlockSpec((tm, d), lambda i, j: (i, 0)),
                  pl.BlockSpec((1, d), lambda i, j: (0, 0)),
                  pl.BlockSpec((d, tc), lambda i, j: (0, j)),
                  pl.BlockSpec((d, tc), lambda i, j: (0, nc + j)),
                  pl.BlockSpec((tc, d), lambda i, j: (j, 0))],
        out_specs=pl.BlockSpec((tm, d), lambda i, j: (i, 0)),
        out_shape=jax.ShapeDtypeStruct((n, d), F32),
        scratch_shapes=[pltpu.VMEM((tm, d), BF16), pltpu.VMEM((tm, d), F32)],
        compiler_params=_cparams("parallel", "arbitrary"),
    )(x, gain.reshape(1, d), w_gate_up, w_gate_up, w_down)


def _nsa_prep_kernel(p_ref, qn_ref, ksn_ref, kwn_ref,
                     q_ref, kc_ref, vc_ref, ksl_ref, vsl_ref, kw_ref, vw_ref, gt_ref):
    G, HG, DH = NSA_GROUPS, NSA_HPG, NSA_DH
    qw, kvw = NSA_HEADS * DH, G * DH
    scale = DH ** -0.5
    tm = p_ref.shape[0]
    ones_col = (lax.broadcasted_iota(jnp.int32, (tm, LANES - DH), 1) == 0).astype(F32)

    def piece(base, g):
        return p_ref[:, base + g * DH: base + (g + 1) * DH]

    for g in range(G):
        for h in range(HG):
            qh = p_ref[:, (g * HG + h) * DH:(g * HG + h + 1) * DH]
            q_ref[0, g, h] = (_rms(qh, qn_ref[...]) * scale).astype(BF16)
        kc_ref[0, g] = piece(qw, g)
        vc_ref[0, g] = piece(qw + kvw, g)
        ksl_ref[0, g] = _rms(piece(qw + 2 * kvw, g), ksn_ref[...]).astype(BF16)
        vsl_ref[0, g] = jnp.concatenate([piece(qw + 3 * kvw, g), ones_col], axis=-1).astype(BF16)
        kw_ref[0, g] = _rms(piece(qw + 4 * kvw, g), kwn_ref[...]).astype(BF16)
        vw_ref[0, g] = jnp.concatenate([piece(qw + 5 * kvw, g), ones_col], axis=-1).astype(BF16)
        gb = qw + 6 * kvw + g * HG * 3
        gt_ref[0, g] = jax.nn.sigmoid(p_ref[:, gb:gb + HG * 3])


def _nsa_prep(p, q_norm, kslc_norm, kwin_norm, B, T, *, tm=256):
    G, HG, DH = NSA_GROUPS, NSA_HPG, NSA_DH
    nt = T // tm
    width = p.shape[1]
    gain = lambda: pl.BlockSpec((1, DH), lambda i: (0, 0))
    per_group = lambda last: pl.BlockSpec((1, G, tm, last), lambda i: (i // nt, 0, i % nt, 0))
    shp = lambda last, dt: jax.ShapeDtypeStruct((B, G, T, last), dt)
    return pl.pallas_call(
        _nsa_prep_kernel,
        grid=(B * nt,),
        in_specs=[pl.BlockSpec((tm, width), lambda i: (i, 0)), gain(), gain(), gain()],
        out_specs=[pl.BlockSpec((1, G, HG, tm, DH), lambda i: (i // nt, 0, 0, i % nt, 0)),
                   per_group(DH), per_group(DH), per_group(DH), per_group(LANES),
                   per_group(DH), per_group(LANES), per_group(HG * 3)],
        out_shape=[jax.ShapeDtypeStruct((B, G, HG, T, DH), BF16),
                   shp(DH, F32), shp(DH, F32), shp(DH, BF16), shp(LANES, BF16),
                   shp(DH, BF16), shp(LANES, BF16), shp(HG * 3, F32)],
        compiler_params=_cparams("parallel"),
    )(p, q_norm.reshape(1, DH), kslc_norm.reshape(1, DH), kwin_norm.reshape(1, DH))


def _compress_kernel(rk_ref, rv_ref, pek_ref, pev_ref, kw1_ref, kb1_ref, kw2_ref,
                     vw1_ref, vb1_ref, vw2_ref, kn_ref, kcmp_ref, vcmp_ref):
    half = CMP_STRIDE * NSA_DH
    nrow = rk_ref.shape[2]

    def mlp(r_ref, pe_ref, w1_ref, b1_ref, w2_ref):
        r = r_ref[0, 0].astype(BF16)
        ya = _dot(r, w1_ref[:half, :])
        yb = _dot(r, w1_ref[half:, :])
        pe = jnp.broadcast_to(pe_ref[...], (8, 2 * half)).astype(BF16)
        c = _dot(pe, w1_ref[...])[0:1] + b1_ref[...]
        hid = ya + pltpu.roll(yb, nrow - 1, 0) + c
        return _dot(jax.nn.gelu(hid).astype(BF16), w2_ref[...])

    kcmp_ref[0, 0] = _rms(mlp(rk_ref, pek_ref, kw1_ref, kb1_ref, kw2_ref), kn_ref[...]).astype(BF16)
    vcmp_ref[0, 0] = mlp(rv_ref, pev_ref, vw1_ref, vb1_ref, vw2_ref).astype(BF16)


def _compress(rk, rv, pe_k, pe_v, k_w1, k_b1, k_w2, v_w1, v_b1, v_w2, kcmp_norm):
    B, G, nrow, half = rk.shape
    DH, HID = NSA_DH, CMP_HIDDEN
    full = lambda shape: pl.BlockSpec(shape, lambda b, g: (0,) * len(shape))
    r_spec = pl.BlockSpec((1, 1, nrow, half), lambda b, g: (b, g, 0, 0))
    o_spec = pl.BlockSpec((1, 1, nrow, DH), lambda b, g: (b, g, 0, 0))
    return pl.pallas_call(
        _compress_kernel,
        grid=(B, G),
        in_specs=[r_spec, r_spec, full((1, 2 * half)), full((1, 2 * half)),
                  full((2 * half, HID)), full((1, HID)), full((HID, DH)),
                  full((2 * half, HID)), full((1, HID)), full((HID, DH)), full((1, DH))],
        out_specs=[o_spec, o_spec],
        out_shape=[jax.ShapeDtypeStruct((B, G, nrow, DH), BF16)] * 2,
        compiler_params=_cparams("parallel", "parallel"),
    )(rk, rv, pe_k.reshape(1, -1), pe_v.reshape(1, -1),
      k_w1.astype(BF16), k_b1.reshape(1, HID), k_w2.astype(BF16),
      v_w1.astype(BF16), v_b1.reshape(1, HID), v_w2.astype(BF16), kcmp_norm.reshape(1, DH))


def _split3(x):
    hi = x.astype(BF16)
    r1 = x - hi.astype(F32)
    mid = r1.astype(BF16)
    lo = (r1 - mid.astype(F32)).astype(BF16)
    return hi, mid, lo


def _nsa_cmp_kernel(q_ref, kc_ref, vc_ref, gt_ref, sl_ref, ov_ref, gm_ref,
                    oc_ref, sel_ref, act_ref, *, n_top):
    HG, DH, TQ = NSA_HPG, NSA_DH, NSA_Q_TILE
    s0 = pl.program_id(2) * TQ
    ncmp = kc_ref.shape[2]
    n_slc = ov_ref.shape[1]
    qs = q_ref[0, 0].reshape(HG * TQ, DH)
    s = _dot_nt(qs, kc_ref[0, 0])
    j = lax.broadcasted_iota(jnp.int32, (TQ, ncmp), 1)
    t = s0 + lax.broadcasted_iota(jnp.int32, (TQ, ncmp), 0)
    mask = j * CMP_STRIDE + (CMP_BLOCK - 1) <= t
    dist = t.astype(F32) - (j.astype(F32) * CMP_STRIDE + 0.5 * (CMP_BLOCK - 1))
    gt = gt_ref[0, 0]
    psum = jnp.zeros((TQ, ncmp), F32)
    outs = []
    for h in range(HG):
        rows = slice(h * TQ, (h + 1) * TQ)
        sh = jnp.where(mask, s[rows] - sl_ref[0, rows, 0:1] * dist, NEG_INF)
        m = jnp.max(sh, axis=-1, keepdims=True)
        e = jnp.where(mask, jnp.exp(sh - m), 0.0)
        l = jnp.sum(e, axis=-1, keepdims=True)
        inv = jnp.where(l > 0.0, 1.0 / l, 0.0)
        p = e * inv
        psum = psum + p
        outs.append(_dot(p.astype(BF16), vc_ref[0, 0]) * gt[:, 3 * h:3 * h + 1])
    oc_ref[...] = jnp.concatenate(outs, axis=-1)

    hi, mid, lo = _split3(psum)
    imp = _dot(hi, ov_ref[...]) + _dot(mid, ov_ref[...]) + _dot(lo, ov_ref[...])
    imp_t = imp.T
    blk = lax.broadcasted_iota(jnp.int32, (n_slc, TQ), 0)
    tq = s0 + lax.broadcasted_iota(jnp.int32, (n_slc, TQ), 1)
    cur = lax.shift_right_logical(tq, SLC_BLOCK.bit_length() - 1)
    forced = (blk == 0) | (blk == cur) | (blk == cur - 1)
    score = jnp.where(blk * SLC_BLOCK <= tq, imp_t + jnp.where(forced, FORCE_SCORE, 0.0), -1.0)

    def pick(_, carry):
        score, sel = carry
        mx = jnp.max(score, axis=0, keepdims=True)
        first = jnp.min(jnp.where(score == mx, blk, n_slc), axis=0, keepdims=True)
        hit = blk == first
        return jnp.where(hit, -2.0, score), jnp.where(hit, 1.0, sel)

    _, sel_t = lax.fori_loop(0, n_top, pick, (score, jnp.zeros((n_slc, TQ), F32)), unroll=True)
    sel = sel_t.T.astype(BF16)
    sel_ref[0, 0] = sel
    cnt = _dot(jnp.ones((8, TQ), BF16), sel)
    act_ref[0, 0, 0] = _dot((cnt > 0.0).astype(BF16), gm_ref[...])


def _nsa_cmp(q, kcmp, vcmp, gates, slope_tab, overlap, group_mat, B, T, n_top):
    G, HG, DH, TQ = NSA_GROUPS, NSA_HPG, NSA_DH, NSA_Q_TILE
    nq = T // TQ
    ncmp = kcmp.shape[2]
    n_slc = overlap.shape[1]
    return pl.pallas_call(
        functools.partial(_nsa_cmp_kernel, n_top=n_top),
        grid=(B, G, nq),
        in_specs=[pl.BlockSpec((1, 1, HG, TQ, DH), lambda b, g, i: (b, g, 0, i, 0)),
                  pl.BlockSpec((1, 1, ncmp, DH), lambda b, g, i: (b, g, 0, 0)),
                  pl.BlockSpec((1, 1, ncmp, DH), lambda b, g, i: (b, g, 0, 0)),
                  pl.BlockSpec((1, 1, TQ, HG * 3), lambda b, g, i: (b, g, i, 0)),
                  pl.BlockSpec((1, HG * TQ, LANES), lambda b, g, i: (g, 0, 0)),
                  pl.BlockSpec((ncmp, n_slc), lambda b, g, i: (0, 0)),
                  pl.BlockSpec((n_slc, LANES), lambda b, g, i: (0, 0))],
        out_specs=[pl.BlockSpec((TQ, HG * DH), lambda b, g, i: (b * nq + i, g)),
                   pl.BlockSpec((1, 1, TQ, n_slc), lambda b, g, i: (b, g, i, 0)),
                   pl.BlockSpec((1, 1, 1, 8, LANES), lambda b, g, i: (b, g, i, 0, 0))],
        out_shape=[jax.ShapeDtypeStruct((B * T, NSA_HEADS * DH), F32),
                   jax.ShapeDtypeStruct((B, G, T, n_slc), BF16),
                   jax.ShapeDtypeStruct((B, G, nq, 8, LANES), F32)],
        compiler_params=_cparams("parallel", "parallel", "arbitrary"),
    )(q, kcmp, vcmp, gates, slope_tab, overlap, group_mat)


def _nsa_main_kernel(act_ref, q_ref, sel_ref, oc_ref, gt_ref, sl_ref, ex_ref,
                     ksl_ref, vsl_ref, kw_ref, vw_ref, o_ref, m_sc, acc_sc, *, n_ktiles):
    HG, DH, TQ, KT, WT = NSA_HPG, NSA_DH, NSA_Q_TILE, NSA_KEY_TILE, NSA_WIN_TILE
    b, g, qi = pl.program_id(0), pl.program_id(1), pl.program_id(2)
    nq = pl.num_programs(2)
    s0 = qi * TQ
    qs = q_ref[0, 0].reshape(HG * TQ, DH)
    selb = sel_ref[0, 0]
    gt = gt_ref[0, 0]

    def reset():
        m_sc[...] = jnp.full_like(m_sc, NEG_INF)
        acc_sc[...] = jnp.zeros_like(acc_sc)

    def attend(k_tile, v_tile, mask, rel):
        s = _dot_nt(qs, k_tile)
        for h in range(HG):
            rows = slice(h * TQ, (h + 1) * TQ)
            sh = s[rows] + sl_ref[0, rows, 0:1] * rel
            if mask is not None:
                sh = jnp.where(mask, sh, NEG_INF)
            m_old = m_sc[rows]
            m_new = jnp.maximum(m_old, jnp.max(sh, axis=-1, keepdims=True))
            p = jnp.exp(sh - m_new[:, 0:1])
            acc_sc[rows] = jnp.exp(m_old - m_new) * acc_sc[rows] + _dot(p.astype(BF16), v_tile)
            m_sc[rows] = m_new

    def result(h):
        rows = slice(h * TQ, (h + 1) * TQ)
        a = acc_sc[rows]
        return a[:, :DH] / a[:, DH:DH + 1]

    reset()
    act_base = ((b * NSA_GROUPS + g) * nq + qi) * n_ktiles
    qq = lax.broadcasted_iota(jnp.int32, (TQ, KT), 0)
    kk = lax.broadcasted_iota(jnp.int32, (TQ, KT), 1)

    def sel_tile(i, carry):
        @pl.when(act_ref[act_base + i] > 0)
        def _():
            k0 = pl.multiple_of(i * KT, KT)
            rel = (k0 - s0) + kk - qq
            picked = _dot(selb, ex_ref[:, pl.ds(k0, KT)])
            mask = (picked > 0.5) & (rel <= 0)
            attend(ksl_ref[0, 0, pl.ds(k0, KT), :], vsl_ref[0, 0, pl.ds(k0, KT), :],
                   mask, rel.astype(F32))
        return carry

    lax.fori_loop(0, (s0 + TQ + KT - 1) // KT, sel_tile, 0)
    o_sel = [result(h) for h in range(HG)]

    reset()
    qw = lax.broadcasted_iota(jnp.int32, (TQ, WT), 0)
    kw_i = lax.broadcasted_iota(jnp.int32, (TQ, WT), 1)
    n_win = WINDOW // WT
    for jt in range(n_win + 1):
        if jt == 0:
            mask = kw_i > qw
        elif jt == n_win:
            mask = kw_i <= qw
        else:
            mask = None
        off = (jt - n_win) * WT

        @pl.when(s0 + off >= 0)
        def _():
            k0 = pl.multiple_of(s0 + off, WT)
            rel = (off + kw_i - qw).astype(F32)
            attend(kw_ref[0, 0, pl.ds(k0, WT), :], vw_ref[0, 0, pl.ds(k0, WT), :], mask, rel)

    outs = []
    for h in range(HG):
        outs.append(oc_ref[:, h * DH:(h + 1) * DH]
                    + gt[:, 3 * h + 1:3 * h + 2] * o_sel[h]
                    + gt[:, 3 * h + 2:3 * h + 3] * result(h))
    o_ref[...] = jnp.concatenate(outs, axis=-1)


def _nsa_main(act, q, sel, oc, gates, slope_tab, expand, ksl, vsl, kw, vw, B, T):
    G, HG, DH, TQ = NSA_GROUPS, NSA_HPG, NSA_DH, NSA_Q_TILE
    nq = T // TQ
    n_slc = sel.shape[3]
    n_ktiles = T // NSA_KEY_TILE
    kv = lambda last: pl.BlockSpec((1, 1, T, last), lambda b, g, i, a: (b, g, 0, 0))
    grid_spec = pltpu.PrefetchScalarGridSpec(
        num_scalar_prefetch=1,
        grid=(B, G, nq),
        in_specs=[pl.BlockSpec((1, 1, HG, TQ, DH), lambda b, g, i, a: (b, g, 0, i, 0)),
                  pl.BlockSpec((1, 1, TQ, n_slc), lambda b, g, i, a: (b, g, i, 0)),
                  pl.BlockSpec((TQ, HG * DH), lambda b, g, i, a: (b * nq + i, g)),
                  pl.BlockSpec((1, 1, TQ, HG * 3), lambda b, g, i, a: (b, g, i, 0)),
                  pl.BlockSpec((1, HG * TQ, LANES), lambda b, g, i, a: (g, 0, 0)),
                  pl.BlockSpec((n_slc, T), lambda b, g, i, a: (0, 0)),
                  kv(DH), kv(LANES), kv(DH), kv(LANES)],
        out_specs=pl.BlockSpec((TQ, HG * DH), lambda b, g, i, a: (b * nq + i, g)),
        scratch_shapes=[pltpu.VMEM((HG * TQ, LANES), F32), pltpu.VMEM((HG * TQ, LANES), F32)],
    )
    return pl.pallas_call(
        functools.partial(_nsa_main_kernel, n_ktiles=n_ktiles),
        grid_spec=grid_spec,
        out_shape=jax.ShapeDtypeStruct((B * T, NSA_HEADS * DH), F32),
        compiler_params=_cparams("parallel", "parallel", "arbitrary"),
    )(act, q, sel, oc, gates, slope_tab, expand, ksl, vsl, kw, vw)


def _nsa_mixer(h_proj, B, T, q_norm, kcmp_norm, kslc_norm, kwin_norm, pos_k, pos_v,
               k_w1, k_b1, k_w2, v_w1, v_b1, v_w2):
    G, HG, DH, TQ = NSA_GROUPS, NSA_HPG, NSA_DH, NSA_Q_TILE
    q, kc, vc, ksl, vsl, kw, vw, gates = _nsa_prep(h_proj, q_norm, kslc_norm, kwin_norm, B, T)

    nrow = T // CMP_STRIDE
    half = CMP_STRIDE * DH
    kcmp, vcmp = _compress(kc.reshape(B, G, nrow, half), vc.reshape(B, G, nrow, half),
                           pos_k, pos_v, k_w1, k_b1, k_w2, v_w1, v_b1, v_w2, kcmp_norm)

    n_slc = T // SLC_BLOCK
    n_top = min(SLC_TOP, n_slc)
    n_slc_pad = max(n_slc, LANES)
    slopes = 2.0 ** (-8.0 * jnp.arange(1, NSA_HEADS + 1, dtype=F32) / NSA_HEADS)
    slope_tab = jnp.broadcast_to(slopes.reshape(G, HG, 1, 1), (G, HG, TQ, LANES)).reshape(G, HG * TQ, LANES)
    cj = jnp.arange(nrow)[:, None] * CMP_STRIDE
    si = jnp.arange(n_slc_pad)[None, :] * SLC_BLOCK
    overlap = ((cj <= si + SLC_BLOCK - 1) & (cj + CMP_BLOCK - 1 >= si)
               & (jnp.arange(nrow)[:, None] < nrow - 1)).astype(BF16)
    blocks_per_tile = NSA_KEY_TILE // SLC_BLOCK
    group_mat = (jnp.arange(n_slc_pad)[:, None] // blocks_per_tile
                 == jnp.arange(LANES)[None, :]).astype(BF16)
    expand = (jnp.arange(n_slc_pad)[:, None] == jnp.arange(T)[None, :] // SLC_BLOCK).astype(BF16)

    oc, sel, act = _nsa_cmp(q, kcmp, vcmp, gates, slope_tab, overlap, group_mat, B, T, n_top)
    n_ktiles = T // NSA_KEY_TILE
    act_i = act[:, :, :, 0, :n_ktiles].astype(jnp.int32).reshape(-1)
    return _nsa_main(act_i, q, sel, oc, gates, slope_tab, expand, ksl, vsl, kw, vw, B, T)


def _rope(r, cos2, sin2):
    half = QK_ROPE // 2
    rot = jnp.concatenate([r[:, half:], r[:, :half]], axis=-1)
    return r * cos2 + rot * sin2


def _mla_q_kernel(q_ref, g_ref, cos_ref, sin_ref, o_ref):
    scale = MLA_QK ** -0.5
    for h in range(MLA_HEADS):
        qh = _rms(q_ref[:, h * MLA_QK:(h + 1) * MLA_QK], g_ref[...]) * scale
        roped = _rope(qh[:, QK_NOPE:], cos_ref[...], sin_ref[...])
        o_ref[0, h] = jnp.concatenate([qh[:, :QK_NOPE], roped], axis=-1).astype(BF16)


def _mla_kv_kernel(kv_ref, kpe_ref, g_ref, cos_ref, sin_ref, k_ref, v_ref):
    kpe = kpe_ref[:, :QK_ROPE]
    for h in range(MLA_HEADS):
        base = h * (QK_NOPE + MLA_V)
        kh = _rms(jnp.concatenate([kv_ref[:, base:base + QK_NOPE], kpe], axis=-1), g_ref[...])
        roped = _rope(kh[:, QK_NOPE:], cos_ref[...], sin_ref[...])
        k_ref[0, h] = jnp.concatenate([kh[:, :QK_NOPE], roped], axis=-1).astype(BF16)
        v_ref[0, h] = kv_ref[:, base + QK_NOPE:base + QK_NOPE + MLA_V].astype(BF16)


def _mla_prep_q(q, gain, cos2, sin2, B, T, *, tm=256):
    nt = T // tm
    return pl.pallas_call(
        _mla_q_kernel,
        grid=(B * nt,),
        in_specs=[pl.BlockSpec((tm, MLA_HEADS * MLA_QK), lambda i: (i, 0)),
                  pl.BlockSpec((1, MLA_QK), lambda i: (0, 0)),
                  pl.BlockSpec((tm, QK_ROPE), lambda i: (i % nt, 0)),
                  pl.BlockSpec((tm, QK_ROPE), lambda i: (i % nt, 0))],
        out_specs=pl.BlockSpec((1, MLA_HEADS, tm, MLA_QK), lambda i: (i // nt, 0, i % nt, 0)),
        out_shape=jax.ShapeDtypeStruct((B, MLA_HEADS, T, MLA_QK), BF16),
        compiler_params=_cparams("parallel"),
    )(q, gain.reshape(1, MLA_QK), cos2, sin2)


def _mla_prep_kv(kv, kv_a, gain, cos2, sin2, B, T, *, tm=256):
    nt = T // tm
    pe_block = KV_LORA // LANES
    return pl.pallas_call(
        _mla_kv_kernel,
        grid=(B * nt,),
        in_specs=[pl.BlockSpec((tm, MLA_HEADS * (QK_NOPE + MLA_V)), lambda i: (i, 0)),
                  pl.BlockSpec((tm, LANES), lambda i: (i, pe_block)),
                  pl.BlockSpec((1, MLA_QK), lambda i: (0, 0)),
                  pl.BlockSpec((tm, QK_ROPE), lambda i: (i % nt, 0)),
                  pl.BlockSpec((tm, QK_ROPE), lambda i: (i % nt, 0))],
        out_specs=[pl.BlockSpec((1, MLA_HEADS, tm, MLA_QK), lambda i: (i // nt, 0, i % nt, 0)),
                   pl.BlockSpec((1, MLA_HEADS, tm, MLA_V), lambda i: (i // nt, 0, i % nt, 0))],
        out_shape=[jax.ShapeDtypeStruct((B, MLA_HEADS, T, MLA_QK), BF16),
                   jax.ShapeDtypeStruct((B, MLA_HEADS, T, MLA_V), BF16)],
        compiler_params=_cparams("parallel"),
    )(kv, kv_a, gain.reshape(1, MLA_QK), cos2, sin2)


def _mla_attn_kernel(q_ref, k_ref, v_ref, o_ref, m_sc, l_sc, acc_sc):
    TQ = MLA_TILE
    qi = pl.program_id(2)
    q = q_ref[0, 0]
    m_sc[...] = jnp.full_like(m_sc, NEG_INF)
    l_sc[...] = jnp.zeros_like(l_sc)
    acc_sc[...] = jnp.zeros_like(acc_sc)

    def step(i, masked):
        k0 = pl.multiple_of(i * TQ, TQ)
        s = _dot_nt(q, k_ref[0, 0, pl.ds(k0, TQ), :])
        if masked:
            qq = lax.broadcasted_iota(jnp.int32, (TQ, TQ), 0)
            kk = lax.broadcasted_iota(jnp.int32, (TQ, TQ), 1)
            s = jnp.where(kk <= qq, s, NEG_INF)
        m_old = m_sc[...]
        m_new = jnp.maximum(m_old, jnp.max(s, axis=-1, keepdims=True))
        p = jnp.exp(s - m_new[:, 0:1])
        alpha = jnp.exp(m_old - m_new)
        l_sc[...] = alpha * l_sc[...] + jnp.sum(p, axis=-1, keepdims=True)
        acc_sc[...] = alpha * acc_sc[...] + _dot(p.astype(BF16), v_ref[0, 0, pl.ds(k0, TQ), :])
        m_sc[...] = m_new

    def body(i, carry):
        step(i, False)
        return carry

    lax.fori_loop(0, qi, body, 0)
    step(qi, True)
    o_ref[...] = acc_sc[...] / l_sc[...]


def _mla_attn(q, k, v, B, T):
    H, TQ = MLA_HEADS, MLA_TILE
    nq = T // TQ
    return pl.pallas_call(
        _mla_attn_kernel,
        grid=(B, H, nq),
        in_specs=[pl.BlockSpec((1, 1, TQ, MLA_QK), lambda b, h, i: (b, h, i, 0)),
                  pl.BlockSpec((1, 1, T, MLA_QK), lambda b, h, i: (b, h, 0, 0)),
                  pl.BlockSpec((1, 1, T, MLA_V), lambda b, h, i: (b, h, 0, 0))],
        out_specs=pl.BlockSpec((TQ, MLA_V), lambda b, h, i: (b * nq + i, h)),
        out_shape=jax.ShapeDtypeStruct((B * T, H * MLA_V), F32),
        scratch_shapes=[pltpu.VMEM((TQ, LANES), F32), pltpu.VMEM((TQ, LANES), F32),
                        pltpu.VMEM((TQ, MLA_V), F32)],
        compiler_params=_cparams("parallel", "parallel", "arbitrary"),
    )(q, k, v)


def _pad_cols(w, mult=LANES):
    pad = -w.shape[1] % mult
    return jnp.pad(w, ((0, 0), (0, pad))) if pad else w


def kernel(x, a_attn_norm, a_w_in, a_q_norm, a_kcmp_norm, a_kslc_norm, a_kwin_norm, a_cmp_pos_k, a_cmp_pos_v, a_cmp_k_w1, a_cmp_k_b1, a_cmp_k_w2, a_cmp_v_w1, a_cmp_v_b1, a_cmp_v_w2, a_w_out, kv_norm, kv_w_a, kv_c_norm, kv_w_b, kv_k_norm, b_attn_norm, b_w_q_a, b_q_a_norm, b_w_q_b, b_q_norm, b_w_out, ffn_norm, ffn_w_gate_up, ffn_w_down):
    B, T, D = x.shape
    n_a = a_w_in.shape[0]
    n_b = b_w_q_a.shape[0]
    xs = x.reshape(B * T, D)

    inv = ROPE_THETA ** (-jnp.arange(0, QK_ROPE, 2, dtype=F32) / QK_ROPE)
    ang = jnp.arange(T, dtype=F32)[:, None] * inv[None, :]
    cos, sin = jnp.cos(ang), jnp.sin(ang)
    cos2 = jnp.concatenate([cos, cos], axis=-1)
    sin2 = jnp.concatenate([-sin, sin], axis=-1)

    k_shared = v_shared = None
    for layer in range(n_a + n_b):
        if layer < n_a:
            i = layer
            proj = _linear(xs, _pad_cols(a_w_in[i]).astype(BF16), gain=a_attn_norm[i])
            o = _nsa_mixer(proj, B, T, a_q_norm[i], a_kcmp_norm[i], a_kslc_norm[i], a_kwin_norm[i],
                           a_cmp_pos_k[i], a_cmp_pos_v[i], a_cmp_k_w1[i], a_cmp_k_b1[i], a_cmp_k_w2[i],
                           a_cmp_v_w1[i], a_cmp_v_b1[i], a_cmp_v_w2[i])
            xs = _linear(o, a_w_out[i].astype(BF16), residual=xs)
        else:
            j = layer - n_a
            qa = _linear(xs, b_w_q_a[j].astype(BF16), gain=b_attn_norm[j])
            qb = _linear(qa, b_w_q_b[j].astype(BF16), gain=b_q_a_norm[j])
            q = _mla_prep_q(qb, b_q_norm[j], cos2, sin2, B, T)
            o = _mla_attn(q, k_shared, v_shared, B, T)
            xs = _linear(o, b_w_out[j].astype(BF16), residual=xs)
        xs = _ffn(xs, ffn_norm[layer], ffn_w_gate_up[layer].astype(BF16), ffn_w_down[layer].astype(BF16))
        if layer == n_a - 1:
            kv_a = _linear(xs, _pad_cols(kv_w_a).astype(BF16), gain=kv_norm)
            kv = _linear(kv_a, kv_w_b.astype(BF16), gain=kv_c_norm, k_cols=KV_LORA)
            k_shared, v_shared = _mla_prep_kv(kv, kv_a, kv_k_norm, cos2, sin2, B, T)
    return xs.reshape(B, T, D)
```

```python
import functools

import jax
import jax.numpy as jnp
from jax import lax
from jax.experimental import pallas as pl
from jax.experimental.pallas import tpu as pltpu

F32 = jnp.float32
BF16 = jnp.bfloat16

NORM_EPS = 1e-6
NEG_INF = -1e30
LANES = 128

NSA_HEADS = 16
NSA_GROUPS = 4
NSA_HPG = NSA_HEADS // NSA_GROUPS
NSA_DH = 64
CMP_BLOCK = 32
CMP_STRIDE = 16
CMP_HIDDEN = 256
SLC_BLOCK = 64
SLC_TOP = 16
WINDOW = 512
FORCE_SCORE = 1e4
NSA_Q_TILE = 256
NSA_KEY_TILE = 256
NSA_WIN_TILE = 256
LOG2E = 1.4426950408889634

MLA_HEADS = 8
QK_NOPE = 128
QK_ROPE = 64
MLA_QK = QK_NOPE + QK_ROPE
MLA_V = 128
Q_LORA = 384
KV_LORA = 256
ROPE_THETA = 10000.0
MLA_TILE = 512

VMEM_LIMIT = 56 * 1024 * 1024


def _cparams(*sem):
    return pltpu.CompilerParams(dimension_semantics=sem, vmem_limit_bytes=VMEM_LIMIT)


def _rms(x, g):
    return x * lax.rsqrt(jnp.mean(x * x, axis=-1, keepdims=True) + NORM_EPS) * g


def _dot(a, b):
    return jnp.dot(a, b, preferred_element_type=F32)


def _dot_nt(a, b):
    return lax.dot_general(a, b, (((1,), (1,)), ((), ())), preferred_element_type=F32)


def _linear_kernel(*refs, has_gain, has_res):
    it = iter(refs)
    x_ref = next(it)
    w_ref = next(it)
    g_ref = next(it) if has_gain else None
    r_ref = next(it) if has_res else None
    o_ref = next(it)
    x = x_ref[...]
    if has_gain:
        x = _rms(x.astype(F32), g_ref[...])
    acc = _dot(x.astype(BF16), w_ref[...])
    if has_res:
        acc = acc + r_ref[...]
    o_ref[...] = acc.astype(o_ref.dtype)


def _linear(x, w, gain=None, residual=None, *, name, k_cols=None, tm=512, tn=None, out_dtype=F32):
    n = x.shape[0]
    k, m = w.shape
    if k_cols is None:
        assert x.shape[1] == k
    tn = m if tn is None else tn
    assert n % tm == 0 and m % tn == 0
    in_specs = [pl.BlockSpec((tm, k), lambda i, j: (i, 0)),
                pl.BlockSpec((k, tn), lambda i, j: (0, j))]
    args = [x, w]
    if gain is not None:
        in_specs.append(pl.BlockSpec((1, k), lambda i, j: (0, 0)))
        args.append(gain.reshape(1, k).astype(F32))
    if residual is not None:
        in_specs.append(pl.BlockSpec((tm, tn), lambda i, j: (i, j)))
        args.append(residual)
    return pl.pallas_call(
        functools.partial(_linear_kernel, has_gain=gain is not None, has_res=residual is not None),
        grid=(n // tm, m // tn),
        in_specs=in_specs,
        out_specs=pl.BlockSpec((tm, tn), lambda i, j: (i, j)),
        out_shape=jax.ShapeDtypeStruct((n, m), out_dtype),
        compiler_params=_cparams("parallel", "arbitrary"),
        name=name,
    )(*args)


def _ffn_kernel(x_ref, g_ref, wg_ref, wu_ref, wd_ref, o_ref, hn_sc, acc_sc):
    j = pl.program_id(1)

    @pl.when(j == 0)
    def _():
        hn_sc[...] = _rms(x_ref[...], g_ref[...]).astype(BF16)
        acc_sc[...] = jnp.zeros_like(acc_sc)

    h = hn_sc[...]
    gate = _dot(h, wg_ref[...])
    up = _dot(h, wu_ref[...])
    a = gate * jax.nn.sigmoid(gate) * up
    acc_sc[...] += _dot(a.astype(BF16), wd_ref[...])

    @pl.when(j == pl.num_programs(1) - 1)
    def _():
        o_ref[...] = x_ref[...] + acc_sc[...]


def _ffn(x, gain, w_gate_up, w_down, *, tm=1024, tc=256):
    n, d = x.shape
    hid = w_down.shape[0]
    assert n % tm == 0 and hid % tc == 0
    nc = hid // tc
    return pl.pallas_call(
        _ffn_kernel,
        grid=(n // tm, nc),
        in_specs=[pl.BlockSpec((tm, d), lambda i, j: (i, 0)),
                  pl.BlockSpec((1, d), lambda i, j: (0, 0)),
                  pl.BlockSpec((d, tc), lambda i, j: (0, j)),
                  pl.BlockSpec((d, tc), lambda i, j: (0, nc + j)),
                  pl.BlockSpec((tc, d), lambda i, j: (j, 0))],
        out_specs=pl.BlockSpec((tm, d), lambda i, j: (i, 0)),
        out_shape=jax.ShapeDtypeStruct((n, d), F32),
        scratch_shapes=[pltpu.VMEM((tm, d), BF16), pltpu.VMEM((tm, d), F32)],
        compiler_params=_cparams("parallel", "arbitrary"),
        name="swiglu_ffn",
    )(x, gain.reshape(1, d), w_gate_up, w_gate_up, w_down)


def _nsa_prep_kernel(p_ref, qn_ref, ksn_ref, kwn_ref,
                     q_ref, kc_ref, vc_ref, ksl_ref, vsl_ref, kw_ref, vw_ref, gt_ref):
    G, HG, DH = NSA_GROUPS, NSA_HPG, NSA_DH
    qw, kvw = NSA_HEADS * DH, G * DH
    scale = DH ** -0.5 * LOG2E
    tm = p_ref.shape[0]
    ones_col = (lax.broadcasted_iota(jnp.int32, (tm, LANES - DH), 1) == 0).astype(F32)

    def piece(base, g):
        return p_ref[:, base + g * DH: base + (g + 1) * DH]

    for g in range(G):
        for h in range(HG):
            qh = p_ref[:, (g * HG + h) * DH:(g * HG + h + 1) * DH]
            q_ref[0, g, h] = (_rms(qh, qn_ref[...]) * scale).astype(BF16)
        kc_ref[0, g] = piece(qw, g)
        vc_ref[0, g] = piece(qw + kvw, g)
        ksl_ref[0, g] = _rms(piece(qw + 2 * kvw, g), ksn_ref[...]).astype(BF16)
        vsl_ref[0, g] = jnp.concatenate([piece(qw + 3 * kvw, g), ones_col], axis=-1).astype(BF16)
        kw_ref[0, g] = _rms(piece(qw + 4 * kvw, g), kwn_ref[...]).astype(BF16)
        vw_ref[0, g] = jnp.concatenate([piece(qw + 5 * kvw, g), ones_col], axis=-1).astype(BF16)
        gb = qw + 6 * kvw + g * HG * 3
        gt_ref[0, g] = jax.nn.sigmoid(p_ref[:, gb:gb + HG * 3])


def _nsa_prep(p, q_norm, kslc_norm, kwin_norm, B, T, *, tm=256):
    G, HG, DH = NSA_GROUPS, NSA_HPG, NSA_DH
    nt = T // tm
    width = p.shape[1]
    gain = lambda: pl.BlockSpec((1, DH), lambda i: (0, 0))
    per_group = lambda last: pl.BlockSpec((1, G, tm, last), lambda i: (i // nt, 0, i % nt, 0))
    shp = lambda last, dt: jax.ShapeDtypeStruct((B, G, T, last), dt)
    return pl.pallas_call(
        _nsa_prep_kernel,
        grid=(B * nt,),
        in_specs=[pl.BlockSpec((tm, width), lambda i: (i, 0)), gain(), gain(), gain()],
        out_specs=[pl.BlockSpec((1, G, HG, tm, DH), lambda i: (i // nt, 0, 0, i % nt, 0)),
                   per_group(DH), per_group(DH), per_group(DH), per_group(LANES),
                   per_group(DH), per_group(LANES), per_group(HG * 3)],
        out_shape=[jax.ShapeDtypeStruct((B, G, HG, T, DH), BF16),
                   shp(DH, F32), shp(DH, F32), shp(DH, BF16), shp(LANES, BF16),
                   shp(DH, BF16), shp(LANES, BF16), shp(HG * 3, F32)],
        compiler_params=_cparams("parallel"),
        name="nsa_prep",
    )(p, q_norm.reshape(1, DH), kslc_norm.reshape(1, DH), kwin_norm.reshape(1, DH))


def _compress_kernel(rk_ref, rv_ref, pek_ref, pev_ref, kw1_ref, kb1_ref, kw2_ref,
                     vw1_ref, vb1_ref, vw2_ref, kn_ref, kcmp_ref, vcmp_ref):
    half = CMP_STRIDE * NSA_DH
    nrow = rk_ref.shape[2]

    def mlp(r_ref, pe_ref, w1_ref, b1_ref, w2_ref):
        r = r_ref[0, 0].astype(BF16)
        ya = _dot(r, w1_ref[:half, :])
        yb = _dot(r, w1_ref[half:, :])
        pe = jnp.broadcast_to(pe_ref[...], (8, 2 * half)).astype(BF16)
        c = _dot(pe, w1_ref[...])[0:1] + b1_ref[...]
        hid = ya + pltpu.roll(yb, nrow - 1, 0) + c
        return _dot(jax.nn.gelu(hid).astype(BF16), w2_ref[...])

    kcmp_ref[0, 0] = _rms(mlp(rk_ref, pek_ref, kw1_ref, kb1_ref, kw2_ref), kn_ref[...]).astype(BF16)
    vcmp_ref[0, 0] = mlp(rv_ref, pev_ref, vw1_ref, vb1_ref, vw2_ref).astype(BF16)


def _compress(rk, rv, pe_k, pe_v, k_w1, k_b1, k_w2, v_w1, v_b1, v_w2, kcmp_norm):
    B, G, nrow, half = rk.shape
    DH, HID = NSA_DH, CMP_HIDDEN
    full = lambda shape: pl.BlockSpec(shape, lambda b, g: (0,) * len(shape))
    r_spec = pl.BlockSpec((1, 1, nrow, half), lambda b, g: (b, g, 0, 0))
    o_spec = pl.BlockSpec((1, 1, nrow, DH), lambda b, g: (b, g, 0, 0))
    return pl.pallas_call(
        _compress_kernel,
        grid=(B, G),
        in_specs=[r_spec, r_spec, full((1, 2 * half)), full((1, 2 * half)),
                  full((2 * half, HID)), full((1, HID)), full((HID, DH)),
                  full((2 * half, HID)), full((1, HID)), full((HID, DH)), full((1, DH))],
        out_specs=[o_spec, o_spec],
        out_shape=[jax.ShapeDtypeStruct((B, G, nrow, DH), BF16)] * 2,
        compiler_params=_cparams("parallel", "parallel"),
        name="nsa_compress",
    )(rk, rv, pe_k.reshape(1, -1), pe_v.reshape(1, -1),
      k_w1.astype(BF16), k_b1.reshape(1, HID), k_w2.astype(BF16),
      v_w1.astype(BF16), v_b1.reshape(1, HID), v_w2.astype(BF16), kcmp_norm.reshape(1, DH))


def _split3(x):
    hi = x.astype(BF16)
    r1 = x - hi.astype(F32)
    mid = r1.astype(BF16)
    lo = (r1 - mid.astype(F32)).astype(BF16)
    return hi, mid, lo


def _nsa_cmp_kernel(q_ref, kc_ref, vc_ref, gt_ref, sl_ref, ov_ref, gm_ref,
                    oc_ref, sel_ref, act_ref, imp_sc, *, n_top):
    HG, DH, TQ = NSA_HPG, NSA_DH, NSA_Q_TILE
    s0 = pl.program_id(2) * TQ
    ncmp = kc_ref.shape[2]
    n_slc = ov_ref.shape[1]
    qs = q_ref[0, 0].reshape(HG * TQ, DH)
    gt = gt_ref[0, 0]

    def attend(nc):
        s = _dot_nt(qs, kc_ref[0, 0, :nc, :])
        j = lax.broadcasted_iota(jnp.int32, (TQ, nc), 1)
        t = s0 + lax.broadcasted_iota(jnp.int32, (TQ, nc), 0)
        mask = j * CMP_STRIDE + (CMP_BLOCK - 1) <= t
        dist = t.astype(F32) - (j.astype(F32) * CMP_STRIDE + 0.5 * (CMP_BLOCK - 1))
        has_key = t[:, 0:1] >= CMP_BLOCK - 1
        psum = jnp.zeros((TQ, nc), F32)
        outs = []
        for h in range(HG):
            rows = slice(h * TQ, (h + 1) * TQ)
            sh = jnp.where(mask, s[rows] - sl_ref[0, rows, 0:1] * dist, NEG_INF)
            m = jnp.max(sh, axis=-1, keepdims=True)
            e = jnp.exp2(sh - m)
            l = jnp.sum(e, axis=-1, keepdims=True)
            inv = jnp.where(has_key, 1.0 / l, 0.0)
            p = e * inv
            psum = psum + p
            outs.append(_dot(p.astype(BF16), vc_ref[0, 0, :nc, :]) * gt[:, 3 * h:3 * h + 1])
        oc_ref[...] = jnp.concatenate(outs, axis=-1)
        hi, mid, lo = _split3(psum)
        ov = ov_ref[:nc, :]
        imp_sc[...] = _dot(hi, ov) + _dot(mid, ov) + _dot(lo, ov)

    need = (s0 + TQ) // CMP_STRIDE
    widths = list(range(LANES, ncmp + 1, LANES))
    for idx, nc in enumerate(widths):
        lo_w = widths[idx - 1] if idx else 0
        pl.when((need > lo_w) & (need <= nc))(functools.partial(attend, nc))

    imp_t = imp_sc[...].T
    blk = lax.broadcasted_iota(jnp.int32, (n_slc, TQ), 0)
    tq = s0 + lax.broadcasted_iota(jnp.int32, (n_slc, TQ), 1)
    cur = lax.shift_right_logical(tq, SLC_BLOCK.bit_length() - 1)
    forced = (blk == 0) | (blk == cur) | (blk == cur - 1)
    score = jnp.where(blk * SLC_BLOCK <= tq, imp_t + jnp.where(forced, FORCE_SCORE, 0.0), -1.0)

    def pick(_, carry):
        score, sel = carry
        mx = jnp.max(score, axis=0, keepdims=True)
        first = jnp.min(jnp.where(score == mx, blk, n_slc), axis=0, keepdims=True)
        hit = blk == first
        return jnp.where(hit, -2.0, score), jnp.where(hit, 1.0, sel)

    _, sel_t = lax.fori_loop(0, n_top, pick, (score, jnp.zeros((n_slc, TQ), F32)), unroll=True)
    sel = sel_t.T.astype(BF16)
    sel_ref[0, 0] = sel
    cnt = _dot(jnp.ones((8, TQ), BF16), sel)
    act_ref[0, 0, 0] = _dot((cnt > 0.0).astype(BF16), gm_ref[...])


def _nsa_cmp(q, kcmp, vcmp, gates, slope_tab, overlap, group_mat, B, T, n_top):
    G, HG, DH, TQ = NSA_GROUPS, NSA_HPG, NSA_DH, NSA_Q_TILE
    nq = T // TQ
    ncmp = kcmp.shape[2]
    n_slc = overlap.shape[1]
    return pl.pallas_call(
        functools.partial(_nsa_cmp_kernel, n_top=n_top),
        grid=(B, G, nq),
        in_specs=[pl.BlockSpec((1, 1, HG, TQ, DH), lambda b, g, i: (b, g, 0, i, 0)),
                  pl.BlockSpec((1, 1, ncmp, DH), lambda b, g, i: (b, g, 0, 0)),
                  pl.BlockSpec((1, 1, ncmp, DH), lambda b, g, i: (b, g, 0, 0)),
                  pl.BlockSpec((1, 1, TQ, HG * 3), lambda b, g, i: (b, g, i, 0)),
                  pl.BlockSpec((1, HG * TQ, LANES), lambda b, g, i: (g, 0, 0)),
                  pl.BlockSpec((ncmp, n_slc), lambda b, g, i: (0, 0)),
                  pl.BlockSpec((n_slc, LANES), lambda b, g, i: (0, 0))],
        out_specs=[pl.BlockSpec((TQ, HG * DH), lambda b, g, i: (b * nq + i, g)),
                   pl.BlockSpec((1, 1, TQ, n_slc), lambda b, g, i: (b, g, i, 0)),
                   pl.BlockSpec((1, 1, 1, 8, LANES), lambda b, g, i: (b, g, i, 0, 0))],
        out_shape=[jax.ShapeDtypeStruct((B * T, NSA_HEADS * DH), F32),
                   jax.ShapeDtypeStruct((B, G, T, n_slc), BF16),
                   jax.ShapeDtypeStruct((B, G, nq, 8, LANES), F32)],
        scratch_shapes=[pltpu.VMEM((TQ, n_slc), F32)],
        compiler_params=_cparams("parallel", "parallel", "arbitrary"),
        name="nsa_cmp_select",
    )(q, kcmp, vcmp, gates, slope_tab, overlap, group_mat)


def _nsa_main_kernel(act_ref, q_ref, sel_ref, oc_ref, gt_ref, sl_ref, ex_ref,
                     ksl_ref, vsl_ref, kw_ref, vw_ref, o_ref, m_sc, acc_sc, *, n_ktiles):
    HG, DH, TQ, KT, WT = NSA_HPG, NSA_DH, NSA_Q_TILE, NSA_KEY_TILE, NSA_WIN_TILE
    b, g, qi = pl.program_id(0), pl.program_id(1), pl.program_id(2)
    nq = pl.num_programs(2)
    s0 = qi * TQ
    qs = q_ref[0, 0].reshape(HG * TQ, DH)
    selb = sel_ref[0, 0]
    gt = gt_ref[0, 0]

    def reset():
        m_sc[...] = jnp.full_like(m_sc, NEG_INF)
        acc_sc[...] = jnp.zeros_like(acc_sc)

    def attend(k_tile, v_tile, mask, rel):
        s = _dot_nt(qs, k_tile)
        for h in range(HG):
            rows = slice(h * TQ, (h + 1) * TQ)
            sh = s[rows] + sl_ref[0, rows, 0:1] * rel
            if mask is not None:
                sh = jnp.where(mask, sh, NEG_INF)
            m_old = m_sc[rows]
            m_new = jnp.maximum(m_old, jnp.max(sh, axis=-1, keepdims=True))
            p = jnp.exp2(sh - m_new[:, 0:1])
            acc_sc[rows] = jnp.exp2(m_old - m_new) * acc_sc[rows] + _dot(p.astype(BF16), v_tile)
            m_sc[rows] = m_new

    def result(h):
        rows = slice(h * TQ, (h + 1) * TQ)
        a = acc_sc[rows]
        return a[:, :DH] / a[:, DH:DH + 1]

    reset()
    act_base = ((b * NSA_GROUPS + g) * nq + qi) * n_ktiles
    qq = lax.broadcasted_iota(jnp.int32, (TQ, KT), 0)
    kk = lax.broadcasted_iota(jnp.int32, (TQ, KT), 1)

    def sel_tile(i, carry):
        @pl.when(act_ref[act_base + i] > 0)
        def _():
            k0 = pl.multiple_of(i * KT, KT)
            rel = (k0 - s0) + kk - qq
            picked = _dot(selb, ex_ref[:, pl.ds(k0, KT)])
            mask = (picked > 0.5) & (rel <= 0)
            attend(ksl_ref[0, 0, pl.ds(k0, KT), :], vsl_ref[0, 0, pl.ds(k0, KT), :],
                   mask, rel.astype(F32))
        return carry

    lax.fori_loop(0, (s0 + TQ + KT - 1) // KT, sel_tile, 0)
    o_sel = [result(h) for h in range(HG)]

    reset()
    qw = lax.broadcasted_iota(jnp.int32, (TQ, WT), 0)
    kw_i = lax.broadcasted_iota(jnp.int32, (TQ, WT), 1)
    n_win = WINDOW // WT
    for jt in range(n_win + 1):
        if jt == 0:
            mask = kw_i > qw
        elif jt == n_win:
            mask = kw_i <= qw
        else:
            mask = None
        off = (jt - n_win) * WT

        @pl.when(s0 + off >= 0)
        def _():
            k0 = pl.multiple_of(s0 + off, WT)
            rel = (off + kw_i - qw).astype(F32)
            attend(kw_ref[0, 0, pl.ds(k0, WT), :], vw_ref[0, 0, pl.ds(k0, WT), :], mask, rel)

    outs = []
    for h in range(HG):
        outs.append(oc_ref[:, h * DH:(h + 1) * DH]
                    + gt[:, 3 * h + 1:3 * h + 2] * o_sel[h]
                    + gt[:, 3 * h + 2:3 * h + 3] * result(h))
    o_ref[...] = jnp.concatenate(outs, axis=-1)


def _nsa_main(act, q, sel, oc, gates, slope_tab, expand, ksl, vsl, kw, vw, B, T):
    G, HG, DH, TQ = NSA_GROUPS, NSA_HPG, NSA_DH, NSA_Q_TILE
    nq = T // TQ
    n_slc = sel.shape[3]
    n_ktiles = T // NSA_KEY_TILE
    kv = lambda last: pl.BlockSpec((1, 1, T, last), lambda b, g, i, a: (b, g, 0, 0))
    grid_spec = pltpu.PrefetchScalarGridSpec(
        num_scalar_prefetch=1,
        grid=(B, G, nq),
        in_specs=[pl.BlockSpec((1, 1, HG, TQ, DH), lambda b, g, i, a: (b, g, 0, i, 0)),
                  pl.BlockSpec((1, 1, TQ, n_slc), lambda b, g, i, a: (b, g, i, 0)),
                  pl.BlockSpec((TQ, HG * DH), lambda b, g, i, a: (b * nq + i, g)),
                  pl.BlockSpec((1, 1, TQ, HG * 3), lambda b, g, i, a: (b, g, i, 0)),
                  pl.BlockSpec((1, HG * TQ, LANES), lambda b, g, i, a: (g, 0, 0)),
                  pl.BlockSpec((n_slc, T), lambda b, g, i, a: (0, 0)),
                  kv(DH), kv(LANES), kv(DH), kv(LANES)],
        out_specs=pl.BlockSpec((TQ, HG * DH), lambda b, g, i, a: (b * nq + i, g)),
        scratch_shapes=[pltpu.VMEM((HG * TQ, LANES), F32), pltpu.VMEM((HG * TQ, LANES), F32)],
    )
    return pl.pallas_call(
        functools.partial(_nsa_main_kernel, n_ktiles=n_ktiles),
        grid_spec=grid_spec,
        out_shape=jax.ShapeDtypeStruct((B * T, NSA_HEADS * DH), F32),
        compiler_params=_cparams("parallel", "parallel", "arbitrary"),
        name="nsa_select_window",
    )(act, q, sel, oc, gates, slope_tab, expand, ksl, vsl, kw, vw)


def _nsa_mixer(h_proj, B, T, q_norm, kcmp_norm, kslc_norm, kwin_norm, pos_k, pos_v,
               k_w1, k_b1, k_w2, v_w1, v_b1, v_w2):
    G, HG, DH, TQ = NSA_GROUPS, NSA_HPG, NSA_DH, NSA_Q_TILE
    q, kc, vc, ksl, vsl, kw, vw, gates = _nsa_prep(h_proj, q_norm, kslc_norm, kwin_norm, B, T)

    nrow = T // CMP_STRIDE
    half = CMP_STRIDE * DH
    kcmp, vcmp = _compress(kc.reshape(B, G, nrow, half), vc.reshape(B, G, nrow, half),
                           pos_k, pos_v, k_w1, k_b1, k_w2, v_w1, v_b1, v_w2, kcmp_norm)

    n_slc = T // SLC_BLOCK
    n_top = min(SLC_TOP, n_slc)
    n_slc_pad = max(n_slc, LANES)
    slopes = 2.0 ** (-8.0 * jnp.arange(1, NSA_HEADS + 1, dtype=F32) / NSA_HEADS) * LOG2E
    slope_tab = jnp.broadcast_to(slopes.reshape(G, HG, 1, 1), (G, HG, TQ, LANES)).reshape(G, HG * TQ, LANES)
    cj = jnp.arange(nrow)[:, None] * CMP_STRIDE
    si = jnp.arange(n_slc_pad)[None, :] * SLC_BLOCK
    overlap = ((cj <= si + SLC_BLOCK - 1) & (cj + CMP_BLOCK - 1 >= si)
               & (jnp.arange(nrow)[:, None] < nrow - 1)).astype(BF16)
    blocks_per_tile = NSA_KEY_TILE // SLC_BLOCK
    group_mat = (jnp.arange(n_slc_pad)[:, None] // blocks_per_tile
                 == jnp.arange(LANES)[None, :]).astype(BF16)
    expand = (jnp.arange(n_slc_pad)[:, None] == jnp.arange(T)[None, :] // SLC_BLOCK).astype(BF16)

    oc, sel, act = _nsa_cmp(q, kcmp, vcmp, gates, slope_tab, overlap, group_mat, B, T, n_top)
    n_ktiles = T // NSA_KEY_TILE
    act_i = act[:, :, :, 0, :n_ktiles].astype(jnp.int32).reshape(-1)
    return _nsa_main(act_i, q, sel, oc, gates, slope_tab, expand, ksl, vsl, kw, vw, B, T)


def _rope(r, cos2, sin2):
    half = QK_ROPE // 2
    rot = jnp.concatenate([r[:, half:], r[:, :half]], axis=-1)
    return r * cos2 + rot * sin2


def _mla_q_kernel(q_ref, g_ref, cos_ref, sin_ref, o_ref):
    scale = MLA_QK ** -0.5 * LOG2E
    for h in range(MLA_HEADS):
        qh = _rms(q_ref[:, h * MLA_QK:(h + 1) * MLA_QK], g_ref[...]) * scale
        roped = _rope(qh[:, QK_NOPE:], cos_ref[...], sin_ref[...])
        o_ref[0, h] = jnp.concatenate([qh[:, :QK_NOPE], roped], axis=-1).astype(BF16)


def _mla_kv_kernel(kv_ref, kpe_ref, g_ref, cos_ref, sin_ref, k_ref, v_ref):
    kpe = kpe_ref[:, :QK_ROPE]
    for h in range(MLA_HEADS):
        base = h * (QK_NOPE + MLA_V)
        kh = _rms(jnp.concatenate([kv_ref[:, base:base + QK_NOPE], kpe], axis=-1), g_ref[...])
        roped = _rope(kh[:, QK_NOPE:], cos_ref[...], sin_ref[...])
        k_ref[0, h] = jnp.concatenate([kh[:, :QK_NOPE], roped], axis=-1).astype(BF16)
        v_ref[0, h] = kv_ref[:, base + QK_NOPE:base + QK_NOPE + MLA_V].astype(BF16)


def _mla_prep_q(q, gain, cos2, sin2, B, T, *, tm=256):
    nt = T // tm
    return pl.pallas_call(
        _mla_q_kernel,
        grid=(B * nt,),
        in_specs=[pl.BlockSpec((tm, MLA_HEADS * MLA_QK), lambda i: (i, 0)),
                  pl.BlockSpec((1, MLA_QK), lambda i: (0, 0)),
                  pl.BlockSpec((tm, QK_ROPE), lambda i: (i % nt, 0)),
                  pl.BlockSpec((tm, QK_ROPE), lambda i: (i % nt, 0))],
        out_specs=pl.BlockSpec((1, MLA_HEADS, tm, MLA_QK), lambda i: (i // nt, 0, i % nt, 0)),
        out_shape=jax.ShapeDtypeStruct((B, MLA_HEADS, T, MLA_QK), BF16),
        compiler_params=_cparams("parallel"),
        name="mla_prep_q",
    )(q, gain.reshape(1, MLA_QK), cos2, sin2)


def _mla_prep_kv(kv, kv_a, gain, cos2, sin2, B, T, *, tm=256):
    nt = T // tm
    pe_block = KV_LORA // LANES
    return pl.pallas_call(
        _mla_kv_kernel,
        grid=(B * nt,),
        in_specs=[pl.BlockSpec((tm, MLA_HEADS * (QK_NOPE + MLA_V)), lambda i: (i, 0)),
                  pl.BlockSpec((tm, LANES), lambda i: (i, pe_block)),
                  pl.BlockSpec((1, MLA_QK), lambda i: (0, 0)),
                  pl.BlockSpec((tm, QK_ROPE), lambda i: (i % nt, 0)),
                  pl.BlockSpec((tm, QK_ROPE), lambda i: (i % nt, 0))],
        out_specs=[pl.BlockSpec((1, MLA_HEADS, tm, MLA_QK), lambda i: (i // nt, 0, i % nt, 0)),
                   pl.BlockSpec((1, MLA_HEADS, tm, MLA_V), lambda i: (i // nt, 0, i % nt, 0))],
        out_shape=[jax.ShapeDtypeStruct((B, MLA_HEADS, T, MLA_QK), BF16),
                   jax.ShapeDtypeStruct((B, MLA_HEADS, T, MLA_V), BF16)],
        compiler_params=_cparams("parallel"),
        name="mla_prep_kv",
    )(kv, kv_a, gain.reshape(1, MLA_QK), cos2, sin2)


def _mla_attn_kernel(q_ref, k_ref, v_ref, o_ref, m_sc, l_sc, acc_sc):
    TQ = MLA_TILE
    qi = pl.program_id(2)
    q = q_ref[0, 0]
    m_sc[...] = jnp.full_like(m_sc, NEG_INF)
    l_sc[...] = jnp.zeros_like(l_sc)
    acc_sc[...] = jnp.zeros_like(acc_sc)

    def step(i, masked):
        k0 = pl.multiple_of(i * TQ, TQ)
        s = _dot_nt(q, k_ref[0, 0, pl.ds(k0, TQ), :])
        if masked:
            qq = lax.broadcasted_iota(jnp.int32, (TQ, TQ), 0)
            kk = lax.broadcasted_iota(jnp.int32, (TQ, TQ), 1)
            s = jnp.where(kk <= qq, s, NEG_INF)
        m_old = m_sc[...]
        m_new = jnp.maximum(m_old, jnp.max(s, axis=-1, keepdims=True))
        p = jnp.exp2(s - m_new[:, 0:1])
        alpha = jnp.exp2(m_old - m_new)
        l_sc[...] = alpha * l_sc[...] + jnp.sum(p, axis=-1, keepdims=True)
        acc_sc[...] = alpha * acc_sc[...] + _dot(p.astype(BF16), v_ref[0, 0, pl.ds(k0, TQ), :])
        m_sc[...] = m_new

    def body(i, carry):
        step(i, False)
        return carry

    lax.fori_loop(0, qi, body, 0)
    step(qi, True)
    o_ref[...] = acc_sc[...] / l_sc[...]


def _mla_attn(q, k, v, B, T):
    H, TQ = MLA_HEADS, MLA_TILE
    nq = T // TQ
    return pl.pallas_call(
        _mla_attn_kernel,
        grid=(B, H, nq),
        in_specs=[pl.BlockSpec((1, 1, TQ, MLA_QK), lambda b, h, i: (b, h, i, 0)),
                  pl.BlockSpec((1, 1, T, MLA_QK), lambda b, h, i: (b, h, 0, 0)),
                  pl.BlockSpec((1, 1, T, MLA_V), lambda b, h, i: (b, h, 0, 0))],
        out_specs=pl.BlockSpec((TQ, MLA_V), lambda b, h, i: (b * nq + i, h)),
        out_shape=jax.ShapeDtypeStruct((B * T, H * MLA_V), F32),
        scratch_shapes=[pltpu.VMEM((TQ, LANES), F32), pltpu.VMEM((TQ, LANES), F32),
                        pltpu.VMEM((TQ, MLA_V), F32)],
        compiler_params=_cparams("parallel", "parallel", "arbitrary"),
        name="mla_flash_attn",
    )(q, k, v)


def _pad_cols(w, mult=LANES):
    pad = -w.shape[1] % mult
    return jnp.pad(w, ((0, 0), (0, pad))) if pad else w


def kernel(x, a_attn_norm, a_w_in, a_q_norm, a_kcmp_norm, a_kslc_norm, a_kwin_norm, a_cmp_pos_k, a_cmp_pos_v, a_cmp_k_w1, a_cmp_k_b1, a_cmp_k_w2, a_cmp_v_w1, a_cmp_v_b1, a_cmp_v_w2, a_w_out, kv_norm, kv_w_a, kv_c_norm, kv_w_b, kv_k_norm, b_attn_norm, b_w_q_a, b_q_a_norm, b_w_q_b, b_q_norm, b_w_out, ffn_norm, ffn_w_gate_up, ffn_w_down):
    B, T, D = x.shape
    n_a = a_w_in.shape[0]
    n_b = b_w_q_a.shape[0]
    xs = x.reshape(B * T, D)

    inv = ROPE_THETA ** (-jnp.arange(0, QK_ROPE, 2, dtype=F32) / QK_ROPE)
    ang = jnp.arange(T, dtype=F32)[:, None] * inv[None, :]
    cos, sin = jnp.cos(ang), jnp.sin(ang)
    cos2 = jnp.concatenate([cos, cos], axis=-1)
    sin2 = jnp.concatenate([-sin, sin], axis=-1)

    k_shared = v_shared = None
    for layer in range(n_a + n_b):
        if layer < n_a:
            i = layer
            proj = _linear(xs, _pad_cols(a_w_in[i]).astype(BF16), gain=a_attn_norm[i], name="nsa_in_proj")
            o = _nsa_mixer(proj, B, T, a_q_norm[i], a_kcmp_norm[i], a_kslc_norm[i], a_kwin_norm[i],
                           a_cmp_pos_k[i], a_cmp_pos_v[i], a_cmp_k_w1[i], a_cmp_k_b1[i], a_cmp_k_w2[i],
                           a_cmp_v_w1[i], a_cmp_v_b1[i], a_cmp_v_w2[i])
            xs = _linear(o, a_w_out[i].astype(BF16), residual=xs, name="nsa_out_proj")
        else:
            j = layer - n_a
            qa = _linear(xs, b_w_q_a[j].astype(BF16), gain=b_attn_norm[j], name="mla_q_a_proj")
            qb = _linear(qa, b_w_q_b[j].astype(BF16), gain=b_q_a_norm[j], name="mla_q_b_proj")
            q = _mla_prep_q(qb, b_q_norm[j], cos2, sin2, B, T)
            o = _mla_attn(q, k_shared, v_shared, B, T)
            xs = _linear(o, b_w_out[j].astype(BF16), residual=xs, name="mla_out_proj")
        xs = _ffn(xs, ffn_norm[layer], ffn_w_gate_up[layer].astype(BF16), ffn_w_down[layer].astype(BF16))
        if layer == n_a - 1:
            kv_a = _linear(xs, _pad_cols(kv_w_a).astype(BF16), gain=kv_norm, name="mla_kv_a_proj")
            kv = _linear(kv_a, kv_w_b.astype(BF16), gain=kv_c_norm, k_cols=KV_LORA, name="mla_kv_b_proj")
            k_shared, v_shared = _mla_prep_kv(kv, kv_a, kv_k_norm, cos2, sin2, B, T)
    return xs.reshape(B, T, D)
```

```python
import functools

import jax
import jax.numpy as jnp
from jax import lax
from jax.experimental import pallas as pl
from jax.experimental.pallas import tpu as pltpu

F32 = jnp.float32
BF16 = jnp.bfloat16

NORM_EPS = 1e-6
NEG_INF = -1e30
LANES = 128

NSA_HEADS = 16
NSA_GROUPS = 4
NSA_HPG = NSA_HEADS // NSA_GROUPS
NSA_DH = 64
CMP_BLOCK = 32
CMP_STRIDE = 16
CMP_HIDDEN = 256
SLC_BLOCK = 64
SLC_TOP = 16
WINDOW = 512
FORCE_SCORE = 1e4
NSA_Q_TILE = 256
NSA_KEY_TILE = 256
LOG2E = 1.4426950408889634

MLA_HEADS = 8
QK_NOPE = 128
QK_ROPE = 64
MLA_QK = QK_NOPE + QK_ROPE
MLA_V = 128
Q_LORA = 384
KV_LORA = 256
ROPE_THETA = 10000.0
MLA_TILE = 512

VMEM_LIMIT = 56 * 1024 * 1024


def _cparams(*sem):
    return pltpu.CompilerParams(dimension_semantics=sem, vmem_limit_bytes=VMEM_LIMIT)


def _rms(x, g):
    return x * lax.rsqrt(jnp.mean(x * x, axis=-1, keepdims=True) + NORM_EPS) * g


def _dot(a, b):
    return jnp.dot(a, b, preferred_element_type=F32)


def _dot_nt(a, b):
    return lax.dot_general(a, b, (((1,), (1,)), ((), ())), preferred_element_type=F32)


def _linear_kernel(*refs, has_gain, has_res):
    it = iter(refs)
    x_ref = next(it)
    w_ref = next(it)
    g_ref = next(it) if has_gain else None
    r_ref = next(it) if has_res else None
    o_ref = next(it)
    x = x_ref[...]
    if has_gain:
        x = _rms(x.astype(F32), g_ref[...])
    acc = _dot(x.astype(BF16), w_ref[...])
    if has_res:
        acc = acc + r_ref[...]
    o_ref[...] = acc.astype(o_ref.dtype)


def _linear(x, w, gain=None, residual=None, *, name, k_cols=None, tm=512, tn=None, out_dtype=F32):
    n = x.shape[0]
    k, m = w.shape
    if k_cols is None:
        assert x.shape[1] == k
    tn = m if tn is None else tn
    assert n % tm == 0 and m % tn == 0
    in_specs = [pl.BlockSpec((tm, k), lambda i, j: (i, 0)),
                pl.BlockSpec((k, tn), lambda i, j: (0, j))]
    args = [x, w]
    if gain is not None:
        in_specs.append(pl.BlockSpec((1, k), lambda i, j: (0, 0)))
        args.append(gain.reshape(1, k).astype(F32))
    if residual is not None:
        in_specs.append(pl.BlockSpec((tm, tn), lambda i, j: (i, j)))
        args.append(residual)
    return pl.pallas_call(
        functools.partial(_linear_kernel, has_gain=gain is not None, has_res=residual is not None),
        grid=(n // tm, m // tn),
        in_specs=in_specs,
        out_specs=pl.BlockSpec((tm, tn), lambda i, j: (i, j)),
        out_shape=jax.ShapeDtypeStruct((n, m), out_dtype),
        compiler_params=_cparams("parallel", "arbitrary"),
        name=name,
    )(*args)


def _ffn_kernel(x_ref, g_ref, wg_ref, wu_ref, wd_ref, o_ref, hn_sc, acc_sc):
    j = pl.program_id(1)

    @pl.when(j == 0)
    def _():
        hn_sc[...] = _rms(x_ref[...], g_ref[...]).astype(BF16)
        acc_sc[...] = jnp.zeros_like(acc_sc)

    h = hn_sc[...]
    gate = _dot(h, wg_ref[...])
    up = _dot(h, wu_ref[...])
    a = gate * jax.nn.sigmoid(gate) * up
    acc_sc[...] += _dot(a.astype(BF16), wd_ref[...])

    @pl.when(j == pl.num_programs(1) - 1)
    def _():
        o_ref[...] = x_ref[...] + acc_sc[...]


def _ffn(x, gain, w_gate_up, w_down, *, tm=1024, tc=256):
    n, d = x.shape
    hid = w_down.shape[0]
    assert n % tm == 0 and hid % tc == 0
    nc = hid // tc
    return pl.pallas_call(
        _ffn_kernel,
        grid=(n // tm, nc),
        in_specs=[pl.BlockSpec((tm, d), lambda i, j: (i, 0)),
                  pl.BlockSpec((1, d), lambda i, j: (0, 0)),
                  pl.BlockSpec((d, tc), lambda i, j: (0, j)),
                  pl.BlockSpec((d, tc), lambda i, j: (0, nc + j)),
                  pl.BlockSpec((tc, d), lambda i, j: (j, 0))],
        out_specs=pl.BlockSpec((tm, d), lambda i, j: (i, 0)),
        out_shape=jax.ShapeDtypeStruct((n, d), F32),
        scratch_shapes=[pltpu.VMEM((tm, d), BF16), pltpu.VMEM((tm, d), F32)],
        compiler_params=_cparams("parallel", "arbitrary"),
        name="swiglu_ffn",
    )(x, gain.reshape(1, d), w_gate_up, w_gate_up, w_down)


def _nsa_prep_kernel(p_ref, qn_ref, ksn_ref, kwn_ref,
                     q_ref, kc_ref, vc_ref, ksl_ref, vsl_ref, kw_ref, vw_ref, gt_ref):
    G, HG, DH = NSA_GROUPS, NSA_HPG, NSA_DH
    qw, kvw = NSA_HEADS * DH, G * DH
    scale = DH ** -0.5 * LOG2E
    tm = p_ref.shape[0]
    ones_col = (lax.broadcasted_iota(jnp.int32, (tm, LANES - DH), 1) == 0).astype(F32)

    def piece(base, g):
        return p_ref[:, base + g * DH: base + (g + 1) * DH]

    for g in range(G):
        for h in range(HG):
            qh = p_ref[:, (g * HG + h) * DH:(g * HG + h + 1) * DH]
            q_ref[0, g, h] = (_rms(qh, qn_ref[...]) * scale).astype(BF16)
        kc_ref[0, g] = piece(qw, g)
        vc_ref[0, g] = piece(qw + kvw, g)
        ksl_ref[0, g] = _rms(piece(qw + 2 * kvw, g), ksn_ref[...]).astype(BF16)
        vsl_ref[0, g] = jnp.concatenate([piece(qw + 3 * kvw, g), ones_col], axis=-1).astype(BF16)
        kw_ref[0, g] = _rms(piece(qw + 4 * kvw, g), kwn_ref[...]).astype(BF16)
        vw_ref[0, g] = jnp.concatenate([piece(qw + 5 * kvw, g), ones_col], axis=-1).astype(BF16)
        gb = qw + 6 * kvw + g * HG * 3
        gt_ref[0, g] = jax.nn.sigmoid(p_ref[:, gb:gb + HG * 3])


def _nsa_prep(p, q_norm, kslc_norm, kwin_norm, B, T, *, tm=256):
    G, HG, DH = NSA_GROUPS, NSA_HPG, NSA_DH
    nt = T // tm
    width = p.shape[1]
    gain = lambda: pl.BlockSpec((1, DH), lambda i: (0, 0))
    per_group = lambda last: pl.BlockSpec((1, G, tm, last), lambda i: (i // nt, 0, i % nt, 0))
    shp = lambda last, dt: jax.ShapeDtypeStruct((B, G, T, last), dt)
    return pl.pallas_call(
        _nsa_prep_kernel,
        grid=(B * nt,),
        in_specs=[pl.BlockSpec((tm, width), lambda i: (i, 0)), gain(), gain(), gain()],
        out_specs=[pl.BlockSpec((1, G, HG, tm, DH), lambda i: (i // nt, 0, 0, i % nt, 0)),
                   per_group(DH), per_group(DH), per_group(DH), per_group(LANES),
                   per_group(DH), per_group(LANES), per_group(HG * 3)],
        out_shape=[jax.ShapeDtypeStruct((B, G, HG, T, DH), BF16),
                   shp(DH, F32), shp(DH, F32), shp(DH, BF16), shp(LANES, BF16),
                   shp(DH, BF16), shp(LANES, BF16), shp(HG * 3, F32)],
        compiler_params=_cparams("parallel"),
        name="nsa_prep",
    )(p, q_norm.reshape(1, DH), kslc_norm.reshape(1, DH), kwin_norm.reshape(1, DH))


def _compress_kernel(rk_ref, rv_ref, pek_ref, pev_ref, kw1_ref, kb1_ref, kw2_ref,
                     vw1_ref, vb1_ref, vw2_ref, kn_ref, kcmp_ref, vcmp_ref):
    half = CMP_STRIDE * NSA_DH
    nrow = rk_ref.shape[2]

    def mlp(r_ref, pe_ref, w1_ref, b1_ref, w2_ref):
        r = r_ref[0, 0].astype(BF16)
        ya = _dot(r, w1_ref[:half, :])
        yb = _dot(r, w1_ref[half:, :])
        pe = jnp.broadcast_to(pe_ref[...], (8, 2 * half)).astype(BF16)
        c = _dot(pe, w1_ref[...])[0:1] + b1_ref[...]
        hid = ya + pltpu.roll(yb, nrow - 1, 0) + c
        return _dot(jax.nn.gelu(hid).astype(BF16), w2_ref[...])

    kcmp_ref[0, 0] = _rms(mlp(rk_ref, pek_ref, kw1_ref, kb1_ref, kw2_ref), kn_ref[...]).astype(BF16)
    vcmp_ref[0, 0] = mlp(rv_ref, pev_ref, vw1_ref, vb1_ref, vw2_ref).astype(BF16)


def _compress(rk, rv, pe_k, pe_v, k_w1, k_b1, k_w2, v_w1, v_b1, v_w2, kcmp_norm):
    B, G, nrow, half = rk.shape
    DH, HID = NSA_DH, CMP_HIDDEN
    full = lambda shape: pl.BlockSpec(shape, lambda b, g: (0,) * len(shape))
    r_spec = pl.BlockSpec((1, 1, nrow, half), lambda b, g: (b, g, 0, 0))
    o_spec = pl.BlockSpec((1, 1, nrow, DH), lambda b, g: (b, g, 0, 0))
    return pl.pallas_call(
        _compress_kernel,
        grid=(B, G),
        in_specs=[r_spec, r_spec, full((1, 2 * half)), full((1, 2 * half)),
                  full((2 * half, HID)), full((1, HID)), full((HID, DH)),
                  full((2 * half, HID)), full((1, HID)), full((HID, DH)), full((1, DH))],
        out_specs=[o_spec, o_spec],
        out_shape=[jax.ShapeDtypeStruct((B, G, nrow, DH), BF16)] * 2,
        compiler_params=_cparams("parallel", "parallel"),
        name="nsa_compress",
    )(rk, rv, pe_k.reshape(1, -1), pe_v.reshape(1, -1),
      k_w1.astype(BF16), k_b1.reshape(1, HID), k_w2.astype(BF16),
      v_w1.astype(BF16), v_b1.reshape(1, HID), v_w2.astype(BF16), kcmp_norm.reshape(1, DH))


def _split3(x):
    hi = x.astype(BF16)
    r1 = x - hi.astype(F32)
    mid = r1.astype(BF16)
    lo = (r1 - mid.astype(F32)).astype(BF16)
    return hi, mid, lo


def _nsa_cmp_kernel(q_ref, kc_ref, vc_ref, gt_ref, sl_ref, ov_ref, gm_ref,
                    oc_ref, sel_ref, act_ref, imp_sc, *, n_top):
    HG, DH, TQ = NSA_HPG, NSA_DH, NSA_Q_TILE
    s0 = pl.program_id(2) * TQ
    ncmp = kc_ref.shape[2]
    n_slc = ov_ref.shape[1]
    qs = q_ref[0, 0].reshape(HG * TQ, DH)
    gt = gt_ref[0, 0]

    def attend(nc):
        s = _dot_nt(qs, kc_ref[0, 0, :nc, :])
        j = lax.broadcasted_iota(jnp.int32, (1, nc), 1)
        t = s0 + lax.broadcasted_iota(jnp.int32, (TQ, 1), 0)
        mask = j * CMP_STRIDE + (CMP_BLOCK - 1) <= t
        mid = (j * CMP_STRIDE - s0).astype(F32) + 0.5 * (CMP_BLOCK - 1)
        has_key = t >= CMP_BLOCK - 1
        psum = jnp.zeros((TQ, nc), F32)
        outs = []
        for h in range(HG):
            rows = slice(h * TQ, (h + 1) * TQ)
            slope_row = jnp.concatenate([sl_ref[0, h * TQ:h * TQ + 1, :]] * (nc // LANES), axis=-1)
            sh = jnp.where(mask, s[rows] + slope_row * mid, NEG_INF)
            m = jnp.max(sh, axis=-1, keepdims=True)
            e = jnp.exp2(sh - m)
            l = jnp.sum(e, axis=-1, keepdims=True)
            inv = jnp.where(has_key, 1.0 / l, 0.0)
            p = e * inv
            psum = psum + p
            outs.append(_dot(p.astype(BF16), vc_ref[0, 0, :nc, :]) * gt[:, 3 * h:3 * h + 1])
        oc_ref[...] = jnp.concatenate(outs, axis=-1)
        hi, mid, lo = _split3(psum)
        ov = ov_ref[:nc, :]
        imp_sc[...] = _dot(hi, ov) + _dot(mid, ov) + _dot(lo, ov)

    need = (s0 + TQ) // CMP_STRIDE
    widths = list(range(LANES, ncmp + 1, LANES))
    for idx, nc in enumerate(widths):
        lo_w = widths[idx - 1] if idx else 0
        pl.when((need > lo_w) & (need <= nc))(functools.partial(attend, nc))

    imp_t = imp_sc[...].T
    blk = lax.broadcasted_iota(jnp.int32, (n_slc, TQ), 0)
    tq = s0 + lax.broadcasted_iota(jnp.int32, (n_slc, TQ), 1)
    cur = lax.shift_right_logical(tq, SLC_BLOCK.bit_length() - 1)
    forced = (blk == 0) | (blk == cur) | (blk == cur - 1)
    score = jnp.where(blk * SLC_BLOCK <= tq, imp_t + jnp.where(forced, FORCE_SCORE, 0.0), -1.0)

    def pick(_, carry):
        score, sel = carry
        mx = jnp.max(score, axis=0, keepdims=True)
        first = jnp.min(jnp.where(score == mx, blk, n_slc), axis=0, keepdims=True)
        hit = blk == first
        return jnp.where(hit, -2.0, score), jnp.where(hit, 1.0, sel)

    _, sel_t = lax.fori_loop(0, n_top, pick, (score, jnp.zeros((n_slc, TQ), F32)), unroll=True)
    sel = sel_t.T.astype(BF16)
    sel_ref[0, 0] = sel
    cnt = _dot(jnp.ones((8, TQ), BF16), sel)
    act_ref[0, 0, 0] = _dot((cnt > 0.0).astype(BF16), gm_ref[...])


def _nsa_cmp(q, kcmp, vcmp, gates, slope_tab, overlap, group_mat, B, T, n_top):
    G, HG, DH, TQ = NSA_GROUPS, NSA_HPG, NSA_DH, NSA_Q_TILE
    nq = T // TQ
    ncmp = kcmp.shape[2]
    n_slc = overlap.shape[1]
    return pl.pallas_call(
        functools.partial(_nsa_cmp_kernel, n_top=n_top),
        grid=(B, G, nq),
        in_specs=[pl.BlockSpec((1, 1, HG, TQ, DH), lambda b, g, i: (b, g, 0, i, 0)),
                  pl.BlockSpec((1, 1, ncmp, DH), lambda b, g, i: (b, g, 0, 0)),
                  pl.BlockSpec((1, 1, ncmp, DH), lambda b, g, i: (b, g, 0, 0)),
                  pl.BlockSpec((1, 1, TQ, HG * 3), lambda b, g, i: (b, g, i, 0)),
                  pl.BlockSpec((1, HG * TQ, LANES), lambda b, g, i: (g, 0, 0)),
                  pl.BlockSpec((ncmp, n_slc), lambda b, g, i: (0, 0)),
                  pl.BlockSpec((n_slc, LANES), lambda b, g, i: (0, 0))],
        out_specs=[pl.BlockSpec((TQ, HG * DH), lambda b, g, i: (b * nq + i, g)),
                   pl.BlockSpec((1, 1, TQ, n_slc), lambda b, g, i: (b, g, i, 0)),
                   pl.BlockSpec((1, 1, 1, 8, LANES), lambda b, g, i: (b, g, i, 0, 0))],
        out_shape=[jax.ShapeDtypeStruct((B * T, NSA_HEADS * DH), F32),
                   jax.ShapeDtypeStruct((B, G, T, n_slc), BF16),
                   jax.ShapeDtypeStruct((B, G, nq, 8, LANES), F32)],
        scratch_shapes=[pltpu.VMEM((TQ, n_slc), F32)],
        compiler_params=_cparams("parallel", "parallel", "arbitrary"),
        name="nsa_cmp_select",
    )(q, kcmp, vcmp, gates, slope_tab, overlap, group_mat)


def _nsa_main_kernel(act_ref, q_ref, sel_ref, oc_ref, gt_ref, sl_ref, ex_ref,
                     ksl_ref, vsl_ref, kw_ref, vw_ref, o_ref, m_sc, acc_sc, list_sc, *, n_ktiles):
    HG, DH, TQ, KT = NSA_HPG, NSA_DH, NSA_Q_TILE, NSA_KEY_TILE
    WK = WINDOW + TQ
    b, g, qi = pl.program_id(0), pl.program_id(1), pl.program_id(2)
    nq = pl.num_programs(2)
    s0 = qi * TQ
    qs = q_ref[0, 0].reshape(HG * TQ, DH)
    selb = sel_ref[0, 0]
    gt = gt_ref[0, 0]
    t_col = s0 + lax.broadcasted_iota(jnp.int32, (TQ, 1), 0)

    def biased(s, kpos, mask, h):
        width = kpos.shape[1]
        slope_row = jnp.concatenate([sl_ref[0, h * TQ:h * TQ + 1, :]] * (width // LANES), axis=-1)
        sh = s[h * TQ:(h + 1) * TQ] + slope_row * (kpos - s0).astype(F32)
        return jnp.where(mask, sh, NEG_INF)

    m_sc[...] = jnp.full_like(m_sc, NEG_INF)
    acc_sc[...] = jnp.zeros_like(acc_sc)
    act_base = ((b * NSA_GROUPS + g) * nq + qi) * n_ktiles
    list_sc[0] = 0

    def compact(i, n):
        hit = act_ref[act_base + i] > 0

        @pl.when(hit)
        def _():
            list_sc[n] = i
        return n + hit.astype(jnp.int32)

    n_act = jnp.maximum(lax.fori_loop(0, qi + 1, compact, 0), 1)

    def scores(j):
        k0 = pl.multiple_of(list_sc[j] * KT, KT)
        return (_dot_nt(qs, ksl_ref[0, 0, pl.ds(k0, KT), :]),
                _dot(selb, ex_ref[:, pl.ds(k0, KT)]))

    def update(j, s, picked):
        k0 = pl.multiple_of(list_sc[j] * KT, KT)
        kpos = k0 + lax.broadcasted_iota(jnp.int32, (1, KT), 1)
        mask = (picked > 0.5) & (kpos <= t_col)
        ps, alphas = [], []
        for h in range(HG):
            rows = slice(h * TQ, (h + 1) * TQ)
            sh = biased(s, kpos, mask, h)
            m_old = m_sc[rows]
            m_new = jnp.maximum(m_old, jnp.max(sh, axis=-1, keepdims=True))
            ps.append(jnp.exp2(sh - jnp.concatenate([m_new] * (KT // LANES), axis=-1)).astype(BF16))
            alphas.append(jnp.exp2(m_old - m_new))
            m_sc[rows] = m_new
        pv = _dot(jnp.concatenate(ps, axis=0), vsl_ref[0, 0, pl.ds(k0, KT), :])
        acc_sc[...] = jnp.concatenate(alphas, axis=0) * acc_sc[...] + pv

    def body(j, carry):
        nxt = scores(j + 1)
        update(j, *carry)
        return nxt

    last = lax.fori_loop(0, n_act - 1, body, scores(0))
    update(n_act - 1, *last)

    start = pl.multiple_of(jnp.maximum(s0 - WINDOW, 0), TQ)
    kpos_w = start + lax.broadcasted_iota(jnp.int32, (1, WK), 1)
    mask_w = (kpos_w <= t_col) & (kpos_w > t_col - WINDOW)
    s_w = _dot_nt(qs, kw_ref[0, 0, pl.ds(start, WK), :])
    pw = []
    for h in range(HG):
        sh = biased(s_w, kpos_w, mask_w, h)
        pw.append(jnp.exp2(sh - jnp.max(sh, axis=-1, keepdims=True)).astype(BF16))
    acc_w = _dot(jnp.concatenate(pw, axis=0), vw_ref[0, 0, pl.ds(start, WK), :])

    outs = []
    for h in range(HG):
        rows = slice(h * TQ, (h + 1) * TQ)
        a_s, a_w = acc_sc[rows], acc_w[rows]
        outs.append(oc_ref[:, h * DH:(h + 1) * DH]
                    + gt[:, 3 * h + 1:3 * h + 2] * (a_s[:, :DH] / a_s[:, DH:DH + 1])
                    + gt[:, 3 * h + 2:3 * h + 3] * (a_w[:, :DH] / a_w[:, DH:DH + 1]))
    o_ref[...] = jnp.concatenate(outs, axis=-1)


def _nsa_main(act, q, sel, oc, gates, slope_tab, expand, ksl, vsl, kw, vw, B, T):
    G, HG, DH, TQ = NSA_GROUPS, NSA_HPG, NSA_DH, NSA_Q_TILE
    nq = T // TQ
    n_slc = sel.shape[3]
    n_ktiles = T // NSA_KEY_TILE
    kv = lambda last: pl.BlockSpec((1, 1, T, last), lambda b, g, i, a: (b, g, 0, 0))
    grid_spec = pltpu.PrefetchScalarGridSpec(
        num_scalar_prefetch=1,
        grid=(B, G, nq),
        in_specs=[pl.BlockSpec((1, 1, HG, TQ, DH), lambda b, g, i, a: (b, g, 0, i, 0)),
                  pl.BlockSpec((1, 1, TQ, n_slc), lambda b, g, i, a: (b, g, i, 0)),
                  pl.BlockSpec((TQ, HG * DH), lambda b, g, i, a: (b * nq + i, g)),
                  pl.BlockSpec((1, 1, TQ, HG * 3), lambda b, g, i, a: (b, g, i, 0)),
                  pl.BlockSpec((1, HG * TQ, LANES), lambda b, g, i, a: (g, 0, 0)),
                  pl.BlockSpec((n_slc, T), lambda b, g, i, a: (0, 0)),
                  kv(DH), kv(LANES), kv(DH), kv(LANES)],
        out_specs=pl.BlockSpec((TQ, HG * DH), lambda b, g, i, a: (b * nq + i, g)),
        scratch_shapes=[pltpu.VMEM((HG * TQ, LANES), F32), pltpu.VMEM((HG * TQ, LANES), F32),
                        pltpu.SMEM((n_ktiles + 1,), jnp.int32)],
    )
    return pl.pallas_call(
        functools.partial(_nsa_main_kernel, n_ktiles=n_ktiles),
        grid_spec=grid_spec,
        out_shape=jax.ShapeDtypeStruct((B * T, NSA_HEADS * DH), F32),
        compiler_params=_cparams("parallel", "parallel", "arbitrary"),
        name="nsa_select_window",
    )(act, q, sel, oc, gates, slope_tab, expand, ksl, vsl, kw, vw)


def _nsa_mixer(h_proj, B, T, q_norm, kcmp_norm, kslc_norm, kwin_norm, pos_k, pos_v,
               k_w1, k_b1, k_w2, v_w1, v_b1, v_w2):
    G, HG, DH, TQ = NSA_GROUPS, NSA_HPG, NSA_DH, NSA_Q_TILE
    q, kc, vc, ksl, vsl, kw, vw, gates = _nsa_prep(h_proj, q_norm, kslc_norm, kwin_norm, B, T)

    nrow = T // CMP_STRIDE
    half = CMP_STRIDE * DH
    kcmp, vcmp = _compress(kc.reshape(B, G, nrow, half), vc.reshape(B, G, nrow, half),
                           pos_k, pos_v, k_w1, k_b1, k_w2, v_w1, v_b1, v_w2, kcmp_norm)

    n_slc = T // SLC_BLOCK
    n_top = min(SLC_TOP, n_slc)
    n_slc_pad = max(n_slc, LANES)
    slopes = 2.0 ** (-8.0 * jnp.arange(1, NSA_HEADS + 1, dtype=F32) / NSA_HEADS) * LOG2E
    slope_tab = jnp.broadcast_to(slopes.reshape(G, HG, 1, 1), (G, HG, TQ, LANES)).reshape(G, HG * TQ, LANES)
    cj = jnp.arange(nrow)[:, None] * CMP_STRIDE
    si = jnp.arange(n_slc_pad)[None, :] * SLC_BLOCK
    overlap = ((cj <= si + SLC_BLOCK - 1) & (cj + CMP_BLOCK - 1 >= si)
               & (jnp.arange(nrow)[:, None] < nrow - 1)).astype(BF16)
    blocks_per_tile = NSA_KEY_TILE // SLC_BLOCK
    group_mat = (jnp.arange(n_slc_pad)[:, None] // blocks_per_tile
                 == jnp.arange(LANES)[None, :]).astype(BF16)
    expand = (jnp.arange(n_slc_pad)[:, None] == jnp.arange(T)[None, :] // SLC_BLOCK).astype(BF16)

    oc, sel, act = _nsa_cmp(q, kcmp, vcmp, gates, slope_tab, overlap, group_mat, B, T, n_top)
    n_ktiles = T // NSA_KEY_TILE
    act_i = act[:, :, :, 0, :n_ktiles].astype(jnp.int32).reshape(-1)
    return _nsa_main(act_i, q, sel, oc, gates, slope_tab, expand, ksl, vsl, kw, vw, B, T)


def _rope(r, cos2, sin2):
    half = QK_ROPE // 2
    rot = jnp.concatenate([r[:, half:], r[:, :half]], axis=-1)
    return r * cos2 + rot * sin2


def _mla_q_kernel(q_ref, g_ref, cos_ref, sin_ref, o_ref):
    scale = MLA_QK ** -0.5 * LOG2E
    for h in range(MLA_HEADS):
        qh = _rms(q_ref[:, h * MLA_QK:(h + 1) * MLA_QK], g_ref[...]) * scale
        roped = _rope(qh[:, QK_NOPE:], cos_ref[...], sin_ref[...])
        o_ref[0, h] = jnp.concatenate([qh[:, :QK_NOPE], roped], axis=-1).astype(BF16)


def _mla_kv_kernel(kv_ref, kpe_ref, g_ref, cos_ref, sin_ref, k_ref, v_ref):
    kpe = kpe_ref[:, :QK_ROPE]
    for h in range(MLA_HEADS):
        base = h * (QK_NOPE + MLA_V)
        kh = _rms(jnp.concatenate([kv_ref[:, base:base + QK_NOPE], kpe], axis=-1), g_ref[...])
        roped = _rope(kh[:, QK_NOPE:], cos_ref[...], sin_ref[...])
        k_ref[0, h] = jnp.concatenate([kh[:, :QK_NOPE], roped], axis=-1).astype(BF16)
        v_ref[0, h] = kv_ref[:, base + QK_NOPE:base + QK_NOPE + MLA_V].astype(BF16)


def _mla_prep_q(q, gain, cos2, sin2, B, T, *, tm=256):
    nt = T // tm
    return pl.pallas_call(
        _mla_q_kernel,
        grid=(B * nt,),
        in_specs=[pl.BlockSpec((tm, MLA_HEADS * MLA_QK), lambda i: (i, 0)),
                  pl.BlockSpec((1, MLA_QK), lambda i: (0, 0)),
                  pl.BlockSpec((tm, QK_ROPE), lambda i: (i % nt, 0)),
                  pl.BlockSpec((tm, QK_ROPE), lambda i: (i % nt, 0))],
        out_specs=pl.BlockSpec((1, MLA_HEADS, tm, MLA_QK), lambda i: (i // nt, 0, i % nt, 0)),
        out_shape=jax.ShapeDtypeStruct((B, MLA_HEADS, T, MLA_QK), BF16),
        compiler_params=_cparams("parallel"),
        name="mla_prep_q",
    )(q, gain.reshape(1, MLA_QK), cos2, sin2)


def _mla_prep_kv(kv, kv_a, gain, cos2, sin2, B, T, *, tm=256):
    nt = T // tm
    pe_block = KV_LORA // LANES
    return pl.pallas_call(
        _mla_kv_kernel,
        grid=(B * nt,),
        in_specs=[pl.BlockSpec((tm, MLA_HEADS * (QK_NOPE + MLA_V)), lambda i: (i, 0)),
                  pl.BlockSpec((tm, LANES), lambda i: (i, pe_block)),
                  pl.BlockSpec((1, MLA_QK), lambda i: (0, 0)),
                  pl.BlockSpec((tm, QK_ROPE), lambda i: (i % nt, 0)),
                  pl.BlockSpec((tm, QK_ROPE), lambda i: (i % nt, 0))],
        out_specs=[pl.BlockSpec((1, MLA_HEADS, tm, MLA_QK), lambda i: (i // nt, 0, i % nt, 0)),
                   pl.BlockSpec((1, MLA_HEADS, tm, MLA_V), lambda i: (i // nt, 0, i % nt, 0))],
        out_shape=[jax.ShapeDtypeStruct((B, MLA_HEADS, T, MLA_QK), BF16),
                   jax.ShapeDtypeStruct((B, MLA_HEADS, T, MLA_V), BF16)],
        compiler_params=_cparams("parallel"),
        name="mla_prep_kv",
    )(kv, kv_a, gain.reshape(1, MLA_QK), cos2, sin2)


def _mla_attn_kernel(q_ref, k_ref, v_ref, o_ref, m_sc, l_sc, acc_sc):
    TQ = MLA_TILE
    qi = pl.program_id(2)
    q = q_ref[0, 0]
    m_sc[...] = jnp.full_like(m_sc, NEG_INF)
    l_sc[...] = jnp.zeros_like(l_sc)
    acc_sc[...] = jnp.zeros_like(acc_sc)

    def scores(i):
        k0 = pl.multiple_of(i * TQ, TQ)
        return _dot_nt(q, k_ref[0, 0, pl.ds(k0, TQ), :])

    def update(i, s):
        k0 = pl.multiple_of(i * TQ, TQ)
        m_old = m_sc[...]
        m_new = jnp.maximum(m_old, jnp.max(s, axis=-1, keepdims=True))
        p = jnp.exp2(s - jnp.concatenate([m_new] * (TQ // LANES), axis=-1))
        alpha = jnp.exp2(m_old - m_new)
        l_sc[...] = alpha * l_sc[...] + jnp.sum(p, axis=-1, keepdims=True)
        acc_sc[...] = alpha * acc_sc[...] + _dot(p.astype(BF16), v_ref[0, 0, pl.ds(k0, TQ), :])
        m_sc[...] = m_new

    def body(i, s):
        s_next = scores(i + 1)
        update(i, s)
        return s_next

    s = lax.fori_loop(0, qi, body, scores(0))
    qq = lax.broadcasted_iota(jnp.int32, (TQ, TQ), 0)
    kk = lax.broadcasted_iota(jnp.int32, (TQ, TQ), 1)
    update(qi, jnp.where(kk <= qq, s, NEG_INF))
    o_ref[...] = acc_sc[...] / l_sc[...]


def _mla_attn(q, k, v, B, T):
    H, TQ = MLA_HEADS, MLA_TILE
    nq = T // TQ
    return pl.pallas_call(
        _mla_attn_kernel,
        grid=(B, H, nq),
        in_specs=[pl.BlockSpec((1, 1, TQ, MLA_QK), lambda b, h, i: (b, h, i, 0)),
                  pl.BlockSpec((1, 1, T, MLA_QK), lambda b, h, i: (b, h, 0, 0)),
                  pl.BlockSpec((1, 1, T, MLA_V), lambda b, h, i: (b, h, 0, 0))],
        out_specs=pl.BlockSpec((TQ, MLA_V), lambda b, h, i: (b * nq + i, h)),
        out_shape=jax.ShapeDtypeStruct((B * T, H * MLA_V), F32),
        scratch_shapes=[pltpu.VMEM((TQ, LANES), F32), pltpu.VMEM((TQ, LANES), F32),
                        pltpu.VMEM((TQ, MLA_V), F32)],
        compiler_params=_cparams("parallel", "parallel", "arbitrary"),
        name="mla_flash_attn",
    )(q, k, v)


def _pad_cols(w, mult=LANES):
    pad = -w.shape[1] % mult
    return jnp.pad(w, ((0, 0), (0, pad))) if pad else w


def kernel(x, a_attn_norm, a_w_in, a_q_norm, a_kcmp_norm, a_kslc_norm, a_kwin_norm, a_cmp_pos_k, a_cmp_pos_v, a_cmp_k_w1, a_cmp_k_b1, a_cmp_k_w2, a_cmp_v_w1, a_cmp_v_b1, a_cmp_v_w2, a_w_out, kv_norm, kv_w_a, kv_c_norm, kv_w_b, kv_k_norm, b_attn_norm, b_w_q_a, b_q_a_norm, b_w_q_b, b_q_norm, b_w_out, ffn_norm, ffn_w_gate_up, ffn_w_down):
    B, T, D = x.shape
    n_a = a_w_in.shape[0]
    n_b = b_w_q_a.shape[0]
    xs = x.reshape(B * T, D)

    inv = ROPE_THETA ** (-jnp.arange(0, QK_ROPE, 2, dtype=F32) / QK_ROPE)
    ang = jnp.arange(T, dtype=F32)[:, None] * inv[None, :]
    cos, sin = jnp.cos(ang), jnp.sin(ang)
    cos2 = jnp.concatenate([cos, cos], axis=-1)
    sin2 = jnp.concatenate([-sin, sin], axis=-1)

    k_shared = v_shared = None
    for layer in range(n_a + n_b):
        if layer < n_a:
            i = layer
            proj = _linear(xs, _pad_cols(a_w_in[i]).astype(BF16), gain=a_attn_norm[i], name="nsa_in_proj")
            o = _nsa_mixer(proj, B, T, a_q_norm[i], a_kcmp_norm[i], a_kslc_norm[i], a_kwin_norm[i],
                           a_cmp_pos_k[i], a_cmp_pos_v[i], a_cmp_k_w1[i], a_cmp_k_b1[i], a_cmp_k_w2[i],
                           a_cmp_v_w1[i], a_cmp_v_b1[i], a_cmp_v_w2[i])
            xs = _linear(o, a_w_out[i].astype(BF16), residual=xs, name="nsa_out_proj")
        else:
            j = layer - n_a
            qa = _linear(xs, b_w_q_a[j].astype(BF16), gain=b_attn_norm[j], name="mla_q_a_proj")
            qb = _linear(qa, b_w_q_b[j].astype(BF16), gain=b_q_a_norm[j], name="mla_q_b_proj")
            q = _mla_prep_q(qb, b_q_norm[j], cos2, sin2, B, T)
            o = _mla_attn(q, k_shared, v_shared, B, T)
            xs = _linear(o, b_w_out[j].astype(BF16), residual=xs, name="mla_out_proj")
        xs = _ffn(xs, ffn_norm[layer], ffn_w_gate_up[layer].astype(BF16), ffn_w_down[layer].astype(BF16))
        if layer == n_a - 1:
            kv_a = _linear(xs, _pad_cols(kv_w_a).astype(BF16), gain=kv_norm, name="mla_kv_a_proj")
            kv = _linear(kv_a, kv_w_b.astype(BF16), gain=kv_c_norm, k_cols=KV_LORA, name="mla_kv_b_proj")
            k_shared, v_shared = _mla_prep_kv(kv, kv_a, kv_k_norm, cos2, sin2, B, T)
    return xs.reshape(B, T, D)
```

```python
import functools

import jax
import jax.numpy as jnp
from jax import lax
from jax.experimental import pallas as pl
from jax.experimental.pallas import tpu as pltpu

F32 = jnp.float32
BF16 = jnp.bfloat16

NORM_EPS = 1e-6
NEG_INF = -1e30
LANES = 128

NSA_HEADS = 16
NSA_GROUPS = 4
NSA_HPG = NSA_HEADS // NSA_GROUPS
NSA_DH = 64
CMP_BLOCK = 32
CMP_STRIDE = 16
CMP_HIDDEN = 256
SLC_BLOCK = 64
SLC_TOP = 16
WINDOW = 512
FORCE_SCORE = 1e4
NSA_Q_TILE = 256
NSA_KEY_TILE = 256
LOG2E = 1.4426950408889634

MLA_HEADS = 8
QK_NOPE = 128
QK_ROPE = 64
MLA_QK = QK_NOPE + QK_ROPE
MLA_V = 128
Q_LORA = 384
KV_LORA = 256
ROPE_THETA = 10000.0
MLA_TILE = 512

VMEM_LIMIT = 56 * 1024 * 1024


def _cparams(*sem):
    return pltpu.CompilerParams(dimension_semantics=sem, vmem_limit_bytes=VMEM_LIMIT)


def _rms(x, g):
    return x * lax.rsqrt(jnp.mean(x * x, axis=-1, keepdims=True) + NORM_EPS) * g


def _dot(a, b):
    return jnp.dot(a, b, preferred_element_type=F32)


def _dot_nt(a, b):
    return lax.dot_general(a, b, (((1,), (1,)), ((), ())), preferred_element_type=F32)


def _linear_kernel(*refs, has_gain, has_res):
    it = iter(refs)
    x_ref = next(it)
    w_ref = next(it)
    g_ref = next(it) if has_gain else None
    r_ref = next(it) if has_res else None
    o_ref = next(it)
    x = x_ref[...]
    if has_gain:
        x = _rms(x.astype(F32), g_ref[...])
    acc = _dot(x.astype(BF16), w_ref[...])
    if has_res:
        acc = acc + r_ref[...]
    o_ref[...] = acc.astype(o_ref.dtype)


def _linear(x, w, gain=None, residual=None, *, name, k_cols=None, tm=512, tn=None, out_dtype=F32):
    n = x.shape[0]
    k, m = w.shape
    if k_cols is None:
        assert x.shape[1] == k
    tn = m if tn is None else tn
    assert n % tm == 0 and m % tn == 0
    in_specs = [pl.BlockSpec((tm, k), lambda i, j: (i, 0)),
                pl.BlockSpec((k, tn), lambda i, j: (0, j))]
    args = [x, w]
    if gain is not None:
        in_specs.append(pl.BlockSpec((1, k), lambda i, j: (0, 0)))
        args.append(gain.reshape(1, k).astype(F32))
    if residual is not None:
        in_specs.append(pl.BlockSpec((tm, tn), lambda i, j: (i, j)))
        args.append(residual)
    return pl.pallas_call(
        functools.partial(_linear_kernel, has_gain=gain is not None, has_res=residual is not None),
        grid=(n // tm, m // tn),
        in_specs=in_specs,
        out_specs=pl.BlockSpec((tm, tn), lambda i, j: (i, j)),
        out_shape=jax.ShapeDtypeStruct((n, m), out_dtype),
        compiler_params=_cparams("parallel", "arbitrary"),
        name=name,
    )(*args)


def _ffn_kernel(x_ref, g_ref, wg_ref, wu_ref, wd_ref, o_ref, hn_sc, acc_sc):
    j = pl.program_id(1)

    @pl.when(j == 0)
    def _():
        hn_sc[...] = _rms(x_ref[...], g_ref[...]).astype(BF16)
        acc_sc[...] = jnp.zeros_like(acc_sc)

    h = hn_sc[...]
    gate = _dot(h, wg_ref[...])
    up = _dot(h, wu_ref[...])
    a = gate * jax.nn.sigmoid(gate) * up
    acc_sc[...] += _dot(a.astype(BF16), wd_ref[...])

    @pl.when(j == pl.num_programs(1) - 1)
    def _():
        o_ref[...] = x_ref[...] + acc_sc[...]


def _ffn(x, gain, w_gate_up, w_down, *, tm=1024, tc=256):
    n, d = x.shape
    hid = w_down.shape[0]
    assert n % tm == 0 and hid % tc == 0
    nc = hid // tc
    return pl.pallas_call(
        _ffn_kernel,
        grid=(n // tm, nc),
        in_specs=[pl.BlockSpec((tm, d), lambda i, j: (i, 0)),
                  pl.BlockSpec((1, d), lambda i, j: (0, 0)),
                  pl.BlockSpec((d, tc), lambda i, j: (0, j)),
                  pl.BlockSpec((d, tc), lambda i, j: (0, nc + j)),
                  pl.BlockSpec((tc, d), lambda i, j: (j, 0))],
        out_specs=pl.BlockSpec((tm, d), lambda i, j: (i, 0)),
        out_shape=jax.ShapeDtypeStruct((n, d), F32),
        scratch_shapes=[pltpu.VMEM((tm, d), BF16), pltpu.VMEM((tm, d), F32)],
        compiler_params=_cparams("parallel", "arbitrary"),
        name="swiglu_ffn",
    )(x, gain.reshape(1, d), w_gate_up, w_gate_up, w_down)


def _nsa_prep_kernel(p_ref, qn_ref, ksn_ref, kwn_ref,
                     q_ref, kc_ref, vc_ref, ksl_ref, vsl_ref, kw_ref, vw_ref, gt_ref):
    G, HG, DH = NSA_GROUPS, NSA_HPG, NSA_DH
    qw, kvw = NSA_HEADS * DH, G * DH
    scale = DH ** -0.5 * LOG2E
    tm = p_ref.shape[0]
    ones_col = (lax.broadcasted_iota(jnp.int32, (tm, LANES - DH), 1) == 0).astype(F32)

    def piece(base, g):
        return p_ref[:, base + g * DH: base + (g + 1) * DH]

    for g in range(G):
        for h in range(HG):
            qh = p_ref[:, (g * HG + h) * DH:(g * HG + h + 1) * DH]
            q_ref[0, g, h] = (_rms(qh, qn_ref[...]) * scale).astype(BF16)
        kc_ref[0, g] = piece(qw, g)
        vc_ref[0, g] = piece(qw + kvw, g)
        ksl_ref[0, g] = _rms(piece(qw + 2 * kvw, g), ksn_ref[...]).astype(BF16)
        vsl_ref[0, g] = jnp.concatenate([piece(qw + 3 * kvw, g), ones_col], axis=-1).astype(BF16)
        kw_ref[0, g] = _rms(piece(qw + 4 * kvw, g), kwn_ref[...]).astype(BF16)
        vw_ref[0, g] = jnp.concatenate([piece(qw + 5 * kvw, g), ones_col], axis=-1).astype(BF16)
        gb = qw + 6 * kvw + g * HG * 3
        gt_ref[0, g] = jax.nn.sigmoid(p_ref[:, gb:gb + HG * 3])


def _nsa_prep(p, q_norm, kslc_norm, kwin_norm, B, T, *, tm=256):
    G, HG, DH = NSA_GROUPS, NSA_HPG, NSA_DH
    nt = T // tm
    width = p.shape[1]
    gain = lambda: pl.BlockSpec((1, DH), lambda i: (0, 0))
    per_group = lambda last: pl.BlockSpec((1, G, tm, last), lambda i: (i // nt, 0, i % nt, 0))
    shp = lambda last, dt: jax.ShapeDtypeStruct((B, G, T, last), dt)
    return pl.pallas_call(
        _nsa_prep_kernel,
        grid=(B * nt,),
        in_specs=[pl.BlockSpec((tm, width), lambda i: (i, 0)), gain(), gain(), gain()],
        out_specs=[pl.BlockSpec((1, G, HG, tm, DH), lambda i: (i // nt, 0, 0, i % nt, 0)),
                   per_group(DH), per_group(DH), per_group(DH), per_group(LANES),
                   per_group(DH), per_group(LANES), per_group(HG * 3)],
        out_shape=[jax.ShapeDtypeStruct((B, G, HG, T, DH), BF16),
                   shp(DH, F32), shp(DH, F32), shp(DH, BF16), shp(LANES, BF16),
                   shp(DH, BF16), shp(LANES, BF16), shp(HG * 3, F32)],
        compiler_params=_cparams("parallel"),
        name="nsa_prep",
    )(p, q_norm.reshape(1, DH), kslc_norm.reshape(1, DH), kwin_norm.reshape(1, DH))


def _compress_kernel(rk_ref, rv_ref, pek_ref, pev_ref, kw1_ref, kb1_ref, kw2_ref,
                     vw1_ref, vb1_ref, vw2_ref, kn_ref, kcmp_ref, vcmp_ref):
    half = CMP_STRIDE * NSA_DH
    nrow = rk_ref.shape[2]

    def mlp(r_ref, pe_ref, w1_ref, b1_ref, w2_ref):
        r = r_ref[0, 0].astype(BF16)
        ya = _dot(r, w1_ref[:half, :])
        yb = _dot(r, w1_ref[half:, :])
        pe = jnp.broadcast_to(pe_ref[...], (8, 2 * half)).astype(BF16)
        c = _dot(pe, w1_ref[...])[0:1] + b1_ref[...]
        hid = ya + pltpu.roll(yb, nrow - 1, 0) + c
        return _dot(jax.nn.gelu(hid).astype(BF16), w2_ref[...])

    kcmp_ref[0, 0] = _rms(mlp(rk_ref, pek_ref, kw1_ref, kb1_ref, kw2_ref), kn_ref[...]).astype(BF16)
    vcmp_ref[0, 0] = mlp(rv_ref, pev_ref, vw1_ref, vb1_ref, vw2_ref).astype(BF16)


def _compress(rk, rv, pe_k, pe_v, k_w1, k_b1, k_w2, v_w1, v_b1, v_w2, kcmp_norm):
    B, G, nrow, half = rk.shape
    DH, HID = NSA_DH, CMP_HIDDEN
    full = lambda shape: pl.BlockSpec(shape, lambda b, g: (0,) * len(shape))
    r_spec = pl.BlockSpec((1, 1, nrow, half), lambda b, g: (b, g, 0, 0))
    o_spec = pl.BlockSpec((1, 1, nrow, DH), lambda b, g: (b, g, 0, 0))
    return pl.pallas_call(
        _compress_kernel,
        grid=(B, G),
        in_specs=[r_spec, r_spec, full((1, 2 * half)), full((1, 2 * half)),
                  full((2 * half, HID)), full((1, HID)), full((HID, DH)),
                  full((2 * half, HID)), full((1, HID)), full((HID, DH)), full((1, DH))],
        out_specs=[o_spec, o_spec],
        out_shape=[jax.ShapeDtypeStruct((B, G, nrow, DH), BF16)] * 2,
        compiler_params=_cparams("parallel", "parallel"),
        name="nsa_compress",
    )(rk, rv, pe_k.reshape(1, -1), pe_v.reshape(1, -1),
      k_w1.astype(BF16), k_b1.reshape(1, HID), k_w2.astype(BF16),
      v_w1.astype(BF16), v_b1.reshape(1, HID), v_w2.astype(BF16), kcmp_norm.reshape(1, DH))


def _split3(x):
    hi = x.astype(BF16)
    r1 = x - hi.astype(F32)
    mid = r1.astype(BF16)
    lo = (r1 - mid.astype(F32)).astype(BF16)
    return hi, mid, lo


def _nsa_cmp_kernel(q_ref, kc_ref, vc_ref, gt_ref, sl_ref, ov_ref, gm_ref,
                    oc_ref, sel_ref, act_ref, *, n_top):
    HG, DH, TQ = NSA_HPG, NSA_DH, NSA_Q_TILE
    s0 = pl.program_id(2) * TQ
    ncmp = kc_ref.shape[2]
    n_slc = ov_ref.shape[1]
    qs = q_ref[0, 0].reshape(HG * TQ, DH)
    gt = gt_ref[0, 0]

    def attend(nc):
        s = _dot_nt(qs, kc_ref[0, 0, :nc, :])
        j = lax.broadcasted_iota(jnp.int32, (1, nc), 1)
        t = s0 + lax.broadcasted_iota(jnp.int32, (TQ, 1), 0)
        mask = j * CMP_STRIDE + (CMP_BLOCK - 1) <= t
        mid = (j * CMP_STRIDE - s0).astype(F32) + 0.5 * (CMP_BLOCK - 1)
        has_key = t >= CMP_BLOCK - 1
        psum = jnp.zeros((TQ, nc), F32)
        outs = []
        for h in range(HG):
            rows = slice(h * TQ, (h + 1) * TQ)
            slope_row = jnp.concatenate([sl_ref[0, h * TQ:h * TQ + 1, :]] * (nc // LANES), axis=-1)
            sh = jnp.where(mask, s[rows] + slope_row * mid, NEG_INF)
            m = jnp.max(sh, axis=-1, keepdims=True)
            e = jnp.exp2(sh - m)
            l = jnp.sum(e, axis=-1, keepdims=True)
            inv = jnp.where(has_key, 1.0 / l, 0.0)
            p = e * inv
            psum = psum + p
            outs.append(_dot(p.astype(BF16), vc_ref[0, 0, :nc, :]) * gt[:, 3 * h:3 * h + 1])
        oc_ref[...] = jnp.concatenate(outs, axis=-1)
        hi, mid, lo = _split3(psum)
        ov = ov_ref[:nc, :]
        imp = _dot(hi, ov) + _dot(mid, ov) + _dot(lo, ov)
        select(imp, nc * CMP_STRIDE // SLC_BLOCK)

    def select(imp, nb):
        imp_t = imp.T[:nb]
        blk = lax.broadcasted_iota(jnp.int32, (nb, TQ), 0)
        tq = s0 + lax.broadcasted_iota(jnp.int32, (nb, TQ), 1)
        cur = lax.shift_right_logical(tq, SLC_BLOCK.bit_length() - 1)
        forced = (blk == 0) | (blk == cur) | (blk == cur - 1)
        score = jnp.where(blk * SLC_BLOCK <= tq, imp_t + jnp.where(forced, FORCE_SCORE, 0.0), -1.0)

        def pick(_, carry):
            score, sel = carry
            mx = jnp.max(score, axis=0, keepdims=True)
            first = jnp.min(jnp.where(score == mx, blk, nb), axis=0, keepdims=True)
            hit = blk == first
            return jnp.where(hit, -2.0, score), jnp.where(hit, 1.0, sel)

        _, sel_t = lax.fori_loop(0, n_top, pick, (score, jnp.zeros((nb, TQ), F32)), unroll=True)
        if nb < n_slc:
            sel_t = jnp.concatenate([sel_t, jnp.zeros((n_slc - nb, TQ), F32)], axis=0)
        sel = sel_t.T.astype(BF16)
        sel_ref[0, 0] = sel
        cnt = _dot(jnp.ones((8, TQ), BF16), sel)
        act_ref[0, 0, 0] = _dot((cnt > 0.0).astype(BF16), gm_ref[...])

    need = (s0 + TQ) // CMP_STRIDE
    widths = list(range(LANES, ncmp + 1, LANES))
    for idx, nc in enumerate(widths):
        lo_w = widths[idx - 1] if idx else 0
        pl.when((need > lo_w) & (need <= nc))(functools.partial(attend, nc))


def _nsa_cmp(q, kcmp, vcmp, gates, slope_tab, overlap, group_mat, B, T, n_top):
    G, HG, DH, TQ = NSA_GROUPS, NSA_HPG, NSA_DH, NSA_Q_TILE
    nq = T // TQ
    ncmp = kcmp.shape[2]
    n_slc = overlap.shape[1]
    return pl.pallas_call(
        functools.partial(_nsa_cmp_kernel, n_top=n_top),
        grid=(B, G, nq),
        in_specs=[pl.BlockSpec((1, 1, HG, TQ, DH), lambda b, g, i: (b, g, 0, i, 0)),
                  pl.BlockSpec((1, 1, ncmp, DH), lambda b, g, i: (b, g, 0, 0)),
                  pl.BlockSpec((1, 1, ncmp, DH), lambda b, g, i: (b, g, 0, 0)),
                  pl.BlockSpec((1, 1, TQ, HG * 3), lambda b, g, i: (b, g, i, 0)),
                  pl.BlockSpec((1, HG * TQ, LANES), lambda b, g, i: (g, 0, 0)),
                  pl.BlockSpec((ncmp, n_slc), lambda b, g, i: (0, 0)),
                  pl.BlockSpec((n_slc, LANES), lambda b, g, i: (0, 0))],
        out_specs=[pl.BlockSpec((TQ, HG * DH), lambda b, g, i: (b * nq + i, g)),
                   pl.BlockSpec((1, 1, TQ, n_slc), lambda b, g, i: (b, g, i, 0)),
                   pl.BlockSpec((1, 1, 1, 8, LANES), lambda b, g, i: (b, g, i, 0, 0))],
        out_shape=[jax.ShapeDtypeStruct((B * T, NSA_HEADS * DH), F32),
                   jax.ShapeDtypeStruct((B, G, T, n_slc), BF16),
                   jax.ShapeDtypeStruct((B, G, nq, 8, LANES), F32)],
        compiler_params=_cparams("parallel", "parallel", "arbitrary"),
        name="nsa_cmp_select",
    )(q, kcmp, vcmp, gates, slope_tab, overlap, group_mat)


def _nsa_main_kernel(act_ref, q_ref, sel_ref, oc_ref, gt_ref, sl_ref, ex_ref,
                     ksl_ref, vsl_ref, kw_ref, vw_ref, o_ref,
                     m_sc, acc_sc, s0_sc, s1_sc, pk0_sc, pk1_sc, list_sc, *, n_ktiles):
    HG, DH, TQ, KT = NSA_HPG, NSA_DH, NSA_Q_TILE, NSA_KEY_TILE
    s_slots, pk_slots = (s0_sc, s1_sc), (pk0_sc, pk1_sc)
    WK = WINDOW + TQ
    b, g, qi = pl.program_id(0), pl.program_id(1), pl.program_id(2)
    nq = pl.num_programs(2)
    s0 = qi * TQ
    qs = q_ref[0, 0].reshape(HG * TQ, DH)
    selb = sel_ref[0, 0]
    gt = gt_ref[0, 0]
    t_col = s0 + lax.broadcasted_iota(jnp.int32, (TQ, 1), 0)

    def biased(s, kpos, mask, h):
        width = kpos.shape[1]
        slope_row = jnp.concatenate([sl_ref[0, h * TQ:h * TQ + 1, :]] * (width // LANES), axis=-1)
        sh = s[h * TQ:(h + 1) * TQ] + slope_row * (kpos - s0).astype(F32)
        return jnp.where(mask, sh, NEG_INF)

    m_sc[...] = jnp.full_like(m_sc, NEG_INF)
    acc_sc[...] = jnp.zeros_like(acc_sc)
    act_base = ((b * NSA_GROUPS + g) * nq + qi) * n_ktiles
    list_sc[0] = 0

    def compact(i, n):
        hit = act_ref[act_base + i] > 0

        @pl.when(hit)
        def _():
            list_sc[n] = i
        return n + hit.astype(jnp.int32)

    n_act = jnp.maximum(lax.fori_loop(0, qi + 1, compact, 0), 1)

    def scores(j, slot):
        k0 = pl.multiple_of(list_sc[j] * KT, KT)
        s_slots[slot][...] = _dot_nt(qs, ksl_ref[0, 0, pl.ds(k0, KT), :])
        pk_slots[slot][...] = _dot(selb, ex_ref[:, pl.ds(k0, KT)])

    def update(j, slot):
        k0 = pl.multiple_of(list_sc[j] * KT, KT)
        kpos = k0 + lax.broadcasted_iota(jnp.int32, (1, KT), 1)
        mask = (pk_slots[slot][...] > 0.5) & (kpos <= t_col)
        ps, alphas = [], []
        for h in range(HG):
            rows = slice(h * TQ, (h + 1) * TQ)
            sh = biased(s_slots[slot], kpos, mask, h)
            m_old = m_sc[rows]
            m_new = jnp.maximum(m_old, jnp.max(sh, axis=-1, keepdims=True))
            ps.append(jnp.exp2(sh - jnp.concatenate([m_new] * (KT // LANES), axis=-1)).astype(BF16))
            alphas.append(jnp.exp2(m_old - m_new))
            m_sc[rows] = m_new
        pv = _dot(jnp.concatenate(ps, axis=0), vsl_ref[0, 0, pl.ds(k0, KT), :])
        acc_sc[...] = jnp.concatenate(alphas, axis=0) * acc_sc[...] + pv

    def stage(j, slot):
        scores(j + 1, 1 - slot)
        update(j, slot)

    scores(0, 0)
    _two_slot_pipeline(n_act - 1, stage, update)

    start = pl.multiple_of(jnp.maximum(s0 - WINDOW, 0), TQ)
    kpos_w = start + lax.broadcasted_iota(jnp.int32, (1, WK), 1)
    mask_w = (kpos_w <= t_col) & (kpos_w > t_col - WINDOW)
    s_w = _dot_nt(qs, kw_ref[0, 0, pl.ds(start, WK), :])
    pw = []
    for h in range(HG):
        sh = biased(s_w, kpos_w, mask_w, h)
        pw.append(jnp.exp2(sh - jnp.max(sh, axis=-1, keepdims=True)).astype(BF16))
    acc_w = _dot(jnp.concatenate(pw, axis=0), vw_ref[0, 0, pl.ds(start, WK), :])

    outs = []
    for h in range(HG):
        rows = slice(h * TQ, (h + 1) * TQ)
        a_s, a_w = acc_sc[rows], acc_w[rows]
        outs.append(oc_ref[:, h * DH:(h + 1) * DH]
                    + gt[:, 3 * h + 1:3 * h + 2] * (a_s[:, :DH] / a_s[:, DH:DH + 1])
                    + gt[:, 3 * h + 2:3 * h + 3] * (a_w[:, :DH] / a_w[:, DH:DH + 1]))
    o_ref[...] = jnp.concatenate(outs, axis=-1)


def _nsa_main(act, q, sel, oc, gates, slope_tab, expand, ksl, vsl, kw, vw, B, T):
    G, HG, DH, TQ = NSA_GROUPS, NSA_HPG, NSA_DH, NSA_Q_TILE
    nq = T // TQ
    n_slc = sel.shape[3]
    n_ktiles = T // NSA_KEY_TILE
    kv = lambda last: pl.BlockSpec((1, 1, T, last), lambda b, g, i, a: (b, g, 0, 0))
    grid_spec = pltpu.PrefetchScalarGridSpec(
        num_scalar_prefetch=1,
        grid=(B, G, nq),
        in_specs=[pl.BlockSpec((1, 1, HG, TQ, DH), lambda b, g, i, a: (b, g, 0, i, 0)),
                  pl.BlockSpec((1, 1, TQ, n_slc), lambda b, g, i, a: (b, g, i, 0)),
                  pl.BlockSpec((TQ, HG * DH), lambda b, g, i, a: (b * nq + i, g)),
                  pl.BlockSpec((1, 1, TQ, HG * 3), lambda b, g, i, a: (b, g, i, 0)),
                  pl.BlockSpec((1, HG * TQ, LANES), lambda b, g, i, a: (g, 0, 0)),
                  pl.BlockSpec((n_slc, T), lambda b, g, i, a: (0, 0)),
                  kv(DH), kv(LANES), kv(DH), kv(LANES)],
        out_specs=pl.BlockSpec((TQ, HG * DH), lambda b, g, i, a: (b * nq + i, g)),
        scratch_shapes=[pltpu.VMEM((HG * TQ, LANES), F32), pltpu.VMEM((HG * TQ, LANES), F32),
                        pltpu.VMEM((HG * TQ, NSA_KEY_TILE), F32), pltpu.VMEM((HG * TQ, NSA_KEY_TILE), F32),
                        pltpu.VMEM((TQ, NSA_KEY_TILE), F32), pltpu.VMEM((TQ, NSA_KEY_TILE), F32),
                        pltpu.SMEM((n_ktiles + 1,), jnp.int32)],
    )
    return pl.pallas_call(
        functools.partial(_nsa_main_kernel, n_ktiles=n_ktiles),
        grid_spec=grid_spec,
        out_shape=jax.ShapeDtypeStruct((B * T, NSA_HEADS * DH), F32),
        compiler_params=_cparams("parallel", "parallel", "arbitrary"),
        name="nsa_select_window",
    )(act, q, sel, oc, gates, slope_tab, expand, ksl, vsl, kw, vw)


def _nsa_mixer(h_proj, B, T, q_norm, kcmp_norm, kslc_norm, kwin_norm, pos_k, pos_v,
               k_w1, k_b1, k_w2, v_w1, v_b1, v_w2):
    G, HG, DH, TQ = NSA_GROUPS, NSA_HPG, NSA_DH, NSA_Q_TILE
    q, kc, vc, ksl, vsl, kw, vw, gates = _nsa_prep(h_proj, q_norm, kslc_norm, kwin_norm, B, T)

    nrow = T // CMP_STRIDE
    half = CMP_STRIDE * DH
    kcmp, vcmp = _compress(kc.reshape(B, G, nrow, half), vc.reshape(B, G, nrow, half),
                           pos_k, pos_v, k_w1, k_b1, k_w2, v_w1, v_b1, v_w2, kcmp_norm)

    n_slc = T // SLC_BLOCK
    n_top = min(SLC_TOP, n_slc)
    n_slc_pad = max(n_slc, LANES)
    slopes = 2.0 ** (-8.0 * jnp.arange(1, NSA_HEADS + 1, dtype=F32) / NSA_HEADS) * LOG2E
    slope_tab = jnp.broadcast_to(slopes.reshape(G, HG, 1, 1), (G, HG, TQ, LANES)).reshape(G, HG * TQ, LANES)
    cj = jnp.arange(nrow)[:, None] * CMP_STRIDE
    si = jnp.arange(n_slc_pad)[None, :] * SLC_BLOCK
    overlap = ((cj <= si + SLC_BLOCK - 1) & (cj + CMP_BLOCK - 1 >= si)
               & (jnp.arange(nrow)[:, None] < nrow - 1)).astype(BF16)
    blocks_per_tile = NSA_KEY_TILE // SLC_BLOCK
    group_mat = (jnp.arange(n_slc_pad)[:, None] // blocks_per_tile
                 == jnp.arange(LANES)[None, :]).astype(BF16)
    expand = (jnp.arange(n_slc_pad)[:, None] == jnp.arange(T)[None, :] // SLC_BLOCK).astype(BF16)

    oc, sel, act = _nsa_cmp(q, kcmp, vcmp, gates, slope_tab, overlap, group_mat, B, T, n_top)
    n_ktiles = T // NSA_KEY_TILE
    act_i = act[:, :, :, 0, :n_ktiles].astype(jnp.int32).reshape(-1)
    return _nsa_main(act_i, q, sel, oc, gates, slope_tab, expand, ksl, vsl, kw, vw, B, T)


def _rope(r, cos2, sin2):
    half = QK_ROPE // 2
    rot = jnp.concatenate([r[:, half:], r[:, :half]], axis=-1)
    return r * cos2 + rot * sin2


def _mla_q_kernel(q_ref, g_ref, cos_ref, sin_ref, o_ref):
    scale = MLA_QK ** -0.5 * LOG2E
    for h in range(MLA_HEADS):
        qh = _rms(q_ref[:, h * MLA_QK:(h + 1) * MLA_QK], g_ref[...]) * scale
        roped = _rope(qh[:, QK_NOPE:], cos_ref[...], sin_ref[...])
        o_ref[0, h] = jnp.concatenate([qh[:, :QK_NOPE], roped], axis=-1).astype(BF16)


def _mla_kv_kernel(kv_ref, kpe_ref, g_ref, cos_ref, sin_ref, k_ref, v_ref):
    kpe = kpe_ref[:, :QK_ROPE]
    for h in range(MLA_HEADS):
        base = h * (QK_NOPE + MLA_V)
        kh = _rms(jnp.concatenate([kv_ref[:, base:base + QK_NOPE], kpe], axis=-1), g_ref[...])
        roped = _rope(kh[:, QK_NOPE:], cos_ref[...], sin_ref[...])
        k_ref[0, h] = jnp.concatenate([kh[:, :QK_NOPE], roped], axis=-1).astype(BF16)
        v_ref[0, h] = kv_ref[:, base + QK_NOPE:base + QK_NOPE + MLA_V].astype(BF16)


def _mla_prep_q(q, gain, cos2, sin2, B, T, *, tm=256):
    nt = T // tm
    return pl.pallas_call(
        _mla_q_kernel,
        grid=(B * nt,),
        in_specs=[pl.BlockSpec((tm, MLA_HEADS * MLA_QK), lambda i: (i, 0)),
                  pl.BlockSpec((1, MLA_QK), lambda i: (0, 0)),
                  pl.BlockSpec((tm, QK_ROPE), lambda i: (i % nt, 0)),
                  pl.BlockSpec((tm, QK_ROPE), lambda i: (i % nt, 0))],
        out_specs=pl.BlockSpec((1, MLA_HEADS, tm, MLA_QK), lambda i: (i // nt, 0, i % nt, 0)),
        out_shape=jax.ShapeDtypeStruct((B, MLA_HEADS, T, MLA_QK), BF16),
        compiler_params=_cparams("parallel"),
        name="mla_prep_q",
    )(q, gain.reshape(1, MLA_QK), cos2, sin2)


def _mla_prep_kv(kv, kv_a, gain, cos2, sin2, B, T, *, tm=256):
    nt = T // tm
    pe_block = KV_LORA // LANES
    return pl.pallas_call(
        _mla_kv_kernel,
        grid=(B * nt,),
        in_specs=[pl.BlockSpec((tm, MLA_HEADS * (QK_NOPE + MLA_V)), lambda i: (i, 0)),
                  pl.BlockSpec((tm, LANES), lambda i: (i, pe_block)),
                  pl.BlockSpec((1, MLA_QK), lambda i: (0, 0)),
                  pl.BlockSpec((tm, QK_ROPE), lambda i: (i % nt, 0)),
                  pl.BlockSpec((tm, QK_ROPE), lambda i: (i % nt, 0))],
        out_specs=[pl.BlockSpec((1, MLA_HEADS, tm, MLA_QK), lambda i: (i // nt, 0, i % nt, 0)),
                   pl.BlockSpec((1, MLA_HEADS, tm, MLA_V), lambda i: (i // nt, 0, i % nt, 0))],
        out_shape=[jax.ShapeDtypeStruct((B, MLA_HEADS, T, MLA_QK), BF16),
                   jax.ShapeDtypeStruct((B, MLA_HEADS, T, MLA_V), BF16)],
        compiler_params=_cparams("parallel"),
        name="mla_prep_kv",
    )(kv, kv_a, gain.reshape(1, MLA_QK), cos2, sin2)


def _two_slot_pipeline(last, stage, finish):
    def pair(j, carry):
        stage(2 * j, 0)
        stage(2 * j + 1, 1)
        return carry

    lax.fori_loop(0, last // 2, pair, 0)

    @pl.when(last % 2 == 1)
    def _():
        stage(last - 1, 0)
        finish(last, 1)

    @pl.when(last % 2 == 0)
    def _():
        finish(last, 0)


def _mla_attn_kernel(q_ref, k_ref, v_ref, o_ref, m_sc, l_sc, acc_sc, s0_sc, s1_sc):
    TQ = MLA_TILE
    qi = pl.program_id(2)
    q = q_ref[0, 0]
    s_slots = (s0_sc, s1_sc)
    m_sc[...] = jnp.full_like(m_sc, NEG_INF)
    l_sc[...] = jnp.zeros_like(l_sc)
    acc_sc[...] = jnp.zeros_like(acc_sc)

    def scores(i, slot):
        k0 = pl.multiple_of(i * TQ, TQ)
        s_slots[slot][...] = _dot_nt(q, k_ref[0, 0, pl.ds(k0, TQ), :])

    def update(i, slot, masked):
        k0 = pl.multiple_of(i * TQ, TQ)

        def load():
            s = s_slots[slot][...]
            if masked:
                qq = lax.broadcasted_iota(jnp.int32, (TQ, TQ), 0)
                kk = lax.broadcasted_iota(jnp.int32, (TQ, TQ), 1)
                s = jnp.where(kk <= qq, s, NEG_INF)
            return s

        m_old = m_sc[...]
        m_new = jnp.maximum(m_old, jnp.max(load(), axis=-1, keepdims=True))
        p = jnp.exp2(load() - jnp.concatenate([m_new] * (TQ // LANES), axis=-1))
        alpha = jnp.exp2(m_old - m_new)
        l_sc[...] = alpha * l_sc[...] + jnp.sum(p, axis=-1, keepdims=True)
        acc_sc[...] = alpha * acc_sc[...] + _dot(p.astype(BF16), v_ref[0, 0, pl.ds(k0, TQ), :])
        m_sc[...] = m_new

    def stage(i, slot):
        scores(i + 1, 1 - slot)
        update(i, slot, False)

    scores(0, 0)
    _two_slot_pipeline(qi, stage, lambda i, slot: update(i, slot, True))
    o_ref[...] = acc_sc[...] / l_sc[...]


def _mla_attn(q, k, v, B, T):
    H, TQ = MLA_HEADS, MLA_TILE
    nq = T // TQ
    return pl.pallas_call(
        _mla_attn_kernel,
        grid=(B, H, nq),
        in_specs=[pl.BlockSpec((1, 1, TQ, MLA_QK), lambda b, h, i: (b, h, i, 0)),
                  pl.BlockSpec((1, 1, T, MLA_QK), lambda b, h, i: (b, h, 0, 0)),
                  pl.BlockSpec((1, 1, T, MLA_V), lambda b, h, i: (b, h, 0, 0))],
        out_specs=pl.BlockSpec((TQ, MLA_V), lambda b, h, i: (b * nq + i, h)),
        out_shape=jax.ShapeDtypeStruct((B * T, H * MLA_V), F32),
        scratch_shapes=[pltpu.VMEM((TQ, LANES), F32), pltpu.VMEM((TQ, LANES), F32),
                        pltpu.VMEM((TQ, MLA_V), F32),
                        pltpu.VMEM((TQ, TQ), F32), pltpu.VMEM((TQ, TQ), F32)],
        compiler_params=_cparams("parallel", "parallel", "arbitrary"),
        name="mla_flash_attn",
    )(q, k, v)


def _pad_cols(w, mult=LANES):
    pad = -w.shape[1] % mult
    return jnp.pad(w, ((0, 0), (0, pad))) if pad else w


def kernel(x, a_attn_norm, a_w_in, a_q_norm, a_kcmp_norm, a_kslc_norm, a_kwin_norm, a_cmp_pos_k, a_cmp_pos_v, a_cmp_k_w1, a_cmp_k_b1, a_cmp_k_w2, a_cmp_v_w1, a_cmp_v_b1, a_cmp_v_w2, a_w_out, kv_norm, kv_w_a, kv_c_norm, kv_w_b, kv_k_norm, b_attn_norm, b_w_q_a, b_q_a_norm, b_w_q_b, b_q_norm, b_w_out, ffn_norm, ffn_w_gate_up, ffn_w_down):
    B, T, D = x.shape
    n_a = a_w_in.shape[0]
    n_b = b_w_q_a.shape[0]
    xs = x.reshape(B * T, D)

    inv = ROPE_THETA ** (-jnp.arange(0, QK_ROPE, 2, dtype=F32) / QK_ROPE)
    ang = jnp.arange(T, dtype=F32)[:, None] * inv[None, :]
    cos, sin = jnp.cos(ang), jnp.sin(ang)
    cos2 = jnp.concatenate([cos, cos], axis=-1)
    sin2 = jnp.concatenate([-sin, sin], axis=-1)

    k_shared = v_shared = None
    for layer in range(n_a + n_b):
        if layer < n_a:
            i = layer
            proj = _linear(xs, _pad_cols(a_w_in[i]).astype(BF16), gain=a_attn_norm[i], name="nsa_in_proj")
            o = _nsa_mixer(proj, B, T, a_q_norm[i], a_kcmp_norm[i], a_kslc_norm[i], a_kwin_norm[i],
                           a_cmp_pos_k[i], a_cmp_pos_v[i], a_cmp_k_w1[i], a_cmp_k_b1[i], a_cmp_k_w2[i],
                           a_cmp_v_w1[i], a_cmp_v_b1[i], a_cmp_v_w2[i])
            xs = _linear(o, a_w_out[i].astype(BF16), residual=xs, name="nsa_out_proj")
        else:
            j = layer - n_a
            qa = _linear(xs, b_w_q_a[j].astype(BF16), gain=b_attn_norm[j], name="mla_q_a_proj")
            qb = _linear(qa, b_w_q_b[j].astype(BF16), gain=b_q_a_norm[j], name="mla_q_b_proj")
            q = _mla_prep_q(qb, b_q_norm[j], cos2, sin2, B, T)
            o = _mla_attn(q, k_shared, v_shared, B, T)
            xs = _linear(o, b_w_out[j].astype(BF16), residual=xs, name="mla_out_proj")
        xs = _ffn(xs, ffn_norm[layer], ffn_w_gate_up[layer].astype(BF16), ffn_w_down[layer].astype(BF16))
        if layer == n_a - 1:
            kv_a = _linear(xs, _pad_cols(kv_w_a).astype(BF16), gain=kv_norm, name="mla_kv_a_proj")
            kv = _linear(kv_a, kv_w_b.astype(BF16), gain=kv_c_norm, k_cols=KV_LORA, name="mla_kv_b_proj")
            k_shared, v_shared = _mla_prep_kv(kv, kv_a, kv_k_norm, cos2, sin2, B, T)
    return xs.reshape(B, T, D)
```

```python
import functools

import jax
import jax.numpy as jnp
from jax import lax
from jax.experimental import pallas as pl
from jax.experimental.pallas import tpu as pltpu

F32 = jnp.float32
BF16 = jnp.bfloat16

NORM_EPS = 1e-6
NEG_INF = -1e30
LANES = 128

NSA_HEADS = 16
NSA_GROUPS = 4
NSA_HPG = NSA_HEADS // NSA_GROUPS
NSA_DH = 64
CMP_BLOCK = 32
CMP_STRIDE = 16
CMP_HIDDEN = 256
SLC_BLOCK = 64
SLC_TOP = 16
WINDOW = 512
FORCE_SCORE = 1e4
NSA_Q_TILE = 256
NSA_KEY_TILE = 256
LOG2E = 1.4426950408889634

MLA_HEADS = 8
QK_NOPE = 128
QK_ROPE = 64
MLA_QK = QK_NOPE + QK_ROPE
MLA_QK_PAD = QK_NOPE + LANES
MLA_V = 128
Q_LORA = 384
KV_LORA = 256
ROPE_THETA = 10000.0
MLA_TILE = 512

VMEM_LIMIT = 56 * 1024 * 1024


def _cparams(*sem):
    return pltpu.CompilerParams(dimension_semantics=sem, vmem_limit_bytes=VMEM_LIMIT)


def _rms(x, g):
    return x * lax.rsqrt(jnp.mean(x * x, axis=-1, keepdims=True) + NORM_EPS) * g


def _dot(a, b):
    return jnp.dot(a, b, preferred_element_type=F32)


def _dot_nt(a, b):
    return lax.dot_general(a, b, (((1,), (1,)), ((), ())), preferred_element_type=F32)


def _linear_kernel(*refs, has_gain, has_res):
    it = iter(refs)
    x_ref = next(it)
    w_ref = next(it)
    g_ref = next(it) if has_gain else None
    r_ref = next(it) if has_res else None
    o_ref = next(it)
    x = x_ref[...]
    if has_gain:
        x = _rms(x.astype(F32), g_ref[...])
    acc = _dot(x.astype(BF16), w_ref[...])
    if has_res:
        acc = acc + r_ref[...]
    o_ref[...] = acc.astype(o_ref.dtype)


def _linear(x, w, gain=None, residual=None, *, name, k_cols=None, tm=512, tn=None, out_dtype=F32):
    n = x.shape[0]
    k, m = w.shape
    if k_cols is None:
        assert x.shape[1] == k
    tn = m if tn is None else tn
    assert n % tm == 0 and m % tn == 0
    in_specs = [pl.BlockSpec((tm, k), lambda i, j: (i, 0)),
                pl.BlockSpec((k, tn), lambda i, j: (0, j))]
    args = [x, w]
    if gain is not None:
        in_specs.append(pl.BlockSpec((1, k), lambda i, j: (0, 0)))
        args.append(gain.reshape(1, k).astype(F32))
    if residual is not None:
        in_specs.append(pl.BlockSpec((tm, tn), lambda i, j: (i, j)))
        args.append(residual)
    return pl.pallas_call(
        functools.partial(_linear_kernel, has_gain=gain is not None, has_res=residual is not None),
        grid=(n // tm, m // tn),
        in_specs=in_specs,
        out_specs=pl.BlockSpec((tm, tn), lambda i, j: (i, j)),
        out_shape=jax.ShapeDtypeStruct((n, m), out_dtype),
        compiler_params=_cparams("parallel", "arbitrary"),
        name=name,
    )(*args)


def _ffn_kernel(x_ref, g_ref, wgu_ref, wd_ref, o_ref):
    hid = wd_ref.shape[0]
    x = x_ref[...]
    h = _rms(x, g_ref[...]).astype(BF16)
    gate = _dot(h, wgu_ref[:, :hid])
    up = _dot(h, wgu_ref[:, hid:])
    a = (gate * jax.nn.sigmoid(gate) * up).astype(BF16)
    o_ref[...] = x + _dot(a, wd_ref[...])


def _ffn(x, gain, w_gate_up, w_down, *, tm=512):
    n, d = x.shape
    hid = w_down.shape[0]
    assert n % tm == 0 and hid % LANES == 0
    resident = lambda shape: pl.BlockSpec(shape, lambda i: (0, 0), pipeline_mode=pl.Buffered(1))
    return pl.pallas_call(
        _ffn_kernel,
        grid=(n // tm,),
        in_specs=[pl.BlockSpec((tm, d), lambda i: (i, 0)),
                  pl.BlockSpec((1, d), lambda i: (0, 0)),
                  resident((d, 2 * hid)), resident((hid, d))],
        out_specs=pl.BlockSpec((tm, d), lambda i: (i, 0)),
        out_shape=jax.ShapeDtypeStruct((n, d), F32),
        compiler_params=_cparams("parallel"),
        name="swiglu_ffn",
    )(x, gain.reshape(1, d), w_gate_up, w_down)


def _nsa_prep_kernel(p_ref, qn_ref, ksn_ref, kwn_ref,
                     q_ref, kc_ref, vc_ref, ksl_ref, vsl_ref, kw_ref, vw_ref, gt_ref):
    G, HG, DH = NSA_GROUPS, NSA_HPG, NSA_DH
    qw, kvw = NSA_HEADS * DH, G * DH
    scale = DH ** -0.5 * LOG2E
    tm = p_ref.shape[0]
    ones_col = jnp.ones((tm, LANES - DH), F32)
    n_gate = HG * 3
    gate_pad = jnp.zeros((tm, LANES - DH - n_gate), F32)

    def piece(base, g):
        return p_ref[:, base + g * DH: base + (g + 1) * DH]

    for g in range(G):
        for h in range(HG):
            qh = p_ref[:, (g * HG + h) * DH:(g * HG + h + 1) * DH]
            q_ref[0, g, h] = (_rms(qh, qn_ref[...]) * scale).astype(BF16)
        kc_ref[0, g] = piece(qw, g).astype(BF16)
        vc_ref[0, g] = piece(qw + kvw, g).astype(BF16)
        ksl_ref[0, g] = _rms(piece(qw + 2 * kvw, g), ksn_ref[...]).astype(BF16)
        vsl_ref[0, g] = jnp.concatenate([piece(qw + 3 * kvw, g), ones_col], axis=-1).astype(BF16)
        kw_ref[0, g] = _rms(piece(qw + 4 * kvw, g), kwn_ref[...]).astype(BF16)
        vw_ref[0, g] = jnp.concatenate([piece(qw + 5 * kvw, g), ones_col], axis=-1).astype(BF16)
        gb = qw + 6 * kvw + g * n_gate
        gt_ref[0, g] = jnp.concatenate([jnp.zeros((tm, DH), F32), jax.nn.sigmoid(p_ref[:, gb:gb + n_gate]),
                                        gate_pad], axis=-1)


def _nsa_prep(p, q_norm, kslc_norm, kwin_norm, B, T, *, tm=256):
    G, HG, DH = NSA_GROUPS, NSA_HPG, NSA_DH
    nt = T // tm
    width = p.shape[1]
    gain = lambda: pl.BlockSpec((1, DH), lambda i: (0, 0))
    per_group = lambda last: pl.BlockSpec((1, G, tm, last), lambda i: (i // nt, 0, i % nt, 0))
    shp = lambda last, dt: jax.ShapeDtypeStruct((B, G, T, last), dt)
    return pl.pallas_call(
        _nsa_prep_kernel,
        grid=(B * nt,),
        in_specs=[pl.BlockSpec((tm, width), lambda i: (i, 0)), gain(), gain(), gain()],
        out_specs=[pl.BlockSpec((1, G, HG, tm, DH), lambda i: (i // nt, 0, 0, i % nt, 0)),
                   per_group(DH), per_group(DH), per_group(DH), per_group(LANES),
                   per_group(DH), per_group(LANES), per_group(LANES)],
        out_shape=[jax.ShapeDtypeStruct((B, G, HG, T, DH), BF16),
                   shp(DH, BF16), shp(DH, BF16), shp(DH, BF16), shp(LANES, BF16),
                   shp(DH, BF16), shp(LANES, BF16), shp(LANES, F32)],
        compiler_params=_cparams("parallel"),
        name="nsa_prep",
    )(p, q_norm.reshape(1, DH), kslc_norm.reshape(1, DH), kwin_norm.reshape(1, DH))


def _compress_kernel(rk_ref, rv_ref, pek_ref, pev_ref, kw1_ref, kb1_ref, kw2_ref,
                     vw1_ref, vb1_ref, vw2_ref, kn_ref, kcmp_ref, vcmp_ref):
    half = CMP_STRIDE * NSA_DH
    nrow = rk_ref.shape[2]

    def mlp(r_ref, pe_ref, w1_ref, b1_ref, w2_ref):
        r = r_ref[0, 0].astype(BF16)
        ya = _dot(r, w1_ref[:half, :])
        yb = _dot(r, w1_ref[half:, :])
        pe = jnp.broadcast_to(pe_ref[...], (8, 2 * half)).astype(BF16)
        c = _dot(pe, w1_ref[...])[0:1] + b1_ref[...]
        hid = ya + pltpu.roll(yb, nrow - 1, 0) + c
        return _dot(jax.nn.gelu(hid).astype(BF16), w2_ref[...])

    kcmp_ref[0, 0] = _rms(mlp(rk_ref, pek_ref, kw1_ref, kb1_ref, kw2_ref), kn_ref[...]).astype(BF16)
    vcmp_ref[0, 0] = mlp(rv_ref, pev_ref, vw1_ref, vb1_ref, vw2_ref).astype(BF16)


def _compress(rk, rv, pe_k, pe_v, k_w1, k_b1, k_w2, v_w1, v_b1, v_w2, kcmp_norm):
    B, G, nrow, half = rk.shape
    DH, HID = NSA_DH, CMP_HIDDEN
    full = lambda shape: pl.BlockSpec(shape, lambda b, g: (0,) * len(shape))
    r_spec = pl.BlockSpec((1, 1, nrow, half), lambda b, g: (b, g, 0, 0))
    o_spec = pl.BlockSpec((1, 1, nrow, DH), lambda b, g: (b, g, 0, 0))
    return pl.pallas_call(
        _compress_kernel,
        grid=(B, G),
        in_specs=[r_spec, r_spec, full((1, 2 * half)), full((1, 2 * half)),
                  full((2 * half, HID)), full((1, HID)), full((HID, DH)),
                  full((2 * half, HID)), full((1, HID)), full((HID, DH)), full((1, DH))],
        out_specs=[o_spec, o_spec],
        out_shape=[jax.ShapeDtypeStruct((B, G, nrow, DH), BF16)] * 2,
        compiler_params=_cparams("parallel", "parallel"),
        name="nsa_compress",
    )(rk, rv, pe_k.reshape(1, -1), pe_v.reshape(1, -1),
      k_w1.astype(BF16), k_b1.reshape(1, HID), k_w2.astype(BF16),
      v_w1.astype(BF16), v_b1.reshape(1, HID), v_w2.astype(BF16), kcmp_norm.reshape(1, DH))


def _split3(x):
    hi = x.astype(BF16)
    r1 = x - hi.astype(F32)
    mid = r1.astype(BF16)
    lo = (r1 - mid.astype(F32)).astype(BF16)
    return hi, mid, lo


def _nsa_cmp_kernel(q_ref, kc_ref, vc_ref, gt_ref, sl_ref, ov_ref, gm_ref,
                    oc_ref, sel_ref, act_ref, *, n_top):
    HG, DH, TQ = NSA_HPG, NSA_DH, NSA_Q_TILE
    s0 = pl.program_id(2) * TQ
    ncmp = kc_ref.shape[2]
    n_slc = ov_ref.shape[1]
    qs = q_ref[0, 0].reshape(HG * TQ, DH)
    gt = gt_ref[0, 0]

    def attend(nc):
        s = _dot_nt(qs, kc_ref[0, 0, :nc, :])
        j = lax.broadcasted_iota(jnp.int32, (1, nc), 1)
        t = s0 + lax.broadcasted_iota(jnp.int32, (TQ, 1), 0)
        mask = j * CMP_STRIDE + (CMP_BLOCK - 1) <= t
        mid = (j * CMP_STRIDE - s0).astype(F32) + 0.5 * (CMP_BLOCK - 1)
        has_key = t >= CMP_BLOCK - 1
        psum = jnp.zeros((TQ, nc), F32)
        ps = []
        for h in range(HG):
            rows = slice(h * TQ, (h + 1) * TQ)
            slope_row = jnp.concatenate([sl_ref[0, h * TQ:h * TQ + 1, :]] * (nc // LANES), axis=-1)
            sh = jnp.where(mask, s[rows] + slope_row * mid, NEG_INF)
            m = jnp.max(sh, axis=-1, keepdims=True)
            e = jnp.exp2(sh - m)
            l = jnp.sum(e, axis=-1, keepdims=True)
            inv = jnp.where(has_key, 1.0 / l, 0.0)
            p = e * inv
            psum = psum + p
            ps.append(p.astype(BF16))
        o_all = _dot(jnp.concatenate(ps, axis=0), vc_ref[0, 0, :nc, :])
        oc_ref[...] = jnp.concatenate(
            [o_all[h * TQ:(h + 1) * TQ] * gt[:, DH + 3 * h:DH + 3 * h + 1] for h in range(HG)], axis=-1)
        hi, md, lo = _split3(psum)
        ov = ov_ref[:nc, :]
        imp = _dot(hi, ov) + _dot(md, ov) + _dot(lo, ov)
        select(imp, nc * CMP_STRIDE // SLC_BLOCK)

    def select(imp, nb):
        imp_t = imp.T[:nb]
        blk = lax.broadcasted_iota(jnp.int32, (nb, TQ), 0)
        tq = s0 + lax.broadcasted_iota(jnp.int32, (nb, TQ), 1)
        cur = lax.shift_right_logical(tq, SLC_BLOCK.bit_length() - 1)
        forced = (blk == 0) | (blk == cur) | (blk == cur - 1)
        score = jnp.where(blk * SLC_BLOCK <= tq, imp_t + jnp.where(forced, FORCE_SCORE, 0.0), -1.0)

        def pick(_, carry):
            score, sel = carry
            mx = jnp.max(score, axis=0, keepdims=True)
            first = jnp.min(jnp.where(score == mx, blk, nb), axis=0, keepdims=True)
            hit = blk == first
            return jnp.where(hit, -2.0, score), jnp.where(hit, 1.0, sel)

        _, sel_t = lax.fori_loop(0, n_top, pick, (score, jnp.zeros((nb, TQ), F32)), unroll=True)
        if nb < n_slc:
            sel_t = jnp.concatenate([sel_t, jnp.zeros((n_slc - nb, TQ), F32)], axis=0)
        sel = sel_t.T.astype(BF16)
        sel_ref[0, 0] = sel
        cnt = _dot(jnp.ones((8, TQ), BF16), sel)
        act_ref[0, 0, 0] = _dot((cnt > 0.0).astype(BF16), gm_ref[...])

    need = (s0 + TQ) // CMP_STRIDE
    widths = list(range(LANES, ncmp + 1, LANES))
    for idx, nc in enumerate(widths):
        lo_w = widths[idx - 1] if idx else 0
        pl.when((need > lo_w) & (need <= nc))(functools.partial(attend, nc))


def _nsa_cmp(q, kcmp, vcmp, gates, slope_tab, overlap, group_mat, B, T, n_top):
    G, HG, DH, TQ = NSA_GROUPS, NSA_HPG, NSA_DH, NSA_Q_TILE
    nq = T // TQ
    ncmp = kcmp.shape[2]
    n_slc = overlap.shape[1]
    return pl.pallas_call(
        functools.partial(_nsa_cmp_kernel, n_top=n_top),
        grid=(B, G, nq),
        in_specs=[pl.BlockSpec((1, 1, HG, TQ, DH), lambda b, g, i: (b, g, 0, i, 0)),
                  pl.BlockSpec((1, 1, ncmp, DH), lambda b, g, i: (b, g, 0, 0)),
                  pl.BlockSpec((1, 1, ncmp, DH), lambda b, g, i: (b, g, 0, 0)),
                  pl.BlockSpec((1, 1, TQ, LANES), lambda b, g, i: (b, g, i, 0)),
                  pl.BlockSpec((1, HG * TQ, LANES), lambda b, g, i: (g, 0, 0)),
                  pl.BlockSpec((ncmp, n_slc), lambda b, g, i: (0, 0)),
                  pl.BlockSpec((n_slc, LANES), lambda b, g, i: (0, 0))],
        out_specs=[pl.BlockSpec((TQ, HG * DH), lambda b, g, i: (b * nq + i, g)),
                   pl.BlockSpec((1, 1, TQ, n_slc), lambda b, g, i: (b, g, i, 0)),
                   pl.BlockSpec((1, 1, 1, 8, LANES), lambda b, g, i: (b, g, i, 0, 0))],
        out_shape=[jax.ShapeDtypeStruct((B * T, NSA_HEADS * DH), F32),
                   jax.ShapeDtypeStruct((B, G, T, n_slc), BF16),
                   jax.ShapeDtypeStruct((B, G, nq, 8, LANES), F32)],
        compiler_params=_cparams("parallel", "parallel", "arbitrary"),
        name="nsa_cmp_select",
    )(q, kcmp, vcmp, gates, slope_tab, overlap, group_mat)


def _nsa_main_kernel(act_ref, q_ref, sel_ref, oc_ref, gt_ref, sl_ref, ex_ref,
                     ksl_ref, vsl_ref, kw_ref, vw_ref, o_ref,
                     m_sc, acc_sc, s0_sc, s1_sc, pk0_sc, pk1_sc, list_sc, *, n_ktiles):
    HG, DH, TQ, KT = NSA_HPG, NSA_DH, NSA_Q_TILE, NSA_KEY_TILE
    s_slots, pk_slots = (s0_sc, s1_sc), (pk0_sc, pk1_sc)
    WK = WINDOW + TQ
    b, g, qi = pl.program_id(0), pl.program_id(1), pl.program_id(2)
    nq = pl.num_programs(2)
    s0 = qi * TQ
    qs = q_ref[0, 0].reshape(HG * TQ, DH)
    selb = sel_ref[0, 0]
    gt = gt_ref[0, 0]
    t_col = s0 + lax.broadcasted_iota(jnp.int32, (TQ, 1), 0)

    def biased(s, kpos, mask, h):
        width = kpos.shape[1]
        slope_row = jnp.concatenate([sl_ref[0, h * TQ:h * TQ + 1, :]] * (width // LANES), axis=-1)
        sh = s[h * TQ:(h + 1) * TQ] + slope_row * (kpos - s0).astype(F32)
        return jnp.where(mask, sh, NEG_INF)

    m_sc[...] = jnp.full_like(m_sc, NEG_INF)
    acc_sc[...] = jnp.zeros_like(acc_sc)
    act_base = ((b * NSA_GROUPS + g) * nq + qi) * n_ktiles
    list_sc[0] = 0

    def compact(i, n):
        hit = act_ref[act_base + i] > 0

        @pl.when(hit)
        def _():
            list_sc[n] = i
        return n + hit.astype(jnp.int32)

    n_act = jnp.maximum(lax.fori_loop(0, qi + 1, compact, 0), 1)

    def scores(j, slot):
        k0 = pl.multiple_of(list_sc[j] * KT, KT)
        s_slots[slot][...] = _dot_nt(qs, ksl_ref[0, 0, pl.ds(k0, KT), :])
        pk_slots[slot][...] = _dot(selb, ex_ref[:, pl.ds(k0, KT)])

    def update(j, slot):
        k0 = pl.multiple_of(list_sc[j] * KT, KT)
        kpos = k0 + lax.broadcasted_iota(jnp.int32, (1, KT), 1)
        mask = (pk_slots[slot][...] > 0.5) & (kpos <= t_col)
        ps, alphas = [], []
        for h in range(HG):
            rows = slice(h * TQ, (h + 1) * TQ)
            sh = biased(s_slots[slot], kpos, mask, h)
            m_old = m_sc[rows]
            m_new = jnp.maximum(m_old, jnp.max(sh, axis=-1, keepdims=True))
            ps.append(jnp.exp2(sh - jnp.concatenate([m_new] * (KT // LANES), axis=-1)).astype(BF16))
            alphas.append(jnp.exp2(m_old - m_new))
            m_sc[rows] = m_new
        pv = _dot(jnp.concatenate(ps, axis=0), vsl_ref[0, 0, pl.ds(k0, KT), :])
        acc_sc[...] = jnp.concatenate(alphas, axis=0) * acc_sc[...] + pv

    def stage(j, slot):
        scores(j + 1, 1 - slot)
        update(j, slot)

    scores(0, 0)
    _two_slot_pipeline(n_act - 1, stage, update)

    start = pl.multiple_of(jnp.maximum(s0 - WINDOW, 0), TQ)
    kpos_w = start + lax.broadcasted_iota(jnp.int32, (1, WK), 1)
    mask_w = (kpos_w <= t_col) & (kpos_w > t_col - WINDOW)
    s_w = _dot_nt(qs, kw_ref[0, 0, pl.ds(start, WK), :])
    pw = []
    for h in range(HG):
        sh = biased(s_w, kpos_w, mask_w, h)
        pw.append(jnp.exp2(sh - jnp.max(sh, axis=-1, keepdims=True)).astype(BF16))
    acc_w = _dot(jnp.concatenate(pw, axis=0), vw_ref[0, 0, pl.ds(start, WK), :])

    outs = []
    for h in range(HG):
        rows = slice(h * TQ, (h + 1) * TQ)
        a_s, a_w = acc_sc[rows], acc_w[rows]
        r_s, r_w = gt / a_s, gt / a_w
        outs.append(oc_ref[:, h * DH:(h + 1) * DH]
                    + r_s[:, DH + 3 * h + 1:DH + 3 * h + 2] * a_s[:, :DH]
                    + r_w[:, DH + 3 * h + 2:DH + 3 * h + 3] * a_w[:, :DH])
    o_ref[...] = jnp.concatenate(outs, axis=-1)


def _nsa_main(act, q, sel, oc, gates, slope_tab, expand, ksl, vsl, kw, vw, B, T):
    G, HG, DH, TQ = NSA_GROUPS, NSA_HPG, NSA_DH, NSA_Q_TILE
    nq = T // TQ
    n_slc = sel.shape[3]
    n_ktiles = T // NSA_KEY_TILE
    kv = lambda last: pl.BlockSpec((1, 1, T, last), lambda b, g, i, a: (b, g, 0, 0))
    grid_spec = pltpu.PrefetchScalarGridSpec(
        num_scalar_prefetch=1,
        grid=(B, G, nq),
        in_specs=[pl.BlockSpec((1, 1, HG, TQ, DH), lambda b, g, i, a: (b, g, 0, i, 0)),
                  pl.BlockSpec((1, 1, TQ, n_slc), lambda b, g, i, a: (b, g, i, 0)),
                  pl.BlockSpec((TQ, HG * DH), lambda b, g, i, a: (b * nq + i, g)),
                  pl.BlockSpec((1, 1, TQ, LANES), lambda b, g, i, a: (b, g, i, 0)),
                  pl.BlockSpec((1, HG * TQ, LANES), lambda b, g, i, a: (g, 0, 0)),
                  pl.BlockSpec((n_slc, T), lambda b, g, i, a: (0, 0)),
                  kv(DH), kv(LANES), kv(DH), kv(LANES)],
        out_specs=pl.BlockSpec((TQ, HG * DH), lambda b, g, i, a: (b * nq + i, g)),
        scratch_shapes=[pltpu.VMEM((HG * TQ, LANES), F32), pltpu.VMEM((HG * TQ, LANES), F32),
                        pltpu.VMEM((HG * TQ, NSA_KEY_TILE), F32), pltpu.VMEM((HG * TQ, NSA_KEY_TILE), F32),
                        pltpu.VMEM((TQ, NSA_KEY_TILE), F32), pltpu.VMEM((TQ, NSA_KEY_TILE), F32),
                        pltpu.SMEM((n_ktiles + 1,), jnp.int32)],
    )
    return pl.pallas_call(
        functools.partial(_nsa_main_kernel, n_ktiles=n_ktiles),
        grid_spec=grid_spec,
        out_shape=jax.ShapeDtypeStruct((B * T, NSA_HEADS * DH), F32),
        compiler_params=_cparams("parallel", "parallel", "arbitrary"),
        name="nsa_select_window",
    )(act, q, sel, oc, gates, slope_tab, expand, ksl, vsl, kw, vw)


def _nsa_mixer(h_proj, B, T, q_norm, kcmp_norm, kslc_norm, kwin_norm, pos_k, pos_v,
               k_w1, k_b1, k_w2, v_w1, v_b1, v_w2):
    G, HG, DH, TQ = NSA_GROUPS, NSA_HPG, NSA_DH, NSA_Q_TILE
    q, kc, vc, ksl, vsl, kw, vw, gates = _nsa_prep(h_proj, q_norm, kslc_norm, kwin_norm, B, T)

    nrow = T // CMP_STRIDE
    half = CMP_STRIDE * DH
    kcmp, vcmp = _compress(kc.reshape(B, G, nrow, half), vc.reshape(B, G, nrow, half),
                           pos_k, pos_v, k_w1, k_b1, k_w2, v_w1, v_b1, v_w2, kcmp_norm)

    n_slc = T // SLC_BLOCK
    n_top = min(SLC_TOP, n_slc)
    n_slc_pad = max(n_slc, LANES)
    slopes = 2.0 ** (-8.0 * jnp.arange(1, NSA_HEADS + 1, dtype=F32) / NSA_HEADS) * LOG2E
    slope_tab = jnp.broadcast_to(slopes.reshape(G, HG, 1, 1), (G, HG, TQ, LANES)).reshape(G, HG * TQ, LANES)
    cj = jnp.arange(nrow)[:, None] * CMP_STRIDE
    si = jnp.arange(n_slc_pad)[None, :] * SLC_BLOCK
    overlap = ((cj <= si + SLC_BLOCK - 1) & (cj + CMP_BLOCK - 1 >= si)
               & (jnp.arange(nrow)[:, None] < nrow - 1)).astype(BF16)
    blocks_per_tile = NSA_KEY_TILE // SLC_BLOCK
    group_mat = (jnp.arange(n_slc_pad)[:, None] // blocks_per_tile
                 == jnp.arange(LANES)[None, :]).astype(BF16)
    expand = (jnp.arange(n_slc_pad)[:, None] == jnp.arange(T)[None, :] // SLC_BLOCK).astype(BF16)

    oc, sel, act = _nsa_cmp(q, kcmp, vcmp, gates, slope_tab, overlap, group_mat, B, T, n_top)
    n_ktiles = T // NSA_KEY_TILE
    act_i = act[:, :, :, 0, :n_ktiles].astype(jnp.int32).reshape(-1)
    return _nsa_main(act_i, q, sel, oc, gates, slope_tab, expand, ksl, vsl, kw, vw, B, T)


def _rope_layout(w):
    half = QK_ROPE // 2
    z = jnp.zeros(w.shape[:-1] + (LANES // 2 - half,), w.dtype)
    return jnp.concatenate([w[..., :half], z, w[..., half:], z], axis=-1)


def _head_layout(w):
    w = w.reshape(w.shape[:-1] + (MLA_HEADS, MLA_QK))
    w = jnp.concatenate([w[..., :QK_NOPE], _rope_layout(w[..., QK_NOPE:])], axis=-1)
    return w.reshape(w.shape[:-2] + (MLA_HEADS * MLA_QK_PAD,))


def _norm_rope(nope, rope, ss, gn_ref, gr_ref, cos_ref, sin_ref, scale):
    inv = lax.rsqrt(ss * (1.0 / MLA_QK) + NORM_EPS)
    a = nope * inv * gn_ref[...]
    r = rope * inv * gr_ref[...]
    if scale != 1.0:
        a, r = a * scale, r * scale
    r = r * cos_ref[...] + pltpu.roll(r, LANES // 2, 1) * sin_ref[...]
    return jnp.concatenate([a, r], axis=-1).astype(BF16)


def _mla_q_kernel(q_ref, gn_ref, gr_ref, cos_ref, sin_ref, o_ref):
    for h in range(MLA_HEADS):
        nope = q_ref[:, h * MLA_QK_PAD:h * MLA_QK_PAD + QK_NOPE]
        rope = q_ref[:, h * MLA_QK_PAD + QK_NOPE:(h + 1) * MLA_QK_PAD]
        ss = jnp.sum(nope * nope + rope * rope, axis=-1, keepdims=True)
        o_ref[0, h] = _norm_rope(nope, rope, ss, gn_ref, gr_ref, cos_ref, sin_ref, MLA_QK ** -0.5 * LOG2E)


def _mla_kv_kernel(kv_ref, kpe_ref, gn_ref, gr_ref, cos_ref, sin_ref, k_ref, v_ref):
    rope = kpe_ref[...]
    for h in range(MLA_HEADS):
        base = h * (QK_NOPE + MLA_V)
        nope = kv_ref[:, base:base + QK_NOPE]
        ss = jnp.sum(nope * nope + rope * rope, axis=-1, keepdims=True)
        k_ref[0, h] = _norm_rope(nope, rope, ss, gn_ref, gr_ref, cos_ref, sin_ref, 1.0)
        v_ref[0, h] = kv_ref[:, base + QK_NOPE:base + QK_NOPE + MLA_V].astype(BF16)


def _mla_prep_q(q, gain, cos_l, sin_l, B, T, *, tm=256):
    nt = T // tm
    row = lambda: pl.BlockSpec((1, LANES), lambda i: (0, 0))
    tab = lambda: pl.BlockSpec((tm, LANES), lambda i: (i % nt, 0))
    return pl.pallas_call(
        _mla_q_kernel,
        grid=(B * nt,),
        in_specs=[pl.BlockSpec((tm, MLA_HEADS * MLA_QK_PAD), lambda i: (i, 0)), row(), row(), tab(), tab()],
        out_specs=pl.BlockSpec((1, MLA_HEADS, tm, MLA_QK_PAD), lambda i: (i // nt, 0, i % nt, 0)),
        out_shape=jax.ShapeDtypeStruct((B, MLA_HEADS, T, MLA_QK_PAD), BF16),
        compiler_params=_cparams("parallel"),
        name="mla_prep_q",
    )(q, gain[:QK_NOPE].reshape(1, LANES), _rope_layout(gain[QK_NOPE:]).reshape(1, LANES), cos_l, sin_l)


def _mla_prep_kv(kv, kv_a, gain, cos_l, sin_l, B, T, *, tm=256):
    nt = T // tm
    pe_block = KV_LORA // LANES
    row = lambda: pl.BlockSpec((1, LANES), lambda i: (0, 0))
    tab = lambda: pl.BlockSpec((tm, LANES), lambda i: (i % nt, 0))
    return pl.pallas_call(
        _mla_kv_kernel,
        grid=(B * nt,),
        in_specs=[pl.BlockSpec((tm, MLA_HEADS * (QK_NOPE + MLA_V)), lambda i: (i, 0)),
                  pl.BlockSpec((tm, LANES), lambda i: (i, pe_block)), row(), row(), tab(), tab()],
        out_specs=[pl.BlockSpec((1, MLA_HEADS, tm, MLA_QK_PAD), lambda i: (i // nt, 0, i % nt, 0)),
                   pl.BlockSpec((1, MLA_HEADS, tm, MLA_V), lambda i: (i // nt, 0, i % nt, 0))],
        out_shape=[jax.ShapeDtypeStruct((B, MLA_HEADS, T, MLA_QK_PAD), BF16),
                   jax.ShapeDtypeStruct((B, MLA_HEADS, T, MLA_V), BF16)],
        compiler_params=_cparams("parallel"),
        name="mla_prep_kv",
    )(kv, kv_a, gain[:QK_NOPE].reshape(1, LANES), _rope_layout(gain[QK_NOPE:]).reshape(1, LANES), cos_l, sin_l)


def _two_slot_pipeline(last, stage, finish):
    def pair(j, carry):
        stage(2 * j, 0)
        stage(2 * j + 1, 1)
        return carry

    lax.fori_loop(0, last // 2, pair, 0)

    @pl.when(last % 2 == 1)
    def _():
        stage(last - 1, 0)
        finish(last, 1)

    @pl.when(last % 2 == 0)
    def _():
        finish(last, 0)


def _mla_attn_kernel(q_ref, k_ref, v_ref, o_ref, m_sc, l_sc, acc_sc, s0_sc, s1_sc):
    TQ = MLA_TILE
    qi = pl.program_id(2)
    q = q_ref[0, 0]
    s_slots = (s0_sc, s1_sc)
    m_sc[...] = jnp.full_like(m_sc, NEG_INF)
    l_sc[...] = jnp.zeros_like(l_sc)
    acc_sc[...] = jnp.zeros_like(acc_sc)

    def scores(i, slot):
        k0 = pl.multiple_of(i * TQ, TQ)
        s_slots[slot][...] = _dot_nt(q, k_ref[0, 0, pl.ds(k0, TQ), :])

    def update(i, slot, masked):
        k0 = pl.multiple_of(i * TQ, TQ)

        def load():
            s = s_slots[slot][...]
            if masked:
                qq = lax.broadcasted_iota(jnp.int32, (TQ, TQ), 0)
                kk = lax.broadcasted_iota(jnp.int32, (TQ, TQ), 1)
                s = jnp.where(kk <= qq, s, NEG_INF)
            return s

        m_old = m_sc[...]
        m_new = jnp.maximum(m_old, jnp.max(load(), axis=-1, keepdims=True))
        p = jnp.exp2(load() - jnp.concatenate([m_new] * (TQ // LANES), axis=-1))
        alpha = jnp.exp2(m_old - m_new)
        l_sc[...] = alpha * l_sc[...] + jnp.sum(p, axis=-1, keepdims=True)
        acc_sc[...] = alpha * acc_sc[...] + _dot(p.astype(BF16), v_ref[0, 0, pl.ds(k0, TQ), :])
        m_sc[...] = m_new

    def stage(i, slot):
        scores(i + 1, 1 - slot)
        update(i, slot, False)

    scores(0, 0)
    _two_slot_pipeline(qi, stage, lambda i, slot: update(i, slot, True))
    o_ref[...] = acc_sc[...] / l_sc[...]


def _mla_attn(q, k, v, B, T):
    H, TQ = MLA_HEADS, MLA_TILE
    nq = T // TQ
    return pl.pallas_call(
        _mla_attn_kernel,
        grid=(B, H, nq),
        in_specs=[pl.BlockSpec((1, 1, TQ, MLA_QK_PAD), lambda b, h, i: (b, h, i, 0)),
                  pl.BlockSpec((1, 1, T, MLA_QK_PAD), lambda b, h, i: (b, h, 0, 0)),
                  pl.BlockSpec((1, 1, T, MLA_V), lambda b, h, i: (b, h, 0, 0))],
        out_specs=pl.BlockSpec((TQ, MLA_V), lambda b, h, i: (b * nq + i, h)),
        out_shape=jax.ShapeDtypeStruct((B * T, H * MLA_V), F32),
        scratch_shapes=[pltpu.VMEM((TQ, LANES), F32), pltpu.VMEM((TQ, LANES), F32),
                        pltpu.VMEM((TQ, MLA_V), F32),
                        pltpu.VMEM((TQ, TQ), F32), pltpu.VMEM((TQ, TQ), F32)],
        compiler_params=_cparams("parallel", "parallel", "arbitrary"),
        name="mla_flash_attn",
    )(q, k, v)


def _pad_cols(w, mult=LANES):
    pad = -w.shape[1] % mult
    return jnp.pad(w, ((0, 0), (0, pad))) if pad else w


def kernel(x, a_attn_norm, a_w_in, a_q_norm, a_kcmp_norm, a_kslc_norm, a_kwin_norm, a_cmp_pos_k, a_cmp_pos_v, a_cmp_k_w1, a_cmp_k_b1, a_cmp_k_w2, a_cmp_v_w1, a_cmp_v_b1, a_cmp_v_w2, a_w_out, kv_norm, kv_w_a, kv_c_norm, kv_w_b, kv_k_norm, b_attn_norm, b_w_q_a, b_q_a_norm, b_w_q_b, b_q_norm, b_w_out, ffn_norm, ffn_w_gate_up, ffn_w_down):
    B, T, D = x.shape
    n_a = a_w_in.shape[0]
    n_b = b_w_q_a.shape[0]
    xs = x.reshape(B * T, D)

    inv = ROPE_THETA ** (-jnp.arange(0, QK_ROPE, 2, dtype=F32) / QK_ROPE)
    ang = jnp.arange(T, dtype=F32)[:, None] * inv[None, :]
    cos, sin = jnp.cos(ang), jnp.sin(ang)
    cos2 = _rope_layout(jnp.concatenate([cos, cos], axis=-1))
    sin2 = _rope_layout(jnp.concatenate([-sin, sin], axis=-1))

    k_shared = v_shared = None
    for layer in range(n_a + n_b):
        if layer < n_a:
            i = layer
            proj = _linear(xs, _pad_cols(a_w_in[i]).astype(BF16), gain=a_attn_norm[i], name="nsa_in_proj")
            o = _nsa_mixer(proj, B, T, a_q_norm[i], a_kcmp_norm[i], a_kslc_norm[i], a_kwin_norm[i],
                           a_cmp_pos_k[i], a_cmp_pos_v[i], a_cmp_k_w1[i], a_cmp_k_b1[i], a_cmp_k_w2[i],
                           a_cmp_v_w1[i], a_cmp_v_b1[i], a_cmp_v_w2[i])
            xs = _linear(o, a_w_out[i].astype(BF16), residual=xs, name="nsa_out_proj")
        else:
            j = layer - n_a
            qa = _linear(xs, b_w_q_a[j].astype(BF16), gain=b_attn_norm[j], name="mla_q_a_proj")
            qb = _linear(qa, _head_layout(b_w_q_b[j]).astype(BF16), gain=b_q_a_norm[j], name="mla_q_b_proj")
            q = _mla_prep_q(qb, b_q_norm[j], cos2, sin2, B, T)
            o = _mla_attn(q, k_shared, v_shared, B, T)
            xs = _linear(o, b_w_out[j].astype(BF16), residual=xs, name="mla_out_proj")
        xs = _ffn(xs, ffn_norm[layer], ffn_w_gate_up[layer].astype(BF16), ffn_w_down[layer].astype(BF16))
        if layer == n_a - 1:
            w_kv_a = jnp.concatenate([kv_w_a[:, :KV_LORA], _rope_layout(kv_w_a[:, KV_LORA:])], axis=-1)
            kv_a = _linear(xs, w_kv_a.astype(BF16), gain=kv_norm, name="mla_kv_a_proj")
            kv = _linear(kv_a, kv_w_b.astype(BF16), gain=kv_c_norm, k_cols=KV_LORA, name="mla_kv_b_proj")
            k_shared, v_shared = _mla_prep_kv(kv, kv_a, kv_k_norm, cos2, sin2, B, T)
    return xs.reshape(B, T, D)
```

```python
import functools

import jax
import jax.numpy as jnp
from jax import lax
from jax.experimental import pallas as pl
from jax.experimental.pallas import tpu as pltpu

F32 = jnp.float32
BF16 = jnp.bfloat16

NORM_EPS = 1e-6
NEG_INF = -1e30
LANES = 128

NSA_HEADS = 16
NSA_GROUPS = 4
NSA_HPG = NSA_HEADS // NSA_GROUPS
NSA_DH = 64
CMP_BLOCK = 32
CMP_STRIDE = 16
CMP_HIDDEN = 256
SLC_BLOCK = 64
SLC_TOP = 16
WINDOW = 512
FORCE_SCORE = 1e4
NSA_Q_TILE = 256
NSA_KEY_TILE = 256
LOG2E = 1.4426950408889634

MLA_HEADS = 8
QK_NOPE = 128
QK_ROPE = 64
MLA_QK = QK_NOPE + QK_ROPE
MLA_QK_PAD = QK_NOPE + LANES
MLA_V = 128
Q_LORA = 384
KV_LORA = 256
ROPE_THETA = 10000.0
MLA_TILE = 512

VMEM_LIMIT = 56 * 1024 * 1024


def _cparams(*sem):
    return pltpu.CompilerParams(dimension_semantics=sem, vmem_limit_bytes=VMEM_LIMIT)


def _rms(x, g):
    return x * lax.rsqrt(jnp.mean(x * x, axis=-1, keepdims=True) + NORM_EPS) * g


def _dot(a, b):
    return jnp.dot(a, b, preferred_element_type=F32)


def _dot_nt(a, b):
    return lax.dot_general(a, b, (((1,), (1,)), ((), ())), preferred_element_type=F32)


def _linear_kernel(*refs, has_gain, has_res):
    it = iter(refs)
    x_ref = next(it)
    w_ref = next(it)
    g_ref = next(it) if has_gain else None
    r_ref = next(it) if has_res else None
    o_ref = next(it)
    x = x_ref[...]
    if has_gain:
        x = _rms(x.astype(F32), g_ref[...])
    acc = _dot(x.astype(BF16), w_ref[...])
    if has_res:
        acc = acc + r_ref[...]
    o_ref[...] = acc.astype(o_ref.dtype)


def _linear(x, w, gain=None, residual=None, *, name, k_cols=None, tm=512, tn=None, out_dtype=F32):
    n = x.shape[0]
    k, m = w.shape
    if k_cols is None:
        assert x.shape[1] == k
    tn = m if tn is None else tn
    assert n % tm == 0 and m % tn == 0
    in_specs = [pl.BlockSpec((tm, k), lambda i, j: (i, 0)),
                pl.BlockSpec((k, tn), lambda i, j: (0, j))]
    args = [x, w]
    if gain is not None:
        in_specs.append(pl.BlockSpec((1, k), lambda i, j: (0, 0)))
        args.append(gain.reshape(1, k).astype(F32))
    if residual is not None:
        in_specs.append(pl.BlockSpec((tm, tn), lambda i, j: (i, j)))
        args.append(residual)
    return pl.pallas_call(
        functools.partial(_linear_kernel, has_gain=gain is not None, has_res=residual is not None),
        grid=(n // tm, m // tn),
        in_specs=in_specs,
        out_specs=pl.BlockSpec((tm, tn), lambda i, j: (i, j)),
        out_shape=jax.ShapeDtypeStruct((n, m), out_dtype),
        compiler_params=_cparams("parallel", "arbitrary"),
        name=name,
    )(*args)


def _ffn_kernel(x_ref, g_ref, wgu_ref, wd_ref, o_ref):
    hid = wd_ref.shape[0]
    x = x_ref[...]
    h = _rms(x, g_ref[...]).astype(BF16)
    gate = _dot(h, wgu_ref[:, :hid])
    up = _dot(h, wgu_ref[:, hid:])
    a = (gate * jax.nn.sigmoid(gate) * up).astype(BF16)
    o_ref[...] = x + _dot(a, wd_ref[...])


def _ffn(x, gain, w_gate_up, w_down, *, tm=512):
    n, d = x.shape
    hid = w_down.shape[0]
    assert n % tm == 0 and hid % LANES == 0
    resident = lambda shape: pl.BlockSpec(shape, lambda i: (0, 0), pipeline_mode=pl.Buffered(1))
    return pl.pallas_call(
        _ffn_kernel,
        grid=(n // tm,),
        in_specs=[pl.BlockSpec((tm, d), lambda i: (i, 0)),
                  pl.BlockSpec((1, d), lambda i: (0, 0)),
                  resident((d, 2 * hid)), resident((hid, d))],
        out_specs=pl.BlockSpec((tm, d), lambda i: (i, 0)),
        out_shape=jax.ShapeDtypeStruct((n, d), F32),
        compiler_params=_cparams("parallel"),
        name="swiglu_ffn",
    )(x, gain.reshape(1, d), w_gate_up, w_down)


def _split2(x):
    hi = x.astype(BF16)
    return hi, (x - hi.astype(F32)).astype(BF16)


def _nsa_prep_kernel(p_ref, gain_ref, gsum_ref, gexp_ref,
                     q_ref, kc_ref, vc_ref, ksl_ref, vsl_ref, kw_ref, vw_ref, gt_ref):
    G, HG, DH = NSA_GROUPS, NSA_HPG, NSA_DH
    qw, kvw = NSA_HEADS * DH, G * DH
    tm = p_ref.shape[0]
    ones_col = jnp.ones((tm, LANES - DH), F32)
    n_gate = HG * 3
    gate_pad = jnp.zeros((tm, LANES - DH - n_gate), F32)

    def piece(base, g):
        return p_ref[:, base + g * DH: base + (g + 1) * DH]

    xn = jnp.concatenate([p_ref[:, :qw], p_ref[:, qw + 2 * kvw:qw + 3 * kvw],
                          p_ref[:, qw + 4 * kvw:qw + 5 * kvw]], axis=-1)
    hi, lo = _split2(xn * xn)
    ss = _dot(hi, gsum_ref[...]) + _dot(lo, gsum_ref[...])
    hi, lo = _split2(lax.rsqrt(ss * (1.0 / DH) + NORM_EPS))
    xn = xn * (_dot(hi, gexp_ref[...]) + _dot(lo, gexp_ref[...])) * gain_ref[...]

    for g in range(G):
        for h in range(HG):
            q_ref[0, g, h] = xn[:, (g * HG + h) * DH:(g * HG + h + 1) * DH].astype(BF16)
        kc_ref[0, g] = piece(qw, g).astype(BF16)
        vc_ref[0, g] = piece(qw + kvw, g).astype(BF16)
        ksl_ref[0, g] = xn[:, qw + g * DH:qw + (g + 1) * DH].astype(BF16)
        vsl_ref[0, g] = jnp.concatenate([piece(qw + 3 * kvw, g), ones_col], axis=-1).astype(BF16)
        kw_ref[0, g] = xn[:, qw + kvw + g * DH:qw + kvw + (g + 1) * DH].astype(BF16)
        vw_ref[0, g] = jnp.concatenate([piece(qw + 5 * kvw, g), ones_col], axis=-1).astype(BF16)
        gb = qw + 6 * kvw + g * n_gate
        gt_ref[0, g] = jnp.concatenate([jnp.zeros((tm, DH), F32), jax.nn.sigmoid(p_ref[:, gb:gb + n_gate]),
                                        gate_pad], axis=-1)


def _nsa_prep(p, q_norm, kslc_norm, kwin_norm, B, T, *, tm=256):
    G, HG, DH = NSA_GROUPS, NSA_HPG, NSA_DH
    nt = T // tm
    width = p.shape[1]
    n_norm = (NSA_HEADS + 2 * G) * DH
    gain = jnp.concatenate([jnp.tile(q_norm, NSA_HEADS) * (DH ** -0.5 * LOG2E),
                            jnp.tile(kslc_norm, G), jnp.tile(kwin_norm, G)]).reshape(1, n_norm)
    gsum = (jnp.arange(n_norm)[:, None] // DH == jnp.arange(LANES)[None, :]).astype(BF16)
    full = lambda shape: pl.BlockSpec(shape, lambda i: (0, 0))
    per_group = lambda last: pl.BlockSpec((1, G, tm, last), lambda i: (i // nt, 0, i % nt, 0))
    shp = lambda last, dt: jax.ShapeDtypeStruct((B, G, T, last), dt)
    return pl.pallas_call(
        _nsa_prep_kernel,
        grid=(B * nt,),
        in_specs=[pl.BlockSpec((tm, width), lambda i: (i, 0)),
                  full((1, n_norm)), full((n_norm, LANES)), full((LANES, n_norm))],
        out_specs=[pl.BlockSpec((1, G, HG, tm, DH), lambda i: (i // nt, 0, 0, i % nt, 0)),
                   per_group(DH), per_group(DH), per_group(DH), per_group(LANES),
                   per_group(DH), per_group(LANES), per_group(LANES)],
        out_shape=[jax.ShapeDtypeStruct((B, G, HG, T, DH), BF16),
                   shp(DH, BF16), shp(DH, BF16), shp(DH, BF16), shp(LANES, BF16),
                   shp(DH, BF16), shp(LANES, BF16), shp(LANES, F32)],
        compiler_params=_cparams("parallel"),
        name="nsa_prep",
    )(p, gain, gsum, gsum.T)


def _compress_kernel(rk_ref, rv_ref, pek_ref, pev_ref, kw1_ref, kb1_ref, kw2_ref,
                     vw1_ref, vb1_ref, vw2_ref, kn_ref, kcmp_ref, vcmp_ref):
    half = CMP_STRIDE * NSA_DH
    nrow = rk_ref.shape[2]

    def mlp(r_ref, pe_ref, w1_ref, b1_ref, w2_ref):
        r = r_ref[0, 0].astype(BF16)
        ya = _dot(r, w1_ref[:half, :])
        yb = _dot(r, w1_ref[half:, :])
        pe = jnp.broadcast_to(pe_ref[...], (8, 2 * half)).astype(BF16)
        c = _dot(pe, w1_ref[...])[0:1] + b1_ref[...]
        hid = ya + pltpu.roll(yb, nrow - 1, 0) + c
        return _dot(jax.nn.gelu(hid).astype(BF16), w2_ref[...])

    kcmp_ref[0, 0] = _rms(mlp(rk_ref, pek_ref, kw1_ref, kb1_ref, kw2_ref), kn_ref[...]).astype(BF16)
    vcmp_ref[0, 0] = mlp(rv_ref, pev_ref, vw1_ref, vb1_ref, vw2_ref).astype(BF16)


def _compress(rk, rv, pe_k, pe_v, k_w1, k_b1, k_w2, v_w1, v_b1, v_w2, kcmp_norm):
    B, G, nrow, half = rk.shape
    DH, HID = NSA_DH, CMP_HIDDEN
    full = lambda shape: pl.BlockSpec(shape, lambda b, g: (0,) * len(shape))
    r_spec = pl.BlockSpec((1, 1, nrow, half), lambda b, g: (b, g, 0, 0))
    o_spec = pl.BlockSpec((1, 1, nrow, DH), lambda b, g: (b, g, 0, 0))
    return pl.pallas_call(
        _compress_kernel,
        grid=(B, G),
        in_specs=[r_spec, r_spec, full((1, 2 * half)), full((1, 2 * half)),
                  full((2 * half, HID)), full((1, HID)), full((HID, DH)),
                  full((2 * half, HID)), full((1, HID)), full((HID, DH)), full((1, DH))],
        out_specs=[o_spec, o_spec],
        out_shape=[jax.ShapeDtypeStruct((B, G, nrow, DH), BF16)] * 2,
        compiler_params=_cparams("parallel", "parallel"),
        name="nsa_compress",
    )(rk, rv, pe_k.reshape(1, -1), pe_v.reshape(1, -1),
      k_w1.astype(BF16), k_b1.reshape(1, HID), k_w2.astype(BF16),
      v_w1.astype(BF16), v_b1.reshape(1, HID), v_w2.astype(BF16), kcmp_norm.reshape(1, DH))


def _split3(x):
    hi = x.astype(BF16)
    r1 = x - hi.astype(F32)
    mid = r1.astype(BF16)
    lo = (r1 - mid.astype(F32)).astype(BF16)
    return hi, mid, lo


def _nsa_cmp_kernel(q_ref, kc_ref, vc_ref, gt_ref, sl_ref, ov_ref, gm_ref,
                    oc_ref, sel_ref, act_ref, *, n_top):
    HG, DH, TQ = NSA_HPG, NSA_DH, NSA_Q_TILE
    s0 = pl.program_id(2) * TQ
    ncmp = kc_ref.shape[2]
    n_slc = ov_ref.shape[1]
    qs = q_ref[0, 0].reshape(HG * TQ, DH)
    gt = gt_ref[0, 0]

    def attend(nc):
        s = _dot_nt(qs, kc_ref[0, 0, :nc, :])
        j = lax.broadcasted_iota(jnp.int32, (1, nc), 1)
        t = s0 + lax.broadcasted_iota(jnp.int32, (TQ, 1), 0)
        mask = j * CMP_STRIDE + (CMP_BLOCK - 1) <= t
        mid = (j * CMP_STRIDE - s0).astype(F32) + 0.5 * (CMP_BLOCK - 1)
        has_key = t >= CMP_BLOCK - 1
        psum = jnp.zeros((TQ, nc), F32)
        ps = []
        for h in range(HG):
            rows = slice(h * TQ, (h + 1) * TQ)
            slope_row = jnp.concatenate([sl_ref[0, h * TQ:h * TQ + 1, :]] * (nc // LANES), axis=-1)
            sh = jnp.where(mask, s[rows] + slope_row * mid, NEG_INF)
            m = jnp.max(sh, axis=-1, keepdims=True)
            e = jnp.exp2(sh - m)
            l = jnp.sum(e, axis=-1, keepdims=True)
            inv = jnp.where(has_key, 1.0 / l, 0.0)
            p = e * inv
            psum = psum + p
            ps.append(p.astype(BF16))
        o_all = _dot(jnp.concatenate(ps, axis=0), vc_ref[0, 0, :nc, :])
        oc_ref[...] = jnp.concatenate(
            [o_all[h * TQ:(h + 1) * TQ] * gt[:, DH + 3 * h:DH + 3 * h + 1] for h in range(HG)], axis=-1)
        hi, md, lo = _split3(psum)
        ov = ov_ref[:nc, :]
        imp = _dot(hi, ov) + _dot(md, ov) + _dot(lo, ov)
        select(imp, nc * CMP_STRIDE // SLC_BLOCK)

    def select(imp, nb):
        imp_t = imp.T[:nb]
        blk = lax.broadcasted_iota(jnp.int32, (nb, TQ), 0)
        tq = s0 + lax.broadcasted_iota(jnp.int32, (nb, TQ), 1)
        cur = lax.shift_right_logical(tq, SLC_BLOCK.bit_length() - 1)
        forced = (blk == 0) | (blk == cur) | (blk == cur - 1)
        taken = -2.0
        score = jnp.where(forced, taken, jnp.where(blk * SLC_BLOCK <= tq, imp_t, -1.0))

        def pick(_, score):
            mx = jnp.max(score, axis=0, keepdims=True)
            first = jnp.min(jnp.where(score == mx, blk, nb), axis=0, keepdims=True)
            return jnp.where(blk == first, taken, score)

        score = lax.fori_loop(0, n_top - 3, pick, score, unroll=True)
        sel_t = jnp.where(score == taken, 1.0, 0.0)
        if nb < n_slc:
            sel_t = jnp.concatenate([sel_t, jnp.zeros((n_slc - nb, TQ), F32)], axis=0)
        sel = sel_t.T.astype(BF16)
        sel_ref[0, 0] = sel
        cnt = _dot(jnp.ones((8, TQ), BF16), sel)
        act_ref[0, 0, 0] = _dot((cnt > 0.0).astype(BF16), gm_ref[...])

    need = (s0 + TQ) // CMP_STRIDE
    widths = list(range(LANES, ncmp + 1, LANES))
    for idx, nc in enumerate(widths):
        lo_w = widths[idx - 1] if idx else 0
        pl.when((need > lo_w) & (need <= nc))(functools.partial(attend, nc))


def _nsa_cmp(q, kcmp, vcmp, gates, slope_tab, overlap, group_mat, B, T, n_top):
    G, HG, DH, TQ = NSA_GROUPS, NSA_HPG, NSA_DH, NSA_Q_TILE
    nq = T // TQ
    ncmp = kcmp.shape[2]
    n_slc = overlap.shape[1]
    assert n_top >= 3 and ncmp % LANES == 0 and HG < FORCE_SCORE
    return pl.pallas_call(
        functools.partial(_nsa_cmp_kernel, n_top=n_top),
        grid=(B, G, nq),
        in_specs=[pl.BlockSpec((1, 1, HG, TQ, DH), lambda b, g, i: (b, g, 0, i, 0)),
                  pl.BlockSpec((1, 1, ncmp, DH), lambda b, g, i: (b, g, 0, 0)),
                  pl.BlockSpec((1, 1, ncmp, DH), lambda b, g, i: (b, g, 0, 0)),
                  pl.BlockSpec((1, 1, TQ, LANES), lambda b, g, i: (b, g, i, 0)),
                  pl.BlockSpec((1, HG * TQ, LANES), lambda b, g, i: (g, 0, 0)),
                  pl.BlockSpec((ncmp, n_slc), lambda b, g, i: (0, 0)),
                  pl.BlockSpec((n_slc, LANES), lambda b, g, i: (0, 0))],
        out_specs=[pl.BlockSpec((TQ, HG * DH), lambda b, g, i: (b * nq + i, g)),
                   pl.BlockSpec((1, 1, TQ, n_slc), lambda b, g, i: (b, g, i, 0)),
                   pl.BlockSpec((1, 1, 1, 8, LANES), lambda b, g, i: (b, g, i, 0, 0))],
        out_shape=[jax.ShapeDtypeStruct((B * T, NSA_HEADS * DH), F32),
                   jax.ShapeDtypeStruct((B, G, T, n_slc), BF16),
                   jax.ShapeDtypeStruct((B, G, nq, 8, LANES), F32)],
        compiler_params=_cparams("parallel", "parallel", "arbitrary"),
        name="nsa_cmp_select",
    )(q, kcmp, vcmp, gates, slope_tab, overlap, group_mat)


def _nsa_main_kernel(act_ref, q_ref, sel_ref, oc_ref, gt_ref, sl_ref, ex_ref,
                     ksl_ref, vsl_ref, kw_ref, vw_ref, o_ref,
                     m_sc, acc_sc, s0_sc, s1_sc, pk0_sc, pk1_sc, list_sc, *, n_ktiles):
    HG, DH, TQ, KT = NSA_HPG, NSA_DH, NSA_Q_TILE, NSA_KEY_TILE
    s_slots, pk_slots = (s0_sc, s1_sc), (pk0_sc, pk1_sc)
    WK = WINDOW + TQ
    b, g, qi = pl.program_id(0), pl.program_id(1), pl.program_id(2)
    nq = pl.num_programs(2)
    s0 = qi * TQ
    qs = q_ref[0, 0].reshape(HG * TQ, DH)
    selb = sel_ref[0, 0]
    gt = gt_ref[0, 0]
    t_col = s0 + lax.broadcasted_iota(jnp.int32, (TQ, 1), 0)

    def biased(s, kpos, mask, h):
        width = kpos.shape[1]
        slope_row = jnp.concatenate([sl_ref[0, h * TQ:h * TQ + 1, :]] * (width // LANES), axis=-1)
        sh = s[h * TQ:(h + 1) * TQ] + slope_row * (kpos - s0).astype(F32)
        return jnp.where(mask, sh, NEG_INF)

    m_sc[...] = jnp.full_like(m_sc, NEG_INF)
    acc_sc[...] = jnp.zeros_like(acc_sc)
    act_base = ((b * NSA_GROUPS + g) * nq + qi) * n_ktiles
    list_sc[0] = 0

    def compact(i, n):
        hit = act_ref[act_base + i] > 0

        @pl.when(hit)
        def _():
            list_sc[n] = i
        return n + hit.astype(jnp.int32)

    n_act = jnp.maximum(lax.fori_loop(0, qi + 1, compact, 0), 1)

    def scores(j, slot):
        k0 = pl.multiple_of(list_sc[j] * KT, KT)
        s_slots[slot][...] = _dot_nt(qs, ksl_ref[0, 0, pl.ds(k0, KT), :])
        pk_slots[slot][...] = _dot(selb, ex_ref[:, pl.ds(k0, KT)])

    def update(j, slot):
        k0 = pl.multiple_of(list_sc[j] * KT, KT)
        kpos = k0 + lax.broadcasted_iota(jnp.int32, (1, KT), 1)
        mask = (pk_slots[slot][...] > 0.5) & (kpos <= t_col)
        ps, alphas = [], []
        for h in range(HG):
            rows = slice(h * TQ, (h + 1) * TQ)
            sh = biased(s_slots[slot], kpos, mask, h)
            m_old = m_sc[rows]
            m_new = jnp.maximum(m_old, jnp.max(sh, axis=-1, keepdims=True))
            ps.append(jnp.exp2((sh - jnp.concatenate([m_new] * (KT // LANES), axis=-1)).astype(BF16)))
            alphas.append(jnp.exp2(m_old - m_new))
            m_sc[rows] = m_new
        pv = _dot(jnp.concatenate(ps, axis=0), vsl_ref[0, 0, pl.ds(k0, KT), :])
        acc_sc[...] = jnp.concatenate(alphas, axis=0) * acc_sc[...] + pv

    def stage(j, slot):
        scores(j + 1, 1 - slot)
        update(j, slot)

    scores(0, 0)
    _two_slot_pipeline(n_act - 1, stage, update)

    start = pl.multiple_of(jnp.maximum(s0 - WINDOW, 0), TQ)
    kpos_w = start + lax.broadcasted_iota(jnp.int32, (1, WK), 1)
    mask_w = (kpos_w <= t_col) & (kpos_w > t_col - WINDOW)
    s_w = _dot_nt(qs, kw_ref[0, 0, pl.ds(start, WK), :])
    pw = []
    for h in range(HG):
        sh = biased(s_w, kpos_w, mask_w, h)
        pw.append(jnp.exp2((sh - jnp.max(sh, axis=-1, keepdims=True)).astype(BF16)))
    acc_w = _dot(jnp.concatenate(pw, axis=0), vw_ref[0, 0, pl.ds(start, WK), :])

    outs = []
    for h in range(HG):
        rows = slice(h * TQ, (h + 1) * TQ)
        a_s, a_w = acc_sc[rows], acc_w[rows]
        r_s, r_w = gt / a_s, gt / a_w
        outs.append(oc_ref[:, h * DH:(h + 1) * DH]
                    + r_s[:, DH + 3 * h + 1:DH + 3 * h + 2] * a_s[:, :DH]
                    + r_w[:, DH + 3 * h + 2:DH + 3 * h + 3] * a_w[:, :DH])
    o_ref[...] = jnp.concatenate(outs, axis=-1)


def _nsa_main(act, q, sel, oc, gates, slope_tab, expand, ksl, vsl, kw, vw, B, T):
    G, HG, DH, TQ = NSA_GROUPS, NSA_HPG, NSA_DH, NSA_Q_TILE
    nq = T // TQ
    n_slc = sel.shape[3]
    n_ktiles = T // NSA_KEY_TILE
    kv = lambda last: pl.BlockSpec((1, 1, T, last), lambda b, g, i, a: (b, g, 0, 0))
    grid_spec = pltpu.PrefetchScalarGridSpec(
        num_scalar_prefetch=1,
        grid=(B, G, nq),
        in_specs=[pl.BlockSpec((1, 1, HG, TQ, DH), lambda b, g, i, a: (b, g, 0, i, 0)),
                  pl.BlockSpec((1, 1, TQ, n_slc), lambda b, g, i, a: (b, g, i, 0)),
                  pl.BlockSpec((TQ, HG * DH), lambda b, g, i, a: (b * nq + i, g)),
                  pl.BlockSpec((1, 1, TQ, LANES), lambda b, g, i, a: (b, g, i, 0)),
                  pl.BlockSpec((1, HG * TQ, LANES), lambda b, g, i, a: (g, 0, 0)),
                  pl.BlockSpec((n_slc, T), lambda b, g, i, a: (0, 0)),
                  kv(DH), kv(LANES), kv(DH), kv(LANES)],
        out_specs=pl.BlockSpec((TQ, HG * DH), lambda b, g, i, a: (b * nq + i, g)),
        scratch_shapes=[pltpu.VMEM((HG * TQ, LANES), F32), pltpu.VMEM((HG * TQ, LANES), F32),
                        pltpu.VMEM((HG * TQ, NSA_KEY_TILE), F32), pltpu.VMEM((HG * TQ, NSA_KEY_TILE), F32),
                        pltpu.VMEM((TQ, NSA_KEY_TILE), F32), pltpu.VMEM((TQ, NSA_KEY_TILE), F32),
                        pltpu.SMEM((n_ktiles + 1,), jnp.int32)],
    )
    return pl.pallas_call(
        functools.partial(_nsa_main_kernel, n_ktiles=n_ktiles),
        grid_spec=grid_spec,
        out_shape=jax.ShapeDtypeStruct((B * T, NSA_HEADS * DH), F32),
        compiler_params=_cparams("parallel", "parallel", "arbitrary"),
        name="nsa_select_window",
    )(act, q, sel, oc, gates, slope_tab, expand, ksl, vsl, kw, vw)


def _nsa_mixer(h_proj, B, T, q_norm, kcmp_norm, kslc_norm, kwin_norm, pos_k, pos_v,
               k_w1, k_b1, k_w2, v_w1, v_b1, v_w2):
    G, HG, DH, TQ = NSA_GROUPS, NSA_HPG, NSA_DH, NSA_Q_TILE
    q, kc, vc, ksl, vsl, kw, vw, gates = _nsa_prep(h_proj, q_norm, kslc_norm, kwin_norm, B, T)

    nrow = T // CMP_STRIDE
    half = CMP_STRIDE * DH
    kcmp, vcmp = _compress(kc.reshape(B, G, nrow, half), vc.reshape(B, G, nrow, half),
                           pos_k, pos_v, k_w1, k_b1, k_w2, v_w1, v_b1, v_w2, kcmp_norm)

    n_slc = T // SLC_BLOCK
    n_top = min(SLC_TOP, n_slc)
    n_slc_pad = max(n_slc, LANES)
    slopes = 2.0 ** (-8.0 * jnp.arange(1, NSA_HEADS + 1, dtype=F32) / NSA_HEADS) * LOG2E
    slope_tab = jnp.broadcast_to(slopes.reshape(G, HG, 1, 1), (G, HG, TQ, LANES)).reshape(G, HG * TQ, LANES)
    cj = jnp.arange(nrow)[:, None] * CMP_STRIDE
    si = jnp.arange(n_slc_pad)[None, :] * SLC_BLOCK
    overlap = ((cj <= si + SLC_BLOCK - 1) & (cj + CMP_BLOCK - 1 >= si)
               & (jnp.arange(nrow)[:, None] < nrow - 1)).astype(BF16)
    blocks_per_tile = NSA_KEY_TILE // SLC_BLOCK
    group_mat = (jnp.arange(n_slc_pad)[:, None] // blocks_per_tile
                 == jnp.arange(LANES)[None, :]).astype(BF16)
    expand = (jnp.arange(n_slc_pad)[:, None] == jnp.arange(T)[None, :] // SLC_BLOCK).astype(BF16)

    oc, sel, act = _nsa_cmp(q, kcmp, vcmp, gates, slope_tab, overlap, group_mat, B, T, n_top)
    n_ktiles = T // NSA_KEY_TILE
    act_i = act[:, :, :, 0, :n_ktiles].astype(jnp.int32).reshape(-1)
    return _nsa_main(act_i, q, sel, oc, gates, slope_tab, expand, ksl, vsl, kw, vw, B, T)


def _rope_layout(w):
    half = QK_ROPE // 2
    z = jnp.zeros(w.shape[:-1] + (LANES // 2 - half,), w.dtype)
    return jnp.concatenate([w[..., :half], z, w[..., half:], z], axis=-1)


def _head_layout(w):
    w = w.reshape(w.shape[:-1] + (MLA_HEADS, MLA_QK))
    w = jnp.concatenate([w[..., :QK_NOPE], _rope_layout(w[..., QK_NOPE:])], axis=-1)
    return w.reshape(w.shape[:-2] + (MLA_HEADS * MLA_QK_PAD,))


def _norm_rope(nope, rope, ss, gn_ref, gr_ref, cos_ref, sin_ref, scale):
    inv = lax.rsqrt(ss * (1.0 / MLA_QK) + NORM_EPS)
    a = nope * inv * gn_ref[...]
    r = rope * inv * gr_ref[...]
    if scale != 1.0:
        a, r = a * scale, r * scale
    r = r * cos_ref[...] + pltpu.roll(r, LANES // 2, 1) * sin_ref[...]
    return jnp.concatenate([a, r], axis=-1).astype(BF16)


def _mla_q_kernel(q_ref, gn_ref, gr_ref, cos_ref, sin_ref, o_ref):
    for h in range(MLA_HEADS):
        nope = q_ref[:, h * MLA_QK_PAD:h * MLA_QK_PAD + QK_NOPE]
        rope = q_ref[:, h * MLA_QK_PAD + QK_NOPE:(h + 1) * MLA_QK_PAD]
        ss = jnp.sum(nope * nope + rope * rope, axis=-1, keepdims=True)
        o_ref[0, h] = _norm_rope(nope, rope, ss, gn_ref, gr_ref, cos_ref, sin_ref, MLA_QK ** -0.5 * LOG2E)


def _mla_kv_kernel(kv_ref, kpe_ref, gn_ref, gr_ref, cos_ref, sin_ref, k_ref, v_ref):
    rope = kpe_ref[...]
    for h in range(MLA_HEADS):
        base = h * (QK_NOPE + MLA_V)
        nope = kv_ref[:, base:base + QK_NOPE]
        ss = jnp.sum(nope * nope + rope * rope, axis=-1, keepdims=True)
        k_ref[0, h] = _norm_rope(nope, rope, ss, gn_ref, gr_ref, cos_ref, sin_ref, 1.0)
        v = kv_ref[:, base + QK_NOPE:base + QK_NOPE + MLA_V]
        v_ref[0, h] = jnp.concatenate([v, jnp.ones_like(v)], axis=-1).astype(BF16)


def _mla_prep_q(q, gain, cos_l, sin_l, B, T, *, tm=256):
    nt = T // tm
    row = lambda: pl.BlockSpec((1, LANES), lambda i: (0, 0))
    tab = lambda: pl.BlockSpec((tm, LANES), lambda i: (i % nt, 0))
    return pl.pallas_call(
        _mla_q_kernel,
        grid=(B * nt,),
        in_specs=[pl.BlockSpec((tm, MLA_HEADS * MLA_QK_PAD), lambda i: (i, 0)), row(), row(), tab(), tab()],
        out_specs=pl.BlockSpec((1, MLA_HEADS, tm, MLA_QK_PAD), lambda i: (i // nt, 0, i % nt, 0)),
        out_shape=jax.ShapeDtypeStruct((B, MLA_HEADS, T, MLA_QK_PAD), BF16),
        compiler_params=_cparams("parallel"),
        name="mla_prep_q",
    )(q, gain[:QK_NOPE].reshape(1, LANES), _rope_layout(gain[QK_NOPE:]).reshape(1, LANES), cos_l, sin_l)


def _mla_prep_kv(kv, kv_a, gain, cos_l, sin_l, B, T, *, tm=256):
    nt = T // tm
    pe_block = KV_LORA // LANES
    row = lambda: pl.BlockSpec((1, LANES), lambda i: (0, 0))
    tab = lambda: pl.BlockSpec((tm, LANES), lambda i: (i % nt, 0))
    return pl.pallas_call(
        _mla_kv_kernel,
        grid=(B * nt,),
        in_specs=[pl.BlockSpec((tm, MLA_HEADS * (QK_NOPE + MLA_V)), lambda i: (i, 0)),
                  pl.BlockSpec((tm, LANES), lambda i: (i, pe_block)), row(), row(), tab(), tab()],
        out_specs=[pl.BlockSpec((1, MLA_HEADS, tm, MLA_QK_PAD), lambda i: (i // nt, 0, i % nt, 0)),
                   pl.BlockSpec((1, MLA_HEADS, tm, 2 * MLA_V), lambda i: (i // nt, 0, i % nt, 0))],
        out_shape=[jax.ShapeDtypeStruct((B, MLA_HEADS, T, MLA_QK_PAD), BF16),
                   jax.ShapeDtypeStruct((B, MLA_HEADS, T, 2 * MLA_V), BF16)],
        compiler_params=_cparams("parallel"),
        name="mla_prep_kv",
    )(kv, kv_a, gain[:QK_NOPE].reshape(1, LANES), _rope_layout(gain[QK_NOPE:]).reshape(1, LANES), cos_l, sin_l)


def _two_slot_pipeline(last, stage, finish):
    def pair(j, carry):
        stage(2 * j, 0)
        stage(2 * j + 1, 1)
        return carry

    lax.fori_loop(0, last // 2, pair, 0)

    @pl.when(last % 2 == 1)
    def _():
        stage(last - 1, 0)
        finish(last, 1)

    @pl.when(last % 2 == 0)
    def _():
        finish(last, 0)


def _mla_attn_kernel(q_ref, k_ref, v_ref, o_ref, m_sc, acc_sc, s0_sc, s1_sc):
    TQ = MLA_TILE
    qi = pl.program_id(2)
    q = q_ref[0, 0]
    s_slots = (s0_sc, s1_sc)
    m_sc[...] = jnp.full_like(m_sc, NEG_INF)
    acc_sc[...] = jnp.zeros_like(acc_sc)

    def scores(i, slot):
        k0 = pl.multiple_of(i * TQ, TQ)
        s_slots[slot][...] = _dot_nt(q, k_ref[0, 0, pl.ds(k0, TQ), :])

    def update(i, slot, masked):
        k0 = pl.multiple_of(i * TQ, TQ)

        def load():
            s = s_slots[slot][...]
            if masked:
                qq = lax.broadcasted_iota(jnp.int32, (TQ, TQ), 0)
                kk = lax.broadcasted_iota(jnp.int32, (TQ, TQ), 1)
                s = jnp.where(kk <= qq, s, NEG_INF)
            return s

        m_old = m_sc[...]
        m_new = jnp.maximum(m_old, jnp.max(load(), axis=-1, keepdims=True))
        p = jnp.exp2((load() - jnp.concatenate([m_new] * (TQ // LANES), axis=-1)).astype(BF16))
        alpha = jnp.exp2(m_old - m_new)
        acc_sc[...] = (jnp.concatenate([alpha] * (acc_sc.shape[1] // LANES), axis=-1) * acc_sc[...]
                       + _dot(p, v_ref[0, 0, pl.ds(k0, TQ), :]))
        m_sc[...] = m_new

    def stage(i, slot):
        scores(i + 1, 1 - slot)
        update(i, slot, False)

    scores(0, 0)
    _two_slot_pipeline(qi, stage, lambda i, slot: update(i, slot, True))
    o_ref[...] = acc_sc[:, :MLA_V] / acc_sc[:, MLA_V:]


def _mla_attn(q, k, v, B, T):
    H, TQ = MLA_HEADS, MLA_TILE
    nq = T // TQ
    return pl.pallas_call(
        _mla_attn_kernel,
        grid=(B, H, nq),
        in_specs=[pl.BlockSpec((1, 1, TQ, MLA_QK_PAD), lambda b, h, i: (b, h, i, 0)),
                  pl.BlockSpec((1, 1, T, MLA_QK_PAD), lambda b, h, i: (b, h, 0, 0)),
                  pl.BlockSpec((1, 1, T, 2 * MLA_V), lambda b, h, i: (b, h, 0, 0))],
        out_specs=pl.BlockSpec((TQ, MLA_V), lambda b, h, i: (b * nq + i, h)),
        out_shape=jax.ShapeDtypeStruct((B * T, H * MLA_V), F32),
        scratch_shapes=[pltpu.VMEM((TQ, LANES), F32), pltpu.VMEM((TQ, 2 * MLA_V), F32),
                        pltpu.VMEM((TQ, TQ), F32), pltpu.VMEM((TQ, TQ), F32)],
        compiler_params=_cparams("parallel", "parallel", "arbitrary"),
        name="mla_flash_attn",
    )(q, k, v)


def _pad_cols(w, mult=LANES):
    pad = -w.shape[1] % mult
    return jnp.pad(w, ((0, 0), (0, pad))) if pad else w


def kernel(x, a_attn_norm, a_w_in, a_q_norm, a_kcmp_norm, a_kslc_norm, a_kwin_norm, a_cmp_pos_k, a_cmp_pos_v, a_cmp_k_w1, a_cmp_k_b1, a_cmp_k_w2, a_cmp_v_w1, a_cmp_v_b1, a_cmp_v_w2, a_w_out, kv_norm, kv_w_a, kv_c_norm, kv_w_b, kv_k_norm, b_attn_norm, b_w_q_a, b_q_a_norm, b_w_q_b, b_q_norm, b_w_out, ffn_norm, ffn_w_gate_up, ffn_w_down):
    B, T, D = x.shape
    n_a = a_w_in.shape[0]
    n_b = b_w_q_a.shape[0]
    xs = x.reshape(B * T, D)

    inv = ROPE_THETA ** (-jnp.arange(0, QK_ROPE, 2, dtype=F32) / QK_ROPE)
    ang = jnp.arange(T, dtype=F32)[:, None] * inv[None, :]
    cos, sin = jnp.cos(ang), jnp.sin(ang)
    cos2 = _rope_layout(jnp.concatenate([cos, cos], axis=-1))
    sin2 = _rope_layout(jnp.concatenate([-sin, sin], axis=-1))

    k_shared = v_shared = None
    for layer in range(n_a + n_b):
        if layer < n_a:
            i = layer
            proj = _linear(xs, _pad_cols(a_w_in[i]).astype(BF16), gain=a_attn_norm[i], name="nsa_in_proj")
            o = _nsa_mixer(proj, B, T, a_q_norm[i], a_kcmp_norm[i], a_kslc_norm[i], a_kwin_norm[i],
                           a_cmp_pos_k[i], a_cmp_pos_v[i], a_cmp_k_w1[i], a_cmp_k_b1[i], a_cmp_k_w2[i],
                           a_cmp_v_w1[i], a_cmp_v_b1[i], a_cmp_v_w2[i])
            xs = _linear(o, a_w_out[i].astype(BF16), residual=xs, name="nsa_out_proj")
        else:
            j = layer - n_a
            qa = _linear(xs, b_w_q_a[j].astype(BF16), gain=b_attn_norm[j], name="mla_q_a_proj")
            qb = _linear(qa, _head_layout(b_w_q_b[j]).astype(BF16), gain=b_q_a_norm[j], name="mla_q_b_proj")
            q = _mla_prep_q(qb, b_q_norm[j], cos2, sin2, B, T)
            o = _mla_attn(q, k_shared, v_shared, B, T)
            xs = _linear(o, b_w_out[j].astype(BF16), residual=xs, name="mla_out_proj")
        xs = _ffn(xs, ffn_norm[layer], ffn_w_gate_up[layer].astype(BF16), ffn_w_down[layer].astype(BF16))
        if layer == n_a - 1:
            w_kv_a = jnp.concatenate([kv_w_a[:, :KV_LORA], _rope_layout(kv_w_a[:, KV_LORA:])], axis=-1)
            kv_a = _linear(xs, w_kv_a.astype(BF16), gain=kv_norm, name="mla_kv_a_proj")
            kv = _linear(kv_a, kv_w_b.astype(BF16), gain=kv_c_norm, k_cols=KV_LORA, name="mla_kv_b_proj")
            k_shared, v_shared = _mla_prep_kv(kv, kv_a, kv_k_norm, cos2, sin2, B, T)
    return xs.reshape(B, T, D)
```

```python
import functools

import jax
import jax.numpy as jnp
from jax import lax
from jax.experimental import pallas as pl
from jax.experimental.pallas import tpu as pltpu

F32 = jnp.float32
BF16 = jnp.bfloat16

NORM_EPS = 1e-6
NEG_INF = -1e30
LANES = 128

NSA_HEADS = 16
NSA_GROUPS = 4
NSA_HPG = NSA_HEADS // NSA_GROUPS
NSA_DH = 64
CMP_BLOCK = 32
CMP_STRIDE = 16
CMP_HIDDEN = 256
SLC_BLOCK = 64
SLC_TOP = 16
WINDOW = 512
FORCE_SCORE = 1e4
NSA_Q_TILE = 256
NSA_KEY_TILE = 256
LOG2E = 1.4426950408889634

MLA_HEADS = 8
QK_NOPE = 128
QK_ROPE = 64
MLA_QK = QK_NOPE + QK_ROPE
MLA_QK_PAD = QK_NOPE + LANES
MLA_V = 128
Q_LORA = 384
KV_LORA = 256
ROPE_THETA = 10000.0
MLA_TILE = 512

VMEM_LIMIT = 56 * 1024 * 1024


def _cparams(*sem):
    return pltpu.CompilerParams(dimension_semantics=sem, vmem_limit_bytes=VMEM_LIMIT)


def _rms(x, g):
    return x * lax.rsqrt(jnp.mean(x * x, axis=-1, keepdims=True) + NORM_EPS) * g


def _dot(a, b):
    return jnp.dot(a, b, preferred_element_type=F32)


def _dot_nt(a, b):
    return lax.dot_general(a, b, (((1,), (1,)), ((), ())), preferred_element_type=F32)


def _out_ffn_kernel(x_ref, a_ref, wo_ref, g_ref, wgu_ref, wd_ref, o_ref):
    hid = wd_ref.shape[0]
    x = x_ref[...] + _dot(a_ref[...], wo_ref[...])
    h = _rms(x, g_ref[...]).astype(BF16)
    gate = _dot(h, wgu_ref[:, :hid])
    up = _dot(h, wgu_ref[:, hid:])
    a = (gate * jax.nn.sigmoid(gate) * up).astype(BF16)
    o_ref[...] = x + _dot(a, wd_ref[...])


def _out_ffn(x, mixed, w_out, gain, w_gate_up, w_down, *, tm=512):
    n, d = x.shape
    hid = w_down.shape[0]
    dm = mixed.shape[1]
    assert n % tm == 0 and hid % LANES == 0
    resident = lambda shape: pl.BlockSpec(shape, lambda i: (0, 0), pipeline_mode=pl.Buffered(1))
    return pl.pallas_call(
        _out_ffn_kernel,
        grid=(n // tm,),
        in_specs=[pl.BlockSpec((tm, d), lambda i: (i, 0)),
                  pl.BlockSpec((tm, dm), lambda i: (i, 0)),
                  resident((dm, d)),
                  pl.BlockSpec((1, d), lambda i: (0, 0)),
                  resident((d, 2 * hid)), resident((hid, d))],
        out_specs=pl.BlockSpec((tm, d), lambda i: (i, 0)),
        out_shape=jax.ShapeDtypeStruct((n, d), F32),
        compiler_params=_cparams("parallel"),
        name="out_proj_swiglu_ffn",
    )(x, mixed, w_out, gain.reshape(1, d), w_gate_up, w_down)


def _split2(x):
    hi = x.astype(BF16)
    return hi, (x - hi.astype(F32)).astype(BF16)


def _nsa_in_kernel(x_ref, an_ref, w_ref, gain_ref, gsum_ref, gexp_ref,
                   q_ref, kc_ref, vc_ref, ksl_ref, vsl_ref, kw_ref, vw_ref, gt_ref):
    G, HG, DH = NSA_GROUPS, NSA_HPG, NSA_DH
    qw, kvw = NSA_HEADS * DH, G * DH
    tm = x_ref.shape[0]
    p = _dot(_rms(x_ref[...], an_ref[...]).astype(BF16), w_ref[...])
    ones_col = jnp.ones((tm, LANES - DH), F32)
    n_gate = HG * 3
    gate_pad = jnp.zeros((tm, LANES - DH - n_gate), F32)

    def piece(base, g):
        return p[:, base + g * DH: base + (g + 1) * DH]

    xn = jnp.concatenate([p[:, :qw], p[:, qw + 2 * kvw:qw + 3 * kvw], p[:, qw + 4 * kvw:qw + 5 * kvw]], axis=-1)
    hi, lo = _split2(xn * xn)
    ss = _dot(hi, gsum_ref[...]) + _dot(lo, gsum_ref[...])
    hi, lo = _split2(lax.rsqrt(ss * (1.0 / DH) + NORM_EPS))
    xn = xn * (_dot(hi, gexp_ref[...]) + _dot(lo, gexp_ref[...])) * gain_ref[...]

    for g in range(G):
        for h in range(HG):
            q_ref[0, g, h] = xn[:, (g * HG + h) * DH:(g * HG + h + 1) * DH].astype(BF16)
        kc_ref[0, g] = piece(qw, g).astype(BF16)
        vc_ref[0, g] = piece(qw + kvw, g).astype(BF16)
        ksl_ref[0, g] = xn[:, qw + g * DH:qw + (g + 1) * DH].astype(BF16)
        vsl_ref[0, g] = jnp.concatenate([piece(qw + 3 * kvw, g), ones_col], axis=-1).astype(BF16)
        kw_ref[0, g] = xn[:, qw + kvw + g * DH:qw + kvw + (g + 1) * DH].astype(BF16)
        vw_ref[0, g] = jnp.concatenate([piece(qw + 5 * kvw, g), ones_col], axis=-1).astype(BF16)
        gb = qw + 6 * kvw + g * n_gate
        gt_ref[0, g] = jnp.concatenate([jnp.zeros((tm, DH), F32), jax.nn.sigmoid(p[:, gb:gb + n_gate]),
                                        gate_pad], axis=-1)


def _nsa_in(x, attn_norm, w_in, q_norm, kslc_norm, kwin_norm, B, T, *, tm=512):
    G, HG, DH = NSA_GROUPS, NSA_HPG, NSA_DH
    nt = T // tm
    d, width = w_in.shape
    n_norm = (NSA_HEADS + 2 * G) * DH
    gain = jnp.concatenate([jnp.tile(q_norm, NSA_HEADS) * (DH ** -0.5 * LOG2E),
                            jnp.tile(kslc_norm, G), jnp.tile(kwin_norm, G)]).reshape(1, n_norm)
    gsum = (jnp.arange(n_norm)[:, None] // DH == jnp.arange(LANES)[None, :]).astype(BF16)
    full = lambda shape: pl.BlockSpec(shape, lambda i: (0, 0))
    per_group = lambda last: pl.BlockSpec((1, G, tm, last), lambda i: (i // nt, 0, i % nt, 0))
    shp = lambda last, dt: jax.ShapeDtypeStruct((B, G, T, last), dt)
    return pl.pallas_call(
        _nsa_in_kernel,
        grid=(B * nt,),
        in_specs=[pl.BlockSpec((tm, d), lambda i: (i, 0)), full((1, d)), full((d, width)),
                  full((1, n_norm)), full((n_norm, LANES)), full((LANES, n_norm))],
        out_specs=[pl.BlockSpec((1, G, HG, tm, DH), lambda i: (i // nt, 0, 0, i % nt, 0)),
                   per_group(DH), per_group(DH), per_group(DH), per_group(LANES),
                   per_group(DH), per_group(LANES), per_group(LANES)],
        out_shape=[jax.ShapeDtypeStruct((B, G, HG, T, DH), BF16),
                   shp(DH, BF16), shp(DH, BF16), shp(DH, BF16), shp(LANES, BF16),
                   shp(DH, BF16), shp(LANES, BF16), shp(LANES, F32)],
        compiler_params=_cparams("parallel"),
        name="nsa_in_proj_split",
    )(x, attn_norm.reshape(1, d), w_in, gain, gsum, gsum.T)


def _compress_kernel(rk_ref, rv_ref, pek_ref, pev_ref, kw1_ref, kb1_ref, kw2_ref,
                     vw1_ref, vb1_ref, vw2_ref, kn_ref, kcmp_ref, vcmp_ref):
    half = CMP_STRIDE * NSA_DH
    nrow = rk_ref.shape[2]

    def mlp(r_ref, pe_ref, w1_ref, b1_ref, w2_ref):
        r = r_ref[0, 0].astype(BF16)
        ya = _dot(r, w1_ref[:half, :])
        yb = _dot(r, w1_ref[half:, :])
        pe = jnp.broadcast_to(pe_ref[...], (8, 2 * half)).astype(BF16)
        c = _dot(pe, w1_ref[...])[0:1] + b1_ref[...]
        hid = ya + pltpu.roll(yb, nrow - 1, 0) + c
        return _dot(jax.nn.gelu(hid).astype(BF16), w2_ref[...])

    kcmp_ref[0, 0] = _rms(mlp(rk_ref, pek_ref, kw1_ref, kb1_ref, kw2_ref), kn_ref[...]).astype(BF16)
    vcmp_ref[0, 0] = mlp(rv_ref, pev_ref, vw1_ref, vb1_ref, vw2_ref).astype(BF16)


def _compress(rk, rv, pe_k, pe_v, k_w1, k_b1, k_w2, v_w1, v_b1, v_w2, kcmp_norm):
    B, G, nrow, half = rk.shape
    DH, HID = NSA_DH, CMP_HIDDEN
    full = lambda shape: pl.BlockSpec(shape, lambda b, g: (0,) * len(shape))
    r_spec = pl.BlockSpec((1, 1, nrow, half), lambda b, g: (b, g, 0, 0))
    o_spec = pl.BlockSpec((1, 1, nrow, DH), lambda b, g: (b, g, 0, 0))
    return pl.pallas_call(
        _compress_kernel,
        grid=(B, G),
        in_specs=[r_spec, r_spec, full((1, 2 * half)), full((1, 2 * half)),
                  full((2 * half, HID)), full((1, HID)), full((HID, DH)),
                  full((2 * half, HID)), full((1, HID)), full((HID, DH)), full((1, DH))],
        out_specs=[o_spec, o_spec],
        out_shape=[jax.ShapeDtypeStruct((B, G, nrow, DH), BF16)] * 2,
        compiler_params=_cparams("parallel", "parallel"),
        name="nsa_compress",
    )(rk, rv, pe_k.reshape(1, -1), pe_v.reshape(1, -1),
      k_w1.astype(BF16), k_b1.reshape(1, HID), k_w2.astype(BF16),
      v_w1.astype(BF16), v_b1.reshape(1, HID), v_w2.astype(BF16), kcmp_norm.reshape(1, DH))


def _split3(x):
    hi = x.astype(BF16)
    r1 = x - hi.astype(F32)
    mid = r1.astype(BF16)
    lo = (r1 - mid.astype(F32)).astype(BF16)
    return hi, mid, lo


def _nsa_cmp_kernel(q_ref, kc_ref, vc_ref, gt_ref, sl_ref, ov_ref, gm_ref,
                    oc_ref, sel_ref, act_ref, *, n_top):
    HG, DH, TQ = NSA_HPG, NSA_DH, NSA_Q_TILE
    s0 = pl.program_id(2) * TQ
    ncmp = kc_ref.shape[2]
    n_slc = ov_ref.shape[1]
    qs = q_ref[0, 0].reshape(HG * TQ, DH)
    gt = gt_ref[0, 0]

    def attend(nc):
        s = _dot_nt(qs, kc_ref[0, 0, :nc, :])
        j = lax.broadcasted_iota(jnp.int32, (1, nc), 1)
        t = s0 + lax.broadcasted_iota(jnp.int32, (TQ, 1), 0)
        mask = j * CMP_STRIDE + (CMP_BLOCK - 1) <= t
        mid = (j * CMP_STRIDE - s0).astype(F32) + 0.5 * (CMP_BLOCK - 1)
        has_key = t >= CMP_BLOCK - 1
        psum = jnp.zeros((TQ, nc), F32)
        ps = []
        for h in range(HG):
            rows = slice(h * TQ, (h + 1) * TQ)
            slope_row = jnp.concatenate([sl_ref[0, h * TQ:h * TQ + 1, :]] * (nc // LANES), axis=-1)
            sh = jnp.where(mask, s[rows] + slope_row * mid, NEG_INF)
            m = jnp.max(sh, axis=-1, keepdims=True)
            e = jnp.exp2(sh - m)
            l = jnp.sum(e, axis=-1, keepdims=True)
            inv = jnp.where(has_key, 1.0 / l, 0.0)
            p = e * inv
            psum = psum + p
            ps.append(p.astype(BF16))
        o_all = _dot(jnp.concatenate(ps, axis=0), vc_ref[0, 0, :nc, :])
        oc_ref[...] = jnp.concatenate(
            [o_all[h * TQ:(h + 1) * TQ] * gt[:, DH + 3 * h:DH + 3 * h + 1] for h in range(HG)],
            axis=-1).astype(oc_ref.dtype)
        hi, md, lo = _split3(psum)
        ov = ov_ref[:nc, :]
        imp = _dot(hi, ov) + _dot(md, ov) + _dot(lo, ov)
        select(imp, nc * CMP_STRIDE // SLC_BLOCK)

    def select(imp, nb):
        imp_t = imp.T[:nb]
        blk = lax.broadcasted_iota(jnp.int32, (nb, TQ), 0)
        tq = s0 + lax.broadcasted_iota(jnp.int32, (nb, TQ), 1)
        cur = lax.shift_right_logical(tq, SLC_BLOCK.bit_length() - 1)
        forced = (blk == 0) | (blk == cur) | (blk == cur - 1)
        taken = -2.0
        score = jnp.where(forced, taken, jnp.where(blk * SLC_BLOCK <= tq, imp_t, -1.0))

        def pick(_, score):
            mx = jnp.max(score, axis=0, keepdims=True)
            first = jnp.min(jnp.where(score == mx, blk, nb), axis=0, keepdims=True)
            return jnp.where(blk == first, taken, score)

        score = lax.fori_loop(0, n_top - 3, pick, score, unroll=True)
        sel_t = jnp.where(score == taken, 1.0, 0.0)
        if nb < n_slc:
            sel_t = jnp.concatenate([sel_t, jnp.zeros((n_slc - nb, TQ), F32)], axis=0)
        sel = sel_t.T.astype(BF16)
        sel_ref[0, 0] = sel
        cnt = _dot(jnp.ones((8, TQ), BF16), sel)
        act_ref[0, 0, 0] = _dot((cnt > 0.0).astype(BF16), gm_ref[...])

    need = (s0 + TQ) // CMP_STRIDE
    widths = list(range(LANES, ncmp + 1, LANES))
    for idx, nc in enumerate(widths):
        lo_w = widths[idx - 1] if idx else 0
        pl.when((need > lo_w) & (need <= nc))(functools.partial(attend, nc))


def _nsa_cmp(q, kcmp, vcmp, gates, slope_tab, overlap, group_mat, B, T, n_top):
    G, HG, DH, TQ = NSA_GROUPS, NSA_HPG, NSA_DH, NSA_Q_TILE
    nq = T // TQ
    ncmp = kcmp.shape[2]
    n_slc = overlap.shape[1]
    assert n_top >= 3 and ncmp % LANES == 0 and HG < FORCE_SCORE
    return pl.pallas_call(
        functools.partial(_nsa_cmp_kernel, n_top=n_top),
        grid=(B, G, nq),
        in_specs=[pl.BlockSpec((1, 1, HG, TQ, DH), lambda b, g, i: (b, g, 0, i, 0)),
                  pl.BlockSpec((1, 1, ncmp, DH), lambda b, g, i: (b, g, 0, 0)),
                  pl.BlockSpec((1, 1, ncmp, DH), lambda b, g, i: (b, g, 0, 0)),
                  pl.BlockSpec((1, 1, TQ, LANES), lambda b, g, i: (b, g, i, 0)),
                  pl.BlockSpec((1, HG * TQ, LANES), lambda b, g, i: (g, 0, 0)),
                  pl.BlockSpec((ncmp, n_slc), lambda b, g, i: (0, 0)),
                  pl.BlockSpec((n_slc, LANES), lambda b, g, i: (0, 0))],
        out_specs=[pl.BlockSpec((TQ, HG * DH), lambda b, g, i: (b * nq + i, g)),
                   pl.BlockSpec((1, 1, TQ, n_slc), lambda b, g, i: (b, g, i, 0)),
                   pl.BlockSpec((1, 1, 1, 8, LANES), lambda b, g, i: (b, g, i, 0, 0))],
        out_shape=[jax.ShapeDtypeStruct((B * T, NSA_HEADS * DH), BF16),
                   jax.ShapeDtypeStruct((B, G, T, n_slc), BF16),
                   jax.ShapeDtypeStruct((B, G, nq, 8, LANES), F32)],
        compiler_params=_cparams("parallel", "parallel", "arbitrary"),
        name="nsa_cmp_select",
    )(q, kcmp, vcmp, gates, slope_tab, overlap, group_mat)


def _nsa_main_kernel(act_ref, q_ref, sel_ref, oc_ref, gt_ref, sl_ref, ex_ref,
                     ksl_ref, vsl_ref, kw_ref, vw_ref, o_ref,
                     m_sc, acc_sc, s0_sc, s1_sc, pk0_sc, pk1_sc, list_sc, *, n_ktiles):
    HG, DH, TQ, KT = NSA_HPG, NSA_DH, NSA_Q_TILE, NSA_KEY_TILE
    s_slots, pk_slots = (s0_sc, s1_sc), (pk0_sc, pk1_sc)
    WK = WINDOW + TQ
    b, g, qi = pl.program_id(0), pl.program_id(1), pl.program_id(2)
    nq = pl.num_programs(2)
    s0 = qi * TQ
    qs = q_ref[0, 0].reshape(HG * TQ, DH)
    selb = sel_ref[0, 0]
    gt = gt_ref[0, 0]
    t_col = s0 + lax.broadcasted_iota(jnp.int32, (TQ, 1), 0)

    def biased(s, kpos, mask, h):
        width = kpos.shape[1]
        slope_row = jnp.concatenate([sl_ref[0, h * TQ:h * TQ + 1, :]] * (width // LANES), axis=-1)
        sh = s[h * TQ:(h + 1) * TQ] + slope_row * (kpos - s0).astype(F32)
        return jnp.where(mask, sh, NEG_INF)

    m_sc[...] = jnp.full_like(m_sc, NEG_INF)
    acc_sc[...] = jnp.zeros_like(acc_sc)
    act_base = ((b * NSA_GROUPS + g) * nq + qi) * n_ktiles
    list_sc[0] = 0

    def compact(i, n):
        hit = act_ref[act_base + i] > 0

        @pl.when(hit)
        def _():
            list_sc[n] = i
        return n + hit.astype(jnp.int32)

    n_act = jnp.maximum(lax.fori_loop(0, qi + 1, compact, 0), 1)

    def scores(j, slot):
        k0 = pl.multiple_of(list_sc[j] * KT, KT)
        s_slots[slot][...] = _dot_nt(qs, ksl_ref[0, 0, pl.ds(k0, KT), :])
        pk_slots[slot][...] = _dot(selb, ex_ref[:, pl.ds(k0, KT)])

    def update(j, slot):
        k0 = pl.multiple_of(list_sc[j] * KT, KT)
        kpos = k0 + lax.broadcasted_iota(jnp.int32, (1, KT), 1)
        mask = (pk_slots[slot][...] > 0.5) & (kpos <= t_col)
        ps, alphas = [], []
        for h in range(HG):
            rows = slice(h * TQ, (h + 1) * TQ)
            sh = biased(s_slots[slot], kpos, mask, h)
            m_old = m_sc[rows]
            m_new = jnp.maximum(m_old, jnp.max(sh, axis=-1, keepdims=True))
            ps.append(jnp.exp2((sh - jnp.concatenate([m_new] * (KT // LANES), axis=-1)).astype(BF16)))
            alphas.append(jnp.exp2(m_old - m_new))
            m_sc[rows] = m_new
        pv = _dot(jnp.concatenate(ps, axis=0), vsl_ref[0, 0, pl.ds(k0, KT), :])
        acc_sc[...] = jnp.concatenate(alphas, axis=0) * acc_sc[...] + pv

    def stage(j, slot):
        scores(j + 1, 1 - slot)
        update(j, slot)

    scores(0, 0)
    _two_slot_pipeline(n_act - 1, stage, update)

    start = pl.multiple_of(jnp.maximum(s0 - WINDOW, 0), TQ)
    kpos_w = start + lax.broadcasted_iota(jnp.int32, (1, WK), 1)
    mask_w = (kpos_w <= t_col) & (kpos_w > t_col - WINDOW)
    s_w = _dot_nt(qs, kw_ref[0, 0, pl.ds(start, WK), :])
    pw = []
    for h in range(HG):
        sh = biased(s_w, kpos_w, mask_w, h)
        pw.append(jnp.exp2((sh - jnp.max(sh, axis=-1, keepdims=True)).astype(BF16)))
    acc_w = _dot(jnp.concatenate(pw, axis=0), vw_ref[0, 0, pl.ds(start, WK), :])

    o_cmp = oc_ref[...].astype(F32)
    outs = []
    for h in range(HG):
        rows = slice(h * TQ, (h + 1) * TQ)
        a_s, a_w = acc_sc[rows], acc_w[rows]
        r_s, r_w = gt / a_s, gt / a_w
        outs.append(o_cmp[:, h * DH:(h + 1) * DH]
                    + r_s[:, DH + 3 * h + 1:DH + 3 * h + 2] * a_s[:, :DH]
                    + r_w[:, DH + 3 * h + 2:DH + 3 * h + 3] * a_w[:, :DH])
    o_ref[...] = jnp.concatenate(outs, axis=-1).astype(o_ref.dtype)


def _nsa_main(act, q, sel, oc, gates, slope_tab, expand, ksl, vsl, kw, vw, B, T):
    G, HG, DH, TQ = NSA_GROUPS, NSA_HPG, NSA_DH, NSA_Q_TILE
    nq = T // TQ
    n_slc = sel.shape[3]
    n_ktiles = T // NSA_KEY_TILE
    kv = lambda last: pl.BlockSpec((1, 1, T, last), lambda b, g, i, a: (b, g, 0, 0))
    grid_spec = pltpu.PrefetchScalarGridSpec(
        num_scalar_prefetch=1,
        grid=(B, G, nq),
        in_specs=[pl.BlockSpec((1, 1, HG, TQ, DH), lambda b, g, i, a: (b, g, 0, i, 0)),
                  pl.BlockSpec((1, 1, TQ, n_slc), lambda b, g, i, a: (b, g, i, 0)),
                  pl.BlockSpec((TQ, HG * DH), lambda b, g, i, a: (b * nq + i, g)),
                  pl.BlockSpec((1, 1, TQ, LANES), lambda b, g, i, a: (b, g, i, 0)),
                  pl.BlockSpec((1, HG * TQ, LANES), lambda b, g, i, a: (g, 0, 0)),
                  pl.BlockSpec((n_slc, T), lambda b, g, i, a: (0, 0)),
                  kv(DH), kv(LANES), kv(DH), kv(LANES)],
        out_specs=pl.BlockSpec((TQ, HG * DH), lambda b, g, i, a: (b * nq + i, g)),
        scratch_shapes=[pltpu.VMEM((HG * TQ, LANES), F32), pltpu.VMEM((HG * TQ, LANES), F32),
                        pltpu.VMEM((HG * TQ, NSA_KEY_TILE), F32), pltpu.VMEM((HG * TQ, NSA_KEY_TILE), F32),
                        pltpu.VMEM((TQ, NSA_KEY_TILE), F32), pltpu.VMEM((TQ, NSA_KEY_TILE), F32),
                        pltpu.SMEM((n_ktiles + 1,), jnp.int32)],
    )
    return pl.pallas_call(
        functools.partial(_nsa_main_kernel, n_ktiles=n_ktiles),
        grid_spec=grid_spec,
        out_shape=jax.ShapeDtypeStruct((B * T, NSA_HEADS * DH), BF16),
        compiler_params=_cparams("parallel", "parallel", "arbitrary"),
        name="nsa_select_window",
    )(act, q, sel, oc, gates, slope_tab, expand, ksl, vsl, kw, vw)


def _nsa_mixer(x, B, T, attn_norm, w_in, q_norm, kcmp_norm, kslc_norm, kwin_norm, pos_k, pos_v,
               k_w1, k_b1, k_w2, v_w1, v_b1, v_w2):
    G, HG, DH, TQ = NSA_GROUPS, NSA_HPG, NSA_DH, NSA_Q_TILE
    q, kc, vc, ksl, vsl, kw, vw, gates = _nsa_in(x, attn_norm, w_in, q_norm, kslc_norm, kwin_norm, B, T)

    nrow = T // CMP_STRIDE
    half = CMP_STRIDE * DH
    kcmp, vcmp = _compress(kc.reshape(B, G, nrow, half), vc.reshape(B, G, nrow, half),
                           pos_k, pos_v, k_w1, k_b1, k_w2, v_w1, v_b1, v_w2, kcmp_norm)

    n_slc = T // SLC_BLOCK
    n_top = min(SLC_TOP, n_slc)
    n_slc_pad = max(n_slc, LANES)
    slopes = 2.0 ** (-8.0 * jnp.arange(1, NSA_HEADS + 1, dtype=F32) / NSA_HEADS) * LOG2E
    slope_tab = jnp.broadcast_to(slopes.reshape(G, HG, 1, 1), (G, HG, TQ, LANES)).reshape(G, HG * TQ, LANES)
    cj = jnp.arange(nrow)[:, None] * CMP_STRIDE
    si = jnp.arange(n_slc_pad)[None, :] * SLC_BLOCK
    overlap = ((cj <= si + SLC_BLOCK - 1) & (cj + CMP_BLOCK - 1 >= si)
               & (jnp.arange(nrow)[:, None] < nrow - 1)).astype(BF16)
    blocks_per_tile = NSA_KEY_TILE // SLC_BLOCK
    group_mat = (jnp.arange(n_slc_pad)[:, None] // blocks_per_tile
                 == jnp.arange(LANES)[None, :]).astype(BF16)
    expand = (jnp.arange(n_slc_pad)[:, None] == jnp.arange(T)[None, :] // SLC_BLOCK).astype(BF16)

    oc, sel, act = _nsa_cmp(q, kcmp, vcmp, gates, slope_tab, overlap, group_mat, B, T, n_top)
    n_ktiles = T // NSA_KEY_TILE
    act_i = act[:, :, :, 0, :n_ktiles].astype(jnp.int32).reshape(-1)
    return _nsa_main(act_i, q, sel, oc, gates, slope_tab, expand, ksl, vsl, kw, vw, B, T)


def _rope_layout(w):
    half = QK_ROPE // 2
    z = jnp.zeros(w.shape[:-1] + (LANES // 2 - half,), w.dtype)
    return jnp.concatenate([w[..., :half], z, w[..., half:], z], axis=-1)


def _head_layout(w):
    w = w.reshape(w.shape[:-1] + (MLA_HEADS, MLA_QK))
    w = jnp.concatenate([w[..., :QK_NOPE], _rope_layout(w[..., QK_NOPE:])], axis=-1)
    return w.reshape(w.shape[:-2] + (MLA_HEADS * MLA_QK_PAD,))


def _norm_rope(nope, rope, ss, gn_ref, gr_ref, cos_ref, sin_ref, scale):
    inv = lax.rsqrt(ss * (1.0 / MLA_QK) + NORM_EPS)
    a = nope * inv * gn_ref[...]
    r = rope * inv * gr_ref[...]
    if scale != 1.0:
        a, r = a * scale, r * scale
    r = r * cos_ref[...] + pltpu.roll(r, LANES // 2, 1) * sin_ref[...]
    return jnp.concatenate([a, r], axis=-1).astype(BF16)


def _mla_q_kernel(x_ref, an_ref, wa_ref, qan_ref, wb_ref, gn_ref, gr_ref, cos_ref, sin_ref, o_ref):
    qa = _dot(_rms(x_ref[...], an_ref[...]).astype(BF16), wa_ref[...])
    q = _dot(_rms(qa, qan_ref[...]).astype(BF16), wb_ref[...])
    for h in range(MLA_HEADS):
        nope = q[:, h * MLA_QK_PAD:h * MLA_QK_PAD + QK_NOPE]
        rope = q[:, h * MLA_QK_PAD + QK_NOPE:(h + 1) * MLA_QK_PAD]
        ss = jnp.sum(nope * nope + rope * rope, axis=-1, keepdims=True)
        o_ref[0, h] = _norm_rope(nope, rope, ss, gn_ref, gr_ref, cos_ref, sin_ref, MLA_QK ** -0.5 * LOG2E)


def _mla_kv_kernel(x_ref, n_ref, wa_ref, cn_ref, wb_ref, gn_ref, gr_ref, cos_ref, sin_ref, k_ref, v_ref):
    kv_a = _dot(_rms(x_ref[...], n_ref[...]).astype(BF16), wa_ref[...])
    kv = _dot(_rms(kv_a[:, :KV_LORA], cn_ref[...]).astype(BF16), wb_ref[...])
    rope = kv_a[:, KV_LORA:]
    for h in range(MLA_HEADS):
        base = h * (QK_NOPE + MLA_V)
        nope = kv[:, base:base + QK_NOPE]
        ss = jnp.sum(nope * nope + rope * rope, axis=-1, keepdims=True)
        k_ref[0, h] = _norm_rope(nope, rope, ss, gn_ref, gr_ref, cos_ref, sin_ref, 1.0)
        v = kv[:, base + QK_NOPE:base + QK_NOPE + MLA_V]
        v_ref[0, h] = jnp.concatenate([v, jnp.ones_like(v)], axis=-1).astype(BF16)


def _mla_specs(tm, nt, d, weights):
    full = lambda shape: pl.BlockSpec(shape, lambda i: (0, 0))
    row = lambda: pl.BlockSpec((1, LANES), lambda i: (0, 0))
    tab = lambda: pl.BlockSpec((tm, LANES), lambda i: (i % nt, 0))
    return ([pl.BlockSpec((tm, d), lambda i: (i, 0))] + [full(w.shape) for w in weights]
            + [row(), row(), tab(), tab()])


def _mla_q(x, attn_norm, w_q_a, q_a_norm, w_q_b, gain, cos_l, sin_l, B, T, *, tm=512):
    nt = T // tm
    d = x.shape[1]
    weights = (attn_norm.reshape(1, d), w_q_a, q_a_norm.reshape(1, -1), w_q_b)
    return pl.pallas_call(
        _mla_q_kernel,
        grid=(B * nt,),
        in_specs=_mla_specs(tm, nt, d, weights),
        out_specs=pl.BlockSpec((1, MLA_HEADS, tm, MLA_QK_PAD), lambda i: (i // nt, 0, i % nt, 0)),
        out_shape=jax.ShapeDtypeStruct((B, MLA_HEADS, T, MLA_QK_PAD), BF16),
        compiler_params=_cparams("parallel"),
        name="mla_q_path",
    )(x, *weights, gain[:QK_NOPE].reshape(1, LANES), _rope_layout(gain[QK_NOPE:]).reshape(1, LANES),
      cos_l, sin_l)


def _mla_kv(x, kv_norm, w_kv_a, kv_c_norm, w_kv_b, gain, cos_l, sin_l, B, T, *, tm=512):
    nt = T // tm
    d = x.shape[1]
    weights = (kv_norm.reshape(1, d), w_kv_a, kv_c_norm.reshape(1, -1), w_kv_b)
    return pl.pallas_call(
        _mla_kv_kernel,
        grid=(B * nt,),
        in_specs=_mla_specs(tm, nt, d, weights),
        out_specs=[pl.BlockSpec((1, MLA_HEADS, tm, MLA_QK_PAD), lambda i: (i // nt, 0, i % nt, 0)),
                   pl.BlockSpec((1, MLA_HEADS, tm, 2 * MLA_V), lambda i: (i // nt, 0, i % nt, 0))],
        out_shape=[jax.ShapeDtypeStruct((B, MLA_HEADS, T, MLA_QK_PAD), BF16),
                   jax.ShapeDtypeStruct((B, MLA_HEADS, T, 2 * MLA_V), BF16)],
        compiler_params=_cparams("parallel"),
        name="mla_kv_path",
    )(x, *weights, gain[:QK_NOPE].reshape(1, LANES), _rope_layout(gain[QK_NOPE:]).reshape(1, LANES),
      cos_l, sin_l)


def _two_slot_pipeline(last, stage, finish):
    def pair(j, carry):
        stage(2 * j, 0)
        stage(2 * j + 1, 1)
        return carry

    lax.fori_loop(0, last // 2, pair, 0)

    @pl.when(last % 2 == 1)
    def _():
        stage(last - 1, 0)
        finish(last, 1)

    @pl.when(last % 2 == 0)
    def _():
        finish(last, 0)


def _mla_attn_kernel(q_ref, k_ref, v_ref, o_ref, m_sc, acc_sc, s0_sc, s1_sc):
    TQ = MLA_TILE
    qi = pl.program_id(2)
    q = q_ref[0, 0]
    s_slots = (s0_sc, s1_sc)
    m_sc[...] = jnp.full_like(m_sc, NEG_INF)
    acc_sc[...] = jnp.zeros_like(acc_sc)

    def scores(i, slot):
        k0 = pl.multiple_of(i * TQ, TQ)
        s_slots[slot][...] = _dot_nt(q, k_ref[0, 0, pl.ds(k0, TQ), :])

    def update(i, slot, masked):
        k0 = pl.multiple_of(i * TQ, TQ)

        def load():
            s = s_slots[slot][...]
            if masked:
                qq = lax.broadcasted_iota(jnp.int32, (TQ, TQ), 0)
                kk = lax.broadcasted_iota(jnp.int32, (TQ, TQ), 1)
                s = jnp.where(kk <= qq, s, NEG_INF)
            return s

        m_old = m_sc[...]
        m_new = jnp.maximum(m_old, jnp.max(load(), axis=-1, keepdims=True))
        p = jnp.exp2((load() - jnp.concatenate([m_new] * (TQ // LANES), axis=-1)).astype(BF16))
        alpha = jnp.exp2(m_old - m_new)
        acc_sc[...] = (jnp.concatenate([alpha] * (acc_sc.shape[1] // LANES), axis=-1) * acc_sc[...]
                       + _dot(p, v_ref[0, 0, pl.ds(k0, TQ), :]))
        m_sc[...] = m_new

    def stage(i, slot):
        scores(i + 1, 1 - slot)
        update(i, slot, False)

    scores(0, 0)
    _two_slot_pipeline(qi, stage, lambda i, slot: update(i, slot, True))
    o_ref[...] = (acc_sc[:, :MLA_V] / acc_sc[:, MLA_V:]).astype(o_ref.dtype)


def _mla_attn(q, k, v, B, T):
    H, TQ = MLA_HEADS, MLA_TILE
    nq = T // TQ
    return pl.pallas_call(
        _mla_attn_kernel,
        grid=(B, H, nq),
        in_specs=[pl.BlockSpec((1, 1, TQ, MLA_QK_PAD), lambda b, h, i: (b, h, i, 0)),
                  pl.BlockSpec((1, 1, T, MLA_QK_PAD), lambda b, h, i: (b, h, 0, 0)),
                  pl.BlockSpec((1, 1, T, 2 * MLA_V), lambda b, h, i: (b, h, 0, 0))],
        out_specs=pl.BlockSpec((TQ, MLA_V), lambda b, h, i: (b * nq + i, h)),
        out_shape=jax.ShapeDtypeStruct((B * T, H * MLA_V), BF16),
        scratch_shapes=[pltpu.VMEM((TQ, LANES), F32), pltpu.VMEM((TQ, 2 * MLA_V), F32),
                        pltpu.VMEM((TQ, TQ), F32), pltpu.VMEM((TQ, TQ), F32)],
        compiler_params=_cparams("parallel", "parallel", "arbitrary"),
        name="mla_flash_attn",
    )(q, k, v)


def _pad_cols(w, mult=LANES):
    pad = -w.shape[1] % mult
    return jnp.pad(w, ((0, 0), (0, pad))) if pad else w


def kernel(x, a_attn_norm, a_w_in, a_q_norm, a_kcmp_norm, a_kslc_norm, a_kwin_norm, a_cmp_pos_k, a_cmp_pos_v, a_cmp_k_w1, a_cmp_k_b1, a_cmp_k_w2, a_cmp_v_w1, a_cmp_v_b1, a_cmp_v_w2, a_w_out, kv_norm, kv_w_a, kv_c_norm, kv_w_b, kv_k_norm, b_attn_norm, b_w_q_a, b_q_a_norm, b_w_q_b, b_q_norm, b_w_out, ffn_norm, ffn_w_gate_up, ffn_w_down):
    B, T, D = x.shape
    n_a = a_w_in.shape[0]
    n_b = b_w_q_a.shape[0]
    xs = x.reshape(B * T, D)

    inv = ROPE_THETA ** (-jnp.arange(0, QK_ROPE, 2, dtype=F32) / QK_ROPE)
    ang = jnp.arange(T, dtype=F32)[:, None] * inv[None, :]
    cos, sin = jnp.cos(ang), jnp.sin(ang)
    cos2 = _rope_layout(jnp.concatenate([cos, cos], axis=-1))
    sin2 = _rope_layout(jnp.concatenate([-sin, sin], axis=-1))

    k_shared = v_shared = None
    for layer in range(n_a + n_b):
        if layer < n_a:
            i = layer
            o = _nsa_mixer(xs, B, T, a_attn_norm[i], _pad_cols(a_w_in[i]).astype(BF16),
                           a_q_norm[i], a_kcmp_norm[i], a_kslc_norm[i], a_kwin_norm[i],
                           a_cmp_pos_k[i], a_cmp_pos_v[i], a_cmp_k_w1[i], a_cmp_k_b1[i], a_cmp_k_w2[i],
                           a_cmp_v_w1[i], a_cmp_v_b1[i], a_cmp_v_w2[i])
            w_out = a_w_out[i]
        else:
            j = layer - n_a
            q = _mla_q(xs, b_attn_norm[j], b_w_q_a[j].astype(BF16), b_q_a_norm[j],
                       _head_layout(b_w_q_b[j]).astype(BF16), b_q_norm[j], cos2, sin2, B, T)
            o = _mla_attn(q, k_shared, v_shared, B, T)
            w_out = b_w_out[j]
        xs = _out_ffn(xs, o, w_out.astype(BF16), ffn_norm[layer],
                      ffn_w_gate_up[layer].astype(BF16), ffn_w_down[layer].astype(BF16))
        if layer == n_a - 1:
            w_kv_a = jnp.concatenate([kv_w_a[:, :KV_LORA], _rope_layout(kv_w_a[:, KV_LORA:])], axis=-1)
            k_shared, v_shared = _mla_kv(xs, kv_norm, w_kv_a.astype(BF16), kv_c_norm, kv_w_b.astype(BF16),
                                         kv_k_norm, cos2, sin2, B, T)
    return xs.reshape(B, T, D)
```

```python
import functools

import jax
import jax.numpy as jnp
from jax import lax
from jax.experimental import pallas as pl
from jax.experimental.pallas import tpu as pltpu

F32 = jnp.float32
BF16 = jnp.bfloat16

NORM_EPS = 1e-6
NEG_INF = -1e30
LANES = 128

NSA_HEADS = 16
NSA_GROUPS = 4
NSA_HPG = NSA_HEADS // NSA_GROUPS
NSA_DH = 64
CMP_BLOCK = 32
CMP_STRIDE = 16
CMP_HIDDEN = 256
SLC_BLOCK = 64
SLC_TOP = 16
WINDOW = 512
FORCE_SCORE = 1e4
NSA_Q_TILE = 256
NSA_KEY_TILE = 256
PAIR = 2
LOG2E = 1.4426950408889634

MLA_HEADS = 8
QK_NOPE = 128
QK_ROPE = 64
MLA_QK = QK_NOPE + QK_ROPE
MLA_QK_PAD = QK_NOPE + LANES
MLA_V = 128
Q_LORA = 384
KV_LORA = 256
ROPE_THETA = 10000.0
MLA_TILE = 512

VMEM_LIMIT = 56 * 1024 * 1024


def _cparams(*sem):
    return pltpu.CompilerParams(dimension_semantics=sem, vmem_limit_bytes=VMEM_LIMIT)


def _rms(x, g):
    return x * lax.rsqrt(jnp.mean(x * x, axis=-1, keepdims=True) + NORM_EPS) * g


def _dot(a, b):
    return jnp.dot(a, b, preferred_element_type=F32)


def _dot_nt(a, b):
    return lax.dot_general(a, b, (((1,), (1,)), ((), ())), preferred_element_type=F32)


def _out_ffn_kernel(x_ref, a_ref, wo_ref, g_ref, wgu_ref, wd_ref, o_ref):
    hid = wd_ref.shape[0]
    x = x_ref[...] + _dot(a_ref[...], wo_ref[...])
    h = _rms(x, g_ref[...]).astype(BF16)
    gate = _dot(h, wgu_ref[:, :hid])
    up = _dot(h, wgu_ref[:, hid:])
    a = (gate * jax.nn.sigmoid(gate) * up).astype(BF16)
    o_ref[...] = x + _dot(a, wd_ref[...])


def _out_ffn(x, mixed, w_out, gain, w_gate_up, w_down, *, tm=512):
    n, d = x.shape
    hid = w_down.shape[0]
    dm = mixed.shape[1]
    assert n % tm == 0 and hid % LANES == 0
    resident = lambda shape: pl.BlockSpec(shape, lambda i: (0, 0), pipeline_mode=pl.Buffered(1))
    return pl.pallas_call(
        _out_ffn_kernel,
        grid=(n // tm,),
        in_specs=[pl.BlockSpec((tm, d), lambda i: (i, 0)),
                  pl.BlockSpec((tm, dm), lambda i: (i, 0)),
                  resident((dm, d)),
                  pl.BlockSpec((1, d), lambda i: (0, 0)),
                  resident((d, 2 * hid)), resident((hid, d))],
        out_specs=pl.BlockSpec((tm, d), lambda i: (i, 0)),
        out_shape=jax.ShapeDtypeStruct((n, d), F32),
        compiler_params=_cparams("parallel"),
        name="out_proj_swiglu_ffn",
    )(x, mixed, w_out, gain.reshape(1, d), w_gate_up, w_down)


def _split2(x):
    hi = x.astype(BF16)
    return hi, (x - hi.astype(F32)).astype(BF16)


def _nsa_in_kernel(x_ref, an_ref, w_ref, gain_ref, gsum_ref, gexp_ref,
                   q_ref, kc_ref, vc_ref, ksl_ref, vsl_ref, kw_ref, vw_ref, gt_ref):
    G, HG, DH = NSA_GROUPS, NSA_HPG, NSA_DH
    qw, kvw = NSA_HEADS * DH, G * DH
    tm = x_ref.shape[0]
    p = _dot(_rms(x_ref[...], an_ref[...]).astype(BF16), w_ref[...])
    ones_col = jnp.ones((tm, LANES - DH), F32)
    n_gate = HG * 3
    gate_pad = jnp.zeros((tm, LANES - DH - n_gate), F32)

    def piece(base, g):
        return p[:, base + g * DH: base + (g + 1) * DH]

    xn = jnp.concatenate([p[:, :qw], p[:, qw + 2 * kvw:qw + 3 * kvw], p[:, qw + 4 * kvw:qw + 5 * kvw]], axis=-1)
    hi, lo = _split2(xn * xn)
    ss = _dot(hi, gsum_ref[...]) + _dot(lo, gsum_ref[...])
    hi, lo = _split2(lax.rsqrt(ss * (1.0 / DH) + NORM_EPS))
    xn = xn * (_dot(hi, gexp_ref[...]) + _dot(lo, gexp_ref[...])) * gain_ref[...]

    for g in range(G):
        for h in range(HG):
            q_ref[0, g, h] = xn[:, (g * HG + h) * DH:(g * HG + h + 1) * DH].astype(BF16)
        kc_ref[0, g] = piece(qw, g).astype(BF16)
        vc_ref[0, g] = piece(qw + kvw, g).astype(BF16)
        ksl_ref[0, g] = xn[:, qw + g * DH:qw + (g + 1) * DH].astype(BF16)
        vsl_ref[0, g] = jnp.concatenate([piece(qw + 3 * kvw, g), ones_col], axis=-1).astype(BF16)
        kw_ref[0, g] = xn[:, qw + kvw + g * DH:qw + kvw + (g + 1) * DH].astype(BF16)
        vw_ref[0, g] = jnp.concatenate([piece(qw + 5 * kvw, g), ones_col], axis=-1).astype(BF16)
        gb = qw + 6 * kvw + g * n_gate
        gt_ref[0, g] = jnp.concatenate([jnp.zeros((tm, DH), F32), jax.nn.sigmoid(p[:, gb:gb + n_gate]),
                                        gate_pad], axis=-1)


def _nsa_in(x, attn_norm, w_in, q_norm, kslc_norm, kwin_norm, B, T, *, tm=512):
    G, HG, DH = NSA_GROUPS, NSA_HPG, NSA_DH
    nt = T // tm
    d, width = w_in.shape
    n_norm = (NSA_HEADS + 2 * G) * DH
    gain = jnp.concatenate([jnp.tile(q_norm, NSA_HEADS) * (DH ** -0.5 * LOG2E),
                            jnp.tile(kslc_norm, G), jnp.tile(kwin_norm, G)]).reshape(1, n_norm)
    gsum = (jnp.arange(n_norm)[:, None] // DH == jnp.arange(LANES)[None, :]).astype(BF16)
    full = lambda shape: pl.BlockSpec(shape, lambda i: (0, 0))
    per_group = lambda last: pl.BlockSpec((1, G, tm, last), lambda i: (i // nt, 0, i % nt, 0))
    shp = lambda last, dt: jax.ShapeDtypeStruct((B, G, T, last), dt)
    return pl.pallas_call(
        _nsa_in_kernel,
        grid=(B * nt,),
        in_specs=[pl.BlockSpec((tm, d), lambda i: (i, 0)), full((1, d)), full((d, width)),
                  full((1, n_norm)), full((n_norm, LANES)), full((LANES, n_norm))],
        out_specs=[pl.BlockSpec((1, G, HG, tm, DH), lambda i: (i // nt, 0, 0, i % nt, 0)),
                   per_group(DH), per_group(DH), per_group(DH), per_group(LANES),
                   per_group(DH), per_group(LANES), per_group(LANES)],
        out_shape=[jax.ShapeDtypeStruct((B, G, HG, T, DH), BF16),
                   shp(DH, BF16), shp(DH, BF16), shp(DH, BF16), shp(LANES, BF16),
                   shp(DH, BF16), shp(LANES, BF16), shp(LANES, F32)],
        compiler_params=_cparams("parallel"),
        name="nsa_in_proj_split",
    )(x, attn_norm.reshape(1, d), w_in, gain, gsum, gsum.T)


def _compress_kernel(rk_ref, rv_ref, pek_ref, pev_ref, kw1_ref, kb1_ref, kw2_ref,
                     vw1_ref, vb1_ref, vw2_ref, kn_ref, kcmp_ref, vcmp_ref):
    half = CMP_STRIDE * NSA_DH
    nrow = rk_ref.shape[2]

    def mlp(r_ref, pe_ref, w1_ref, b1_ref, w2_ref):
        r = r_ref[0, 0].astype(BF16)
        ya = _dot(r, w1_ref[:half, :])
        yb = _dot(r, w1_ref[half:, :])
        pe = jnp.broadcast_to(pe_ref[...], (8, 2 * half)).astype(BF16)
        c = _dot(pe, w1_ref[...])[0:1] + b1_ref[...]
        hid = ya + pltpu.roll(yb, nrow - 1, 0) + c
        return _dot(jax.nn.gelu(hid).astype(BF16), w2_ref[...])

    kcmp_ref[0, 0] = _rms(mlp(rk_ref, pek_ref, kw1_ref, kb1_ref, kw2_ref), kn_ref[...]).astype(BF16)
    vcmp_ref[0, 0] = mlp(rv_ref, pev_ref, vw1_ref, vb1_ref, vw2_ref).astype(BF16)


def _compress(rk, rv, pe_k, pe_v, k_w1, k_b1, k_w2, v_w1, v_b1, v_w2, kcmp_norm):
    B, G, nrow, half = rk.shape
    DH, HID = NSA_DH, CMP_HIDDEN
    full = lambda shape: pl.BlockSpec(shape, lambda b, g: (0,) * len(shape))
    r_spec = pl.BlockSpec((1, 1, nrow, half), lambda b, g: (b, g, 0, 0))
    o_spec = pl.BlockSpec((1, 1, nrow, DH), lambda b, g: (b, g, 0, 0))
    return pl.pallas_call(
        _compress_kernel,
        grid=(B, G),
        in_specs=[r_spec, r_spec, full((1, 2 * half)), full((1, 2 * half)),
                  full((2 * half, HID)), full((1, HID)), full((HID, DH)),
                  full((2 * half, HID)), full((1, HID)), full((HID, DH)), full((1, DH))],
        out_specs=[o_spec, o_spec],
        out_shape=[jax.ShapeDtypeStruct((B, G, nrow, DH), BF16)] * 2,
        compiler_params=_cparams("parallel", "parallel"),
        name="nsa_compress",
    )(rk, rv, pe_k.reshape(1, -1), pe_v.reshape(1, -1),
      k_w1.astype(BF16), k_b1.reshape(1, HID), k_w2.astype(BF16),
      v_w1.astype(BF16), v_b1.reshape(1, HID), v_w2.astype(BF16), kcmp_norm.reshape(1, DH))


def _split3(x):
    hi = x.astype(BF16)
    r1 = x - hi.astype(F32)
    mid = r1.astype(BF16)
    lo = (r1 - mid.astype(F32)).astype(BF16)
    return hi, mid, lo


def _nsa_cmp_kernel(q_ref, kc_ref, vc_ref, gt_ref, sl_ref, ov_ref, gm_ref,
                    oc_ref, sel_ref, act_ref, *, n_top):
    HG, DH, TQ = NSA_HPG, NSA_DH, NSA_Q_TILE
    s0 = pl.program_id(2) * TQ
    ncmp = kc_ref.shape[2]
    n_slc = ov_ref.shape[1]
    qs = q_ref[0, 0].reshape(HG * TQ, DH)
    gt = gt_ref[0, 0]

    def attend(nc):
        j = lax.broadcasted_iota(jnp.int32, (1, nc), 1)
        t = s0 + lax.broadcasted_iota(jnp.int32, (TQ, 1), 0)
        mask = j * CMP_STRIDE + (CMP_BLOCK - 1) <= t
        mid = (j * CMP_STRIDE - s0).astype(F32) + 0.5 * (CMP_BLOCK - 1)
        has_key = t >= CMP_BLOCK - 1
        s = _dot_nt(qs, kc_ref[0, 0, :nc, :])
        psum = jnp.zeros((TQ, nc), F32)
        ps = []
        for h in range(HG):
            rows = slice(h * TQ, (h + 1) * TQ)
            slope_row = jnp.concatenate([sl_ref[0, h * TQ:h * TQ + 1, :]] * (nc // LANES), axis=-1)
            sh = jnp.where(mask, s[rows] + slope_row * mid, NEG_INF)
            m = jnp.max(sh, axis=-1, keepdims=True)
            e = jnp.exp2(sh - m)
            l = jnp.sum(e, axis=-1, keepdims=True)
            inv = jnp.where(has_key, 1.0 / l, 0.0)
            p = e * inv
            psum = psum + p
            ps.append(p.astype(BF16))
        o_all = _dot(jnp.concatenate(ps, axis=0), vc_ref[0, 0, :nc, :])
        oc_ref[...] = jnp.concatenate(
            [o_all[h * TQ:(h + 1) * TQ] * gt[:, DH + 3 * h:DH + 3 * h + 1] for h in range(HG)],
            axis=-1).astype(oc_ref.dtype)
        hi, md, lo = _split3(psum)
        ov = ov_ref[:nc, :]
        imp = _dot(hi, ov) + _dot(md, ov) + _dot(lo, ov)
        select(imp, nc * CMP_STRIDE // SLC_BLOCK)

    def select(imp, nb):
        imp_t = imp.T[:nb]
        blk = lax.broadcasted_iota(jnp.int32, (nb, TQ), 0)
        tq = s0 + lax.broadcasted_iota(jnp.int32, (nb, TQ), 1)
        cur = lax.shift_right_logical(tq, SLC_BLOCK.bit_length() - 1)
        forced = (blk == 0) | (blk == cur) | (blk == cur - 1)
        taken = -2.0
        score = jnp.where(forced, taken, jnp.where(blk * SLC_BLOCK <= tq, imp_t, -1.0))

        def pick(_, score):
            mx = jnp.max(score, axis=0, keepdims=True)
            first = jnp.min(jnp.where(score == mx, blk, nb), axis=0, keepdims=True)
            return jnp.where(blk == first, taken, score)

        score = lax.fori_loop(0, n_top - 3, pick, score, unroll=True)
        sel_t = jnp.where(score == taken, 1.0, 0.0)
        if nb < n_slc:
            sel_t = jnp.concatenate([sel_t, jnp.zeros((n_slc - nb, TQ), F32)], axis=0)
        sel = sel_t.T.astype(BF16)
        sel_ref[0, 0] = sel
        cnt = _dot(jnp.ones((8, TQ), BF16), sel)
        act_ref[0, 0, 0] = _dot((cnt > 0.0).astype(BF16), gm_ref[...])

    need = (s0 + TQ) // CMP_STRIDE
    widths = list(range(LANES, ncmp + 1, LANES))
    for idx, nc in enumerate(widths):
        lo_w = widths[idx - 1] if idx else 0
        pl.when((need > lo_w) & (need <= nc))(functools.partial(attend, nc))


def _nsa_cmp(q, kcmp, vcmp, gates, slope_tab, overlap, group_mat, B, T, n_top):
    G, HG, DH, TQ = NSA_GROUPS, NSA_HPG, NSA_DH, NSA_Q_TILE
    nq = T // TQ
    ncmp = kcmp.shape[2]
    n_slc = overlap.shape[1]
    assert n_top >= 3 and ncmp % LANES == 0 and HG < FORCE_SCORE
    return pl.pallas_call(
        functools.partial(_nsa_cmp_kernel, n_top=n_top),
        grid=(B, G, nq),
        in_specs=[pl.BlockSpec((1, 1, HG, TQ, DH), lambda b, g, i: (b, g, 0, i, 0)),
                  pl.BlockSpec((1, 1, ncmp, DH), lambda b, g, i: (b, g, 0, 0)),
                  pl.BlockSpec((1, 1, ncmp, DH), lambda b, g, i: (b, g, 0, 0)),
                  pl.BlockSpec((1, 1, TQ, LANES), lambda b, g, i: (b, g, i, 0)),
                  pl.BlockSpec((1, HG * TQ, LANES), lambda b, g, i: (g, 0, 0)),
                  pl.BlockSpec((ncmp, n_slc), lambda b, g, i: (0, 0)),
                  pl.BlockSpec((n_slc, LANES), lambda b, g, i: (0, 0))],
        out_specs=[pl.BlockSpec((TQ, HG * DH), lambda b, g, i: (b * nq + i, g)),
                   pl.BlockSpec((1, 1, TQ, n_slc), lambda b, g, i: (b, g, i, 0)),
                   pl.BlockSpec((1, 1, 1, 8, LANES), lambda b, g, i: (b, g, i, 0, 0))],
        out_shape=[jax.ShapeDtypeStruct((B * T, NSA_HEADS * DH), BF16),
                   jax.ShapeDtypeStruct((B, G, T, n_slc), BF16),
                   jax.ShapeDtypeStruct((B, G, nq, 8, LANES), F32)],
        compiler_params=_cparams("parallel", "parallel", "arbitrary"),
        name="nsa_cmp_select",
    )(q, kcmp, vcmp, gates, slope_tab, overlap, group_mat)


def _nsa_main_kernel(act_ref, q_ref, sel_ref, oc_ref, gt_ref, sl_ref, ex_ref,
                     ksl_ref, vsl_ref, kw_ref, vw_ref, o_ref,
                     m_sc, acc_sc, s0_sc, s1_sc, pk0_sc, pk1_sc, list_sc, *, n_ktiles):
    HG, DH, TQ, KT = NSA_HPG, NSA_DH, NSA_Q_TILE, NSA_KEY_TILE
    s_slots, pk_slots = (s0_sc, s1_sc), (pk0_sc, pk1_sc)
    WK = WINDOW + TQ
    b, g, qi = pl.program_id(0), pl.program_id(1), pl.program_id(2)
    nq = pl.num_programs(2)
    s0 = qi * TQ
    qs = q_ref[0, 0].reshape(HG * TQ, DH)
    selb = sel_ref[0, 0]
    gt = gt_ref[0, 0]
    t_col = s0 + lax.broadcasted_iota(jnp.int32, (TQ, 1), 0)

    def biased(s, kpos, mask, blk, head=None):
        head = blk if head is None else head
        width = kpos.shape[1]
        slope_row = jnp.concatenate([sl_ref[0, head * TQ:head * TQ + 1, :]] * (width // LANES), axis=-1)
        sh = s[blk * TQ:(blk + 1) * TQ] + slope_row * (kpos - s0).astype(F32)
        return jnp.where(mask, sh, NEG_INF)

    m_sc[...] = jnp.full_like(m_sc, NEG_INF)
    acc_sc[...] = jnp.zeros_like(acc_sc)
    act_base = ((b * NSA_GROUPS + g) * nq + qi) * n_ktiles
    list_sc[0] = 0

    def compact(i, n):
        hit = act_ref[act_base + i] > 0

        @pl.when(hit)
        def _():
            list_sc[n] = i
        return n + hit.astype(jnp.int32)

    n_act = jnp.maximum(lax.fori_loop(0, qi + 1, compact, 0), 1)

    def scores(j, slot):
        k0 = pl.multiple_of(list_sc[j] * KT, KT)
        s_slots[slot][...] = _dot_nt(qs, ksl_ref[0, 0, pl.ds(k0, KT), :])
        pk_slots[slot][...] = _dot(selb, ex_ref[:, pl.ds(k0, KT)])

    def update(j, slot):
        k0 = pl.multiple_of(list_sc[j] * KT, KT)
        kpos = k0 + lax.broadcasted_iota(jnp.int32, (1, KT), 1)
        mask = (pk_slots[slot][...] > 0.5) & (kpos <= t_col)
        ps, alphas = [], []
        for h in range(HG):
            rows = slice(h * TQ, (h + 1) * TQ)
            sh = biased(s_slots[slot], kpos, mask, h)
            m_old = m_sc[rows]
            m_new = jnp.maximum(m_old, jnp.max(sh, axis=-1, keepdims=True))
            ps.append(jnp.exp2((sh - jnp.concatenate([m_new] * (KT // LANES), axis=-1)).astype(BF16)))
            alphas.append(jnp.exp2(m_old - m_new))
            m_sc[rows] = m_new
        pv = _dot(jnp.concatenate(ps, axis=0), vsl_ref[0, 0, pl.ds(k0, KT), :])
        acc_sc[...] = jnp.concatenate(alphas, axis=0) * acc_sc[...] + pv

    def stage(j, slot):
        scores(j + 1, 1 - slot)
        update(j, slot)

    scores(0, 0)
    _two_slot_pipeline(n_act - 1, stage, update)

    start = pl.multiple_of(jnp.maximum(s0 - WINDOW, 0), TQ)
    kpos_w = start + lax.broadcasted_iota(jnp.int32, (1, WK), 1)
    mask_w = (kpos_w <= t_col) & (kpos_w > t_col - WINDOW)
    acc_w = []
    for h0 in range(0, HG, PAIR):
        s_w = _dot_nt(qs[h0 * TQ:(h0 + PAIR) * TQ], kw_ref[0, 0, pl.ds(start, WK), :])
        pw = []
        for h in range(PAIR):
            sh = biased(s_w, kpos_w, mask_w, h, head=h0 + h)
            pw.append(jnp.exp2((sh - jnp.max(sh, axis=-1, keepdims=True)).astype(BF16)))
        acc_w.append(_dot(jnp.concatenate(pw, axis=0), vw_ref[0, 0, pl.ds(start, WK), :]))
    acc_w = jnp.concatenate(acc_w, axis=0)

    o_cmp = oc_ref[...].astype(F32)
    outs = []
    for h in range(HG):
        rows = slice(h * TQ, (h + 1) * TQ)
        a_s, a_w = acc_sc[rows], acc_w[rows]
        r_s, r_w = gt / a_s, gt / a_w
        outs.append(o_cmp[:, h * DH:(h + 1) * DH]
                    + r_s[:, DH + 3 * h + 1:DH + 3 * h + 2] * a_s[:, :DH]
                    + r_w[:, DH + 3 * h + 2:DH + 3 * h + 3] * a_w[:, :DH])
    o_ref[...] = jnp.concatenate(outs, axis=-1).astype(o_ref.dtype)


def _nsa_main(act, q, sel, oc, gates, slope_tab, expand, ksl, vsl, kw, vw, B, T):
    G, HG, DH, TQ = NSA_GROUPS, NSA_HPG, NSA_DH, NSA_Q_TILE
    nq = T // TQ
    n_slc = sel.shape[3]
    n_ktiles = T // NSA_KEY_TILE
    kv = lambda last: pl.BlockSpec((1, 1, T, last), lambda b, g, i, a: (b, g, 0, 0))
    grid_spec = pltpu.PrefetchScalarGridSpec(
        num_scalar_prefetch=1,
        grid=(B, G, nq),
        in_specs=[pl.BlockSpec((1, 1, HG, TQ, DH), lambda b, g, i, a: (b, g, 0, i, 0)),
                  pl.BlockSpec((1, 1, TQ, n_slc), lambda b, g, i, a: (b, g, i, 0)),
                  pl.BlockSpec((TQ, HG * DH), lambda b, g, i, a: (b * nq + i, g)),
                  pl.BlockSpec((1, 1, TQ, LANES), lambda b, g, i, a: (b, g, i, 0)),
                  pl.BlockSpec((1, HG * TQ, LANES), lambda b, g, i, a: (g, 0, 0)),
                  pl.BlockSpec((n_slc, T), lambda b, g, i, a: (0, 0)),
                  kv(DH), kv(LANES), kv(DH), kv(LANES)],
        out_specs=pl.BlockSpec((TQ, HG * DH), lambda b, g, i, a: (b * nq + i, g)),
        scratch_shapes=[pltpu.VMEM((HG * TQ, LANES), F32), pltpu.VMEM((HG * TQ, LANES), F32),
                        pltpu.VMEM((HG * TQ, NSA_KEY_TILE), F32), pltpu.VMEM((HG * TQ, NSA_KEY_TILE), F32),
                        pltpu.VMEM((TQ, NSA_KEY_TILE), F32), pltpu.VMEM((TQ, NSA_KEY_TILE), F32),
                        pltpu.SMEM((n_ktiles + 1,), jnp.int32)],
    )
    return pl.pallas_call(
        functools.partial(_nsa_main_kernel, n_ktiles=n_ktiles),
        grid_spec=grid_spec,
        out_shape=jax.ShapeDtypeStruct((B * T, NSA_HEADS * DH), BF16),
        compiler_params=_cparams("parallel", "parallel", "arbitrary"),
        name="nsa_select_window",
    )(act, q, sel, oc, gates, slope_tab, expand, ksl, vsl, kw, vw)


def _nsa_mixer(x, B, T, attn_norm, w_in, q_norm, kcmp_norm, kslc_norm, kwin_norm, pos_k, pos_v,
               k_w1, k_b1, k_w2, v_w1, v_b1, v_w2):
    G, HG, DH, TQ = NSA_GROUPS, NSA_HPG, NSA_DH, NSA_Q_TILE
    q, kc, vc, ksl, vsl, kw, vw, gates = _nsa_in(x, attn_norm, w_in, q_norm, kslc_norm, kwin_norm, B, T)

    nrow = T // CMP_STRIDE
    half = CMP_STRIDE * DH
    kcmp, vcmp = _compress(kc.reshape(B, G, nrow, half), vc.reshape(B, G, nrow, half),
                           pos_k, pos_v, k_w1, k_b1, k_w2, v_w1, v_b1, v_w2, kcmp_norm)

    n_slc = T // SLC_BLOCK
    n_top = min(SLC_TOP, n_slc)
    n_slc_pad = max(n_slc, LANES)
    slopes = 2.0 ** (-8.0 * jnp.arange(1, NSA_HEADS + 1, dtype=F32) / NSA_HEADS) * LOG2E
    slope_tab = jnp.broadcast_to(slopes.reshape(G, HG, 1, 1), (G, HG, TQ, LANES)).reshape(G, HG * TQ, LANES)
    cj = jnp.arange(nrow)[:, None] * CMP_STRIDE
    si = jnp.arange(n_slc_pad)[None, :] * SLC_BLOCK
    overlap = ((cj <= si + SLC_BLOCK - 1) & (cj + CMP_BLOCK - 1 >= si)
               & (jnp.arange(nrow)[:, None] < nrow - 1)).astype(BF16)
    blocks_per_tile = NSA_KEY_TILE // SLC_BLOCK
    group_mat = (jnp.arange(n_slc_pad)[:, None] // blocks_per_tile
                 == jnp.arange(LANES)[None, :]).astype(BF16)
    expand = (jnp.arange(n_slc_pad)[:, None] == jnp.arange(T)[None, :] // SLC_BLOCK).astype(BF16)

    oc, sel, act = _nsa_cmp(q, kcmp, vcmp, gates, slope_tab, overlap, group_mat, B, T, n_top)
    n_ktiles = T // NSA_KEY_TILE
    act_i = act[:, :, :, 0, :n_ktiles].astype(jnp.int32).reshape(-1)
    return _nsa_main(act_i, q, sel, oc, gates, slope_tab, expand, ksl, vsl, kw, vw, B, T)


def _rope_layout(w):
    half = QK_ROPE // 2
    z = jnp.zeros(w.shape[:-1] + (LANES // 2 - half,), w.dtype)
    return jnp.concatenate([w[..., :half], z, w[..., half:], z], axis=-1)


def _head_layout(w):
    w = w.reshape(w.shape[:-1] + (MLA_HEADS, MLA_QK))
    w = jnp.concatenate([w[..., :QK_NOPE], _rope_layout(w[..., QK_NOPE:])], axis=-1)
    return w.reshape(w.shape[:-2] + (MLA_HEADS * MLA_QK_PAD,))


def _roped(rope, gr_ref, cos_ref, sin_ref):
    r = rope * gr_ref[...]
    return r * cos_ref[...] + pltpu.roll(r, LANES // 2, 1) * sin_ref[...]


def _head_inv_rms(nope, rope_sq):
    ss = jnp.sum(nope * nope + rope_sq, axis=-1, keepdims=True)
    return lax.rsqrt(ss * (1.0 / MLA_QK) + NORM_EPS)


def _mla_q_kernel(x_ref, an_ref, wa_ref, qan_ref, wb_ref, gn_ref, gr_ref, cos_ref, sin_ref, o_ref):
    qa = _dot(_rms(x_ref[...], an_ref[...]).astype(BF16), wa_ref[...])
    q = _dot(_rms(qa, qan_ref[...]).astype(BF16), wb_ref[...])
    for h in range(MLA_HEADS):
        nope = q[:, h * MLA_QK_PAD:h * MLA_QK_PAD + QK_NOPE]
        rope = q[:, h * MLA_QK_PAD + QK_NOPE:(h + 1) * MLA_QK_PAD]
        inv = _head_inv_rms(nope, rope * rope)
        o_ref[0, h] = jnp.concatenate([nope * inv * gn_ref[...], _roped(rope * inv, gr_ref, cos_ref, sin_ref)],
                                      axis=-1).astype(BF16)


def _mla_kv_kernel(x_ref, n_ref, wa_ref, cn_ref, wb_ref, gn_ref, gr_ref, cos_ref, sin_ref, k_ref, v_ref):
    kv_a = _dot(_rms(x_ref[...], n_ref[...]).astype(BF16), wa_ref[...])
    kv = _dot(_rms(kv_a[:, :KV_LORA], cn_ref[...]).astype(BF16), wb_ref[...])
    rope = kv_a[:, KV_LORA:]
    rope_sq = rope * rope
    roped = _roped(rope, gr_ref, cos_ref, sin_ref)
    for h in range(MLA_HEADS):
        base = h * (QK_NOPE + MLA_V)
        nope = kv[:, base:base + QK_NOPE]
        inv = _head_inv_rms(nope, rope_sq)
        k_ref[0, h] = jnp.concatenate([nope * inv * gn_ref[...], inv * roped], axis=-1).astype(BF16)
        v = kv[:, base + QK_NOPE:base + QK_NOPE + MLA_V]
        v_ref[0, h] = jnp.concatenate([v, jnp.ones_like(v)], axis=-1).astype(BF16)


def _mla_specs(tm, nt, d, weights):
    full = lambda shape: pl.BlockSpec(shape, lambda i: (0, 0))
    row = lambda: pl.BlockSpec((1, LANES), lambda i: (0, 0))
    tab = lambda: pl.BlockSpec((tm, LANES), lambda i: (i % nt, 0))
    return ([pl.BlockSpec((tm, d), lambda i: (i, 0))] + [full(w.shape) for w in weights]
            + [row(), row(), tab(), tab()])


def _mla_q(x, attn_norm, w_q_a, q_a_norm, w_q_b, gain, cos_l, sin_l, B, T, *, tm=512):
    nt = T // tm
    d = x.shape[1]
    weights = (attn_norm.reshape(1, d), w_q_a, q_a_norm.reshape(1, -1), w_q_b)
    gain = gain * (MLA_QK ** -0.5 * LOG2E)
    return pl.pallas_call(
        _mla_q_kernel,
        grid=(B * nt,),
        in_specs=_mla_specs(tm, nt, d, weights),
        out_specs=pl.BlockSpec((1, MLA_HEADS, tm, MLA_QK_PAD), lambda i: (i // nt, 0, i % nt, 0)),
        out_shape=jax.ShapeDtypeStruct((B, MLA_HEADS, T, MLA_QK_PAD), BF16),
        compiler_params=_cparams("parallel"),
        name="mla_q_path",
    )(x, *weights, gain[:QK_NOPE].reshape(1, LANES), _rope_layout(gain[QK_NOPE:]).reshape(1, LANES),
      cos_l, sin_l)


def _mla_kv(x, kv_norm, w_kv_a, kv_c_norm, w_kv_b, gain, cos_l, sin_l, B, T, *, tm=512):
    nt = T // tm
    d = x.shape[1]
    weights = (kv_norm.reshape(1, d), w_kv_a, kv_c_norm.reshape(1, -1), w_kv_b)
    return pl.pallas_call(
        _mla_kv_kernel,
        grid=(B * nt,),
        in_specs=_mla_specs(tm, nt, d, weights),
        out_specs=[pl.BlockSpec((1, MLA_HEADS, tm, MLA_QK_PAD), lambda i: (i // nt, 0, i % nt, 0)),
                   pl.BlockSpec((1, MLA_HEADS, tm, 2 * MLA_V), lambda i: (i // nt, 0, i % nt, 0))],
        out_shape=[jax.ShapeDtypeStruct((B, MLA_HEADS, T, MLA_QK_PAD), BF16),
                   jax.ShapeDtypeStruct((B, MLA_HEADS, T, 2 * MLA_V), BF16)],
        compiler_params=_cparams("parallel"),
        name="mla_kv_path",
    )(x, *weights, gain[:QK_NOPE].reshape(1, LANES), _rope_layout(gain[QK_NOPE:]).reshape(1, LANES),
      cos_l, sin_l)


def _two_slot_pipeline(last, stage, finish):
    def pair(j, carry):
        stage(2 * j, 0)
        stage(2 * j + 1, 1)
        return carry

    lax.fori_loop(0, last // 2, pair, 0)

    @pl.when(last % 2 == 1)
    def _():
        stage(last - 1, 0)
        finish(last, 1)

    @pl.when(last % 2 == 0)
    def _():
        finish(last, 0)


def _mla_attn_kernel(q_ref, k_ref, v_ref, o_ref, m_sc, acc_sc, s0_sc, s1_sc):
    TQ = MLA_TILE
    qi = pl.program_id(2)
    q = q_ref[0, 0]
    s_slots = (s0_sc, s1_sc)
    m_sc[...] = jnp.full_like(m_sc, NEG_INF)
    acc_sc[...] = jnp.zeros_like(acc_sc)

    def scores(i, slot):
        k0 = pl.multiple_of(i * TQ, TQ)
        s_slots[slot][...] = _dot_nt(q, k_ref[0, 0, pl.ds(k0, TQ), :])

    def update(i, slot, masked):
        k0 = pl.multiple_of(i * TQ, TQ)

        def load():
            s = s_slots[slot][...]
            if masked:
                qq = lax.broadcasted_iota(jnp.int32, (TQ, TQ), 0)
                kk = lax.broadcasted_iota(jnp.int32, (TQ, TQ), 1)
                s = jnp.where(kk <= qq, s, NEG_INF)
            return s

        m_old = m_sc[...]
        m_new = jnp.maximum(m_old, jnp.max(load(), axis=-1, keepdims=True))
        p = jnp.exp2((load() - jnp.concatenate([m_new] * (TQ // LANES), axis=-1)).astype(BF16))
        alpha = jnp.exp2(m_old - m_new)
        acc_sc[...] = (jnp.concatenate([alpha] * (acc_sc.shape[1] // LANES), axis=-1) * acc_sc[...]
                       + _dot(p, v_ref[0, 0, pl.ds(k0, TQ), :]))
        m_sc[...] = m_new

    def stage(i, slot):
        scores(i + 1, 1 - slot)
        update(i, slot, False)

    scores(0, 0)
    _two_slot_pipeline(qi, stage, lambda i, slot: update(i, slot, True))
    o_ref[...] = (acc_sc[:, :MLA_V] / acc_sc[:, MLA_V:]).astype(o_ref.dtype)


def _mla_attn(q, k, v, B, T):
    H, TQ = MLA_HEADS, MLA_TILE
    nq = T // TQ
    return pl.pallas_call(
        _mla_attn_kernel,
        grid=(B, H, nq),
        in_specs=[pl.BlockSpec((1, 1, TQ, MLA_QK_PAD), lambda b, h, i: (b, h, i, 0)),
                  pl.BlockSpec((1, 1, T, MLA_QK_PAD), lambda b, h, i: (b, h, 0, 0)),
                  pl.BlockSpec((1, 1, T, 2 * MLA_V), lambda b, h, i: (b, h, 0, 0))],
        out_specs=pl.BlockSpec((TQ, MLA_V), lambda b, h, i: (b * nq + i, h)),
        out_shape=jax.ShapeDtypeStruct((B * T, H * MLA_V), BF16),
        scratch_shapes=[pltpu.VMEM((TQ, LANES), F32), pltpu.VMEM((TQ, 2 * MLA_V), F32),
                        pltpu.VMEM((TQ, TQ), F32), pltpu.VMEM((TQ, TQ), F32)],
        compiler_params=_cparams("parallel", "parallel", "arbitrary"),
        name="mla_flash_attn",
    )(q, k, v)


def _pad_cols(w, mult=LANES):
    pad = -w.shape[1] % mult
    return jnp.pad(w, ((0, 0), (0, pad))) if pad else w


def kernel(x, a_attn_norm, a_w_in, a_q_norm, a_kcmp_norm, a_kslc_norm, a_kwin_norm, a_cmp_pos_k, a_cmp_pos_v, a_cmp_k_w1, a_cmp_k_b1, a_cmp_k_w2, a_cmp_v_w1, a_cmp_v_b1, a_cmp_v_w2, a_w_out, kv_norm, kv_w_a, kv_c_norm, kv_w_b, kv_k_norm, b_attn_norm, b_w_q_a, b_q_a_norm, b_w_q_b, b_q_norm, b_w_out, ffn_norm, ffn_w_gate_up, ffn_w_down):
    B, T, D = x.shape
    n_a = a_w_in.shape[0]
    n_b = b_w_q_a.shape[0]
    xs = x.reshape(B * T, D)

    inv = ROPE_THETA ** (-jnp.arange(0, QK_ROPE, 2, dtype=F32) / QK_ROPE)
    ang = jnp.arange(T, dtype=F32)[:, None] * inv[None, :]
    cos, sin = jnp.cos(ang), jnp.sin(ang)
    cos2 = _rope_layout(jnp.concatenate([cos, cos], axis=-1))
    sin2 = _rope_layout(jnp.concatenate([-sin, sin], axis=-1))

    k_shared = v_shared = None
    for layer in range(n_a + n_b):
        if layer < n_a:
            i = layer
            o = _nsa_mixer(xs, B, T, a_attn_norm[i], _pad_cols(a_w_in[i]).astype(BF16),
                           a_q_norm[i], a_kcmp_norm[i], a_kslc_norm[i], a_kwin_norm[i],
                           a_cmp_pos_k[i], a_cmp_pos_v[i], a_cmp_k_w1[i], a_cmp_k_b1[i], a_cmp_k_w2[i],
                           a_cmp_v_w1[i], a_cmp_v_b1[i], a_cmp_v_w2[i])
            w_out = a_w_out[i]
        else:
            j = layer - n_a
            q = _mla_q(xs, b_attn_norm[j], b_w_q_a[j].astype(BF16), b_q_a_norm[j],
                       _head_layout(b_w_q_b[j]).astype(BF16), b_q_norm[j], cos2, sin2, B, T)
            o = _mla_attn(q, k_shared, v_shared, B, T)
            w_out = b_w_out[j]
        xs = _out_ffn(xs, o, w_out.astype(BF16), ffn_norm[layer],
                      ffn_w_gate_up[layer].astype(BF16), ffn_w_down[layer].astype(BF16))
        if layer == n_a - 1:
            w_kv_a = jnp.concatenate([kv_w_a[:, :KV_LORA], _rope_layout(kv_w_a[:, KV_LORA:])], axis=-1)
            k_shared, v_shared = _mla_kv(xs, kv_norm, w_kv_a.astype(BF16), kv_c_norm, kv_w_b.astype(BF16),
                                         kv_k_norm, cos2, sin2, B, T)
    return xs.reshape(B, T, D)
```

```python
import functools

import jax
import jax.numpy as jnp
from jax import lax
from jax.experimental import pallas as pl
from jax.experimental.pallas import tpu as pltpu

F32 = jnp.float32
BF16 = jnp.bfloat16

NORM_EPS = 1e-6
NEG_INF = -1e30
LANES = 128

NSA_HEADS = 16
NSA_GROUPS = 4
NSA_HPG = NSA_HEADS // NSA_GROUPS
NSA_DH = 64
CMP_BLOCK = 32
CMP_STRIDE = 16
CMP_HIDDEN = 256
SLC_BLOCK = 64
SLC_TOP = 16
WINDOW = 512
FORCE_SCORE = 1e4
NSA_Q_TILE = 256
NSA_CMP_TILE = 1024
NSA_KEY_TILE = 256
PAIR = 2
LOG2E = 1.4426950408889634

MLA_HEADS = 8
QK_NOPE = 128
QK_ROPE = 64
MLA_QK = QK_NOPE + QK_ROPE
MLA_QK_PAD = QK_NOPE + LANES
MLA_V = 128
Q_LORA = 384
KV_LORA = 256
ROPE_THETA = 10000.0
MLA_TILE = 512

VMEM_LIMIT = 56 * 1024 * 1024


def _cparams(*sem):
    return pltpu.CompilerParams(dimension_semantics=sem, vmem_limit_bytes=VMEM_LIMIT)


def _rms(x, g):
    return x * lax.rsqrt(jnp.mean(x * x, axis=-1, keepdims=True) + NORM_EPS) * g


def _dot(a, b):
    return jnp.dot(a, b, preferred_element_type=F32)


def _dot_nt(a, b):
    return lax.dot_general(a, b, (((1,), (1,)), ((), ())), preferred_element_type=F32)


def _out_ffn_kernel(x_ref, a_ref, wo_ref, g_ref, wgu_ref, wd_ref, o_ref):
    hid = wd_ref.shape[0]
    x = x_ref[...] + _dot(a_ref[...], wo_ref[...])
    h = _rms(x, g_ref[...]).astype(BF16)
    gate = _dot(h, wgu_ref[:, :hid])
    up = _dot(h, wgu_ref[:, hid:])
    a = (gate * jax.nn.sigmoid(gate) * up).astype(BF16)
    o_ref[...] = x + _dot(a, wd_ref[...])


def _out_ffn(x, mixed, w_out, gain, w_gate_up, w_down, *, tm=512):
    n, d = x.shape
    hid = w_down.shape[0]
    dm = mixed.shape[1]
    assert n % tm == 0 and hid % LANES == 0
    resident = lambda shape: pl.BlockSpec(shape, lambda i: (0, 0), pipeline_mode=pl.Buffered(1))
    return pl.pallas_call(
        _out_ffn_kernel,
        grid=(n // tm,),
        in_specs=[pl.BlockSpec((tm, d), lambda i: (i, 0)),
                  pl.BlockSpec((tm, dm), lambda i: (i, 0)),
                  resident((dm, d)),
                  pl.BlockSpec((1, d), lambda i: (0, 0)),
                  resident((d, 2 * hid)), resident((hid, d))],
        out_specs=pl.BlockSpec((tm, d), lambda i: (i, 0)),
        out_shape=jax.ShapeDtypeStruct((n, d), F32),
        compiler_params=_cparams("parallel"),
        name="out_proj_swiglu_ffn",
    )(x, mixed, w_out, gain.reshape(1, d), w_gate_up, w_down)


def _split2(x):
    hi = x.astype(BF16)
    return hi, (x - hi.astype(F32)).astype(BF16)


def _nsa_in_kernel(x_ref, an_ref, w_ref, gain_ref, gsum_ref, gexp_ref,
                   q_ref, kc_ref, vc_ref, ksl_ref, vsl_ref, kw_ref, vw_ref, gt_ref):
    G, HG, DH = NSA_GROUPS, NSA_HPG, NSA_DH
    qw, kvw = NSA_HEADS * DH, G * DH
    tm = x_ref.shape[0]
    p = _dot(_rms(x_ref[...], an_ref[...]).astype(BF16), w_ref[...])
    ones_col = jnp.ones((tm, LANES - DH), F32)
    n_gate = HG * 3
    gate_pad = jnp.zeros((tm, LANES - DH - n_gate), F32)

    def piece(base, g):
        return p[:, base + g * DH: base + (g + 1) * DH]

    xn = jnp.concatenate([p[:, :qw], p[:, qw + 2 * kvw:qw + 3 * kvw], p[:, qw + 4 * kvw:qw + 5 * kvw]], axis=-1)
    hi, lo = _split2(xn * xn)
    ss = _dot(hi, gsum_ref[...]) + _dot(lo, gsum_ref[...])
    hi, lo = _split2(lax.rsqrt(ss * (1.0 / DH) + NORM_EPS))
    xn = xn * (_dot(hi, gexp_ref[...]) + _dot(lo, gexp_ref[...])) * gain_ref[...]

    for g in range(G):
        for h in range(HG):
            q_ref[0, g, h] = xn[:, (g * HG + h) * DH:(g * HG + h + 1) * DH].astype(BF16)
        kc_ref[0, g] = piece(qw, g).astype(BF16)
        vc_ref[0, g] = piece(qw + kvw, g).astype(BF16)
        ksl_ref[0, g] = xn[:, qw + g * DH:qw + (g + 1) * DH].astype(BF16)
        vsl_ref[0, g] = jnp.concatenate([piece(qw + 3 * kvw, g), ones_col], axis=-1).astype(BF16)
        kw_ref[0, g] = xn[:, qw + kvw + g * DH:qw + kvw + (g + 1) * DH].astype(BF16)
        vw_ref[0, g] = jnp.concatenate([piece(qw + 5 * kvw, g), ones_col], axis=-1).astype(BF16)
        gb = qw + 6 * kvw + g * n_gate
        gt_ref[0, g] = jnp.concatenate([jnp.zeros((tm, DH), F32), jax.nn.sigmoid(p[:, gb:gb + n_gate]),
                                        gate_pad], axis=-1)


def _nsa_in(x, attn_norm, w_in, q_norm, kslc_norm, kwin_norm, B, T, *, tm=512):
    G, HG, DH = NSA_GROUPS, NSA_HPG, NSA_DH
    nt = T // tm
    d, width = w_in.shape
    n_norm = (NSA_HEADS + 2 * G) * DH
    gain = jnp.concatenate([jnp.tile(q_norm, NSA_HEADS) * (DH ** -0.5 * LOG2E),
                            jnp.tile(kslc_norm, G), jnp.tile(kwin_norm, G)]).reshape(1, n_norm)
    gsum = (jnp.arange(n_norm)[:, None] // DH == jnp.arange(LANES)[None, :]).astype(BF16)
    full = lambda shape: pl.BlockSpec(shape, lambda i: (0, 0))
    per_group = lambda last: pl.BlockSpec((1, G, tm, last), lambda i: (i // nt, 0, i % nt, 0))
    shp = lambda last, dt: jax.ShapeDtypeStruct((B, G, T, last), dt)
    return pl.pallas_call(
        _nsa_in_kernel,
        grid=(B * nt,),
        in_specs=[pl.BlockSpec((tm, d), lambda i: (i, 0)), full((1, d)), full((d, width)),
                  full((1, n_norm)), full((n_norm, LANES)), full((LANES, n_norm))],
        out_specs=[pl.BlockSpec((1, G, HG, tm, DH), lambda i: (i // nt, 0, 0, i % nt, 0)),
                   per_group(DH), per_group(DH), per_group(DH), per_group(LANES),
                   per_group(DH), per_group(LANES), per_group(LANES)],
        out_shape=[jax.ShapeDtypeStruct((B, G, HG, T, DH), BF16),
                   shp(DH, BF16), shp(DH, BF16), shp(DH, BF16), shp(LANES, BF16),
                   shp(DH, BF16), shp(LANES, BF16), shp(LANES, F32)],
        compiler_params=_cparams("parallel"),
        name="nsa_in_proj_split",
    )(x, attn_norm.reshape(1, d), w_in, gain, gsum, gsum.T)


def _compress_kernel(rk_ref, rv_ref, pek_ref, pev_ref, kw1_ref, kb1_ref, kw2_ref,
                     vw1_ref, vb1_ref, vw2_ref, kn_ref, kcmp_ref, vcmp_ref):
    half = CMP_STRIDE * NSA_DH
    nrow = rk_ref.shape[2]

    def mlp(r_ref, pe_ref, w1_ref, b1_ref, w2_ref):
        r = r_ref[0, 0].astype(BF16)
        ya = _dot(r, w1_ref[:half, :])
        yb = _dot(r, w1_ref[half:, :])
        pe = jnp.broadcast_to(pe_ref[...], (8, 2 * half)).astype(BF16)
        c = _dot(pe, w1_ref[...])[0:1] + b1_ref[...]
        hid = ya + pltpu.roll(yb, nrow - 1, 0) + c
        return _dot(jax.nn.gelu(hid).astype(BF16), w2_ref[...])

    kcmp_ref[0, 0] = _rms(mlp(rk_ref, pek_ref, kw1_ref, kb1_ref, kw2_ref), kn_ref[...]).astype(BF16)
    vcmp_ref[0, 0] = mlp(rv_ref, pev_ref, vw1_ref, vb1_ref, vw2_ref).astype(BF16)


def _compress(rk, rv, pe_k, pe_v, k_w1, k_b1, k_w2, v_w1, v_b1, v_w2, kcmp_norm):
    B, G, nrow, half = rk.shape
    DH, HID = NSA_DH, CMP_HIDDEN
    full = lambda shape: pl.BlockSpec(shape, lambda b, g: (0,) * len(shape))
    r_spec = pl.BlockSpec((1, 1, nrow, half), lambda b, g: (b, g, 0, 0))
    o_spec = pl.BlockSpec((1, 1, nrow, DH), lambda b, g: (b, g, 0, 0))
    return pl.pallas_call(
        _compress_kernel,
        grid=(B, G),
        in_specs=[r_spec, r_spec, full((1, 2 * half)), full((1, 2 * half)),
                  full((2 * half, HID)), full((1, HID)), full((HID, DH)),
                  full((2 * half, HID)), full((1, HID)), full((HID, DH)), full((1, DH))],
        out_specs=[o_spec, o_spec],
        out_shape=[jax.ShapeDtypeStruct((B, G, nrow, DH), BF16)] * 2,
        compiler_params=_cparams("parallel", "parallel"),
        name="nsa_compress",
    )(rk, rv, pe_k.reshape(1, -1), pe_v.reshape(1, -1),
      k_w1.astype(BF16), k_b1.reshape(1, HID), k_w2.astype(BF16),
      v_w1.astype(BF16), v_b1.reshape(1, HID), v_w2.astype(BF16), kcmp_norm.reshape(1, DH))


def _split3(x):
    hi = x.astype(BF16)
    r1 = x - hi.astype(F32)
    mid = r1.astype(BF16)
    lo = (r1 - mid.astype(F32)).astype(BF16)
    return hi, mid, lo


def _nsa_cmp_kernel(q_ref, kc_ref, vc_ref, gt_ref, sl_ref, ov_ref, gm_ref,
                    oc_ref, sel_ref, act_ref, *, n_top):
    HG, DH, TQ, TM = NSA_HPG, NSA_DH, NSA_CMP_TILE, NSA_Q_TILE
    s0 = pl.program_id(2) * TQ
    ncmp = kc_ref.shape[2]
    n_slc = ov_ref.shape[1]
    qs = q_ref[0, 0].reshape(HG * TQ, DH)
    gt = gt_ref[0, 0]

    def attend(nc):
        j = lax.broadcasted_iota(jnp.int32, (1, nc), 1)
        t = s0 + lax.broadcasted_iota(jnp.int32, (TQ, 1), 0)
        mask = j * CMP_STRIDE + (CMP_BLOCK - 1) <= t
        mid = (j * CMP_STRIDE - s0).astype(F32) + 0.5 * (CMP_BLOCK - 1)
        has_key = t >= CMP_BLOCK - 1
        s = _dot_nt(qs, kc_ref[0, 0, :nc, :])
        psum = jnp.zeros((TQ, nc), F32)
        ps = []
        for h in range(HG):
            rows = slice(h * TQ, (h + 1) * TQ)
            slope_row = jnp.concatenate([sl_ref[0, h * TM:h * TM + 1, :]] * (nc // LANES), axis=-1)
            sh = jnp.where(mask, s[rows] + slope_row * mid, NEG_INF)
            m = jnp.max(sh, axis=-1, keepdims=True)
            e = jnp.exp2(sh - m)
            l = jnp.sum(e, axis=-1, keepdims=True)
            inv = jnp.where(has_key, 1.0 / l, 0.0)
            p = e * inv
            psum = psum + p
            ps.append(p.astype(BF16))
        o_all = _dot(jnp.concatenate(ps, axis=0), vc_ref[0, 0, :nc, :])
        oc_ref[...] = jnp.concatenate(
            [o_all[h * TQ:(h + 1) * TQ] * gt[:, DH + 3 * h:DH + 3 * h + 1] for h in range(HG)],
            axis=-1).astype(oc_ref.dtype)
        hi, md, lo = _split3(psum)
        ov = ov_ref[:nc, :]
        imp = _dot(hi, ov) + _dot(md, ov) + _dot(lo, ov)
        select(imp, nc * CMP_STRIDE // SLC_BLOCK)

    def select(imp, nb):
        imp_t = imp.T[:nb]
        blk = lax.broadcasted_iota(jnp.int32, (nb, TQ), 0)
        tq = s0 + lax.broadcasted_iota(jnp.int32, (nb, TQ), 1)
        cur = lax.shift_right_logical(tq, SLC_BLOCK.bit_length() - 1)
        forced = (blk == 0) | (blk == cur) | (blk == cur - 1)
        taken = -2.0
        score = jnp.where(forced, taken, jnp.where(blk * SLC_BLOCK <= tq, imp_t, -1.0))

        def pick(_, score):
            mx = jnp.max(score, axis=0, keepdims=True)
            first = jnp.min(jnp.where(score == mx, blk, nb), axis=0, keepdims=True)
            return jnp.where(blk == first, taken, score)

        score = lax.fori_loop(0, n_top - 3, pick, score, unroll=True)
        sel_t = jnp.where(score == taken, 1.0, 0.0)
        if nb < n_slc:
            sel_t = jnp.concatenate([sel_t, jnp.zeros((n_slc - nb, TQ), F32)], axis=0)
        sel = sel_t.T.astype(BF16)
        sel_ref[0, 0] = sel
        for sub in range(TQ // TM):
            cnt = _dot(jnp.ones((8, TM), BF16), sel[sub * TM:(sub + 1) * TM])
            act_ref[0, 0, sub] = _dot((cnt > 0.0).astype(BF16), gm_ref[...])

    need = (s0 + TQ) // CMP_STRIDE
    widths = list(range(LANES, ncmp + 1, LANES))
    for idx, nc in enumerate(widths):
        lo_w = widths[idx - 1] if idx else 0
        pl.when((need > lo_w) & (need <= nc))(functools.partial(attend, nc))


def _nsa_cmp(q, kcmp, vcmp, gates, slope_tab, overlap, group_mat, B, T, n_top):
    G, HG, DH, TQ, TM = NSA_GROUPS, NSA_HPG, NSA_DH, NSA_CMP_TILE, NSA_Q_TILE
    nq = T // TQ
    ncmp = kcmp.shape[2]
    n_slc = overlap.shape[1]
    assert n_top >= 3 and ncmp % LANES == 0 and HG < FORCE_SCORE
    return pl.pallas_call(
        functools.partial(_nsa_cmp_kernel, n_top=n_top),
        grid=(B, G, nq),
        in_specs=[pl.BlockSpec((1, 1, HG, TQ, DH), lambda b, g, i: (b, g, 0, i, 0)),
                  pl.BlockSpec((1, 1, ncmp, DH), lambda b, g, i: (b, g, 0, 0)),
                  pl.BlockSpec((1, 1, ncmp, DH), lambda b, g, i: (b, g, 0, 0)),
                  pl.BlockSpec((1, 1, TQ, LANES), lambda b, g, i: (b, g, i, 0)),
                  pl.BlockSpec((1, HG * TM, LANES), lambda b, g, i: (g, 0, 0)),
                  pl.BlockSpec((ncmp, n_slc), lambda b, g, i: (0, 0)),
                  pl.BlockSpec((n_slc, LANES), lambda b, g, i: (0, 0))],
        out_specs=[pl.BlockSpec((TQ, HG * DH), lambda b, g, i: (b * nq + i, g)),
                   pl.BlockSpec((1, 1, TQ, n_slc), lambda b, g, i: (b, g, i, 0)),
                   pl.BlockSpec((1, 1, TQ // TM, 8, LANES), lambda b, g, i: (b, g, i, 0, 0))],
        out_shape=[jax.ShapeDtypeStruct((B * T, NSA_HEADS * DH), BF16),
                   jax.ShapeDtypeStruct((B, G, T, n_slc), BF16),
                   jax.ShapeDtypeStruct((B, G, T // TM, 8, LANES), F32)],
        compiler_params=_cparams("parallel", "parallel", "arbitrary"),
        name="nsa_cmp_select",
    )(q, kcmp, vcmp, gates, slope_tab, overlap, group_mat)


def _nsa_main_kernel(act_ref, q_ref, sel_ref, oc_ref, gt_ref, sl_ref, ex_ref,
                     ksl_ref, vsl_ref, kw_ref, vw_ref, o_ref,
                     m_sc, acc_sc, s0_sc, s1_sc, pk0_sc, pk1_sc, list_sc, *, n_ktiles):
    HG, DH, TQ, KT = NSA_HPG, NSA_DH, NSA_Q_TILE, NSA_KEY_TILE
    s_slots, pk_slots = (s0_sc, s1_sc), (pk0_sc, pk1_sc)
    WK = WINDOW + TQ
    b, g, qi = pl.program_id(0), pl.program_id(1), pl.program_id(2)
    nq = pl.num_programs(2)
    s0 = qi * TQ
    qs = q_ref[0, 0].reshape(HG * TQ, DH)
    selb = sel_ref[0, 0]
    gt = gt_ref[0, 0]
    t_col = s0 + lax.broadcasted_iota(jnp.int32, (TQ, 1), 0)

    def biased(s, kpos, mask, blk, head=None):
        head = blk if head is None else head
        width = kpos.shape[1]
        slope_row = jnp.concatenate([sl_ref[0, head * TQ:head * TQ + 1, :]] * (width // LANES), axis=-1)
        sh = s[blk * TQ:(blk + 1) * TQ] + slope_row * (kpos - s0).astype(F32)
        return jnp.where(mask, sh, NEG_INF)

    m_sc[...] = jnp.full_like(m_sc, NEG_INF)
    acc_sc[...] = jnp.zeros_like(acc_sc)
    act_base = ((b * NSA_GROUPS + g) * nq + qi) * n_ktiles
    list_sc[0] = 0

    def compact(i, n):
        hit = act_ref[act_base + i] > 0

        @pl.when(hit)
        def _():
            list_sc[n] = i
        return n + hit.astype(jnp.int32)

    n_act = jnp.maximum(lax.fori_loop(0, qi + 1, compact, 0), 1)

    def scores(j, slot):
        k0 = pl.multiple_of(list_sc[j] * KT, KT)
        s_slots[slot][...] = _dot_nt(qs, ksl_ref[0, 0, pl.ds(k0, KT), :])
        pk_slots[slot][...] = _dot(selb, ex_ref[:, pl.ds(k0, KT)])

    def update(j, slot):
        k0 = pl.multiple_of(list_sc[j] * KT, KT)
        kpos = k0 + lax.broadcasted_iota(jnp.int32, (1, KT), 1)
        mask = (pk_slots[slot][...] > 0.5) & (kpos <= t_col)
        ps, alphas = [], []
        for h in range(HG):
            rows = slice(h * TQ, (h + 1) * TQ)
            sh = biased(s_slots[slot], kpos, mask, h)
            m_old = m_sc[rows]
            m_new = jnp.maximum(m_old, jnp.max(sh, axis=-1, keepdims=True))
            ps.append(jnp.exp2((sh - jnp.concatenate([m_new] * (KT // LANES), axis=-1)).astype(BF16)))
            alphas.append(jnp.exp2(m_old - m_new))
            m_sc[rows] = m_new
        pv = _dot(jnp.concatenate(ps, axis=0), vsl_ref[0, 0, pl.ds(k0, KT), :])
        acc_sc[...] = jnp.concatenate(alphas, axis=0) * acc_sc[...] + pv

    def stage(j, slot):
        scores(j + 1, 1 - slot)
        update(j, slot)

    scores(0, 0)
    _two_slot_pipeline(n_act - 1, stage, update)

    start = pl.multiple_of(jnp.maximum(s0 - WINDOW, 0), TQ)
    kpos_w = start + lax.broadcasted_iota(jnp.int32, (1, WK), 1)
    mask_w = (kpos_w <= t_col) & (kpos_w > t_col - WINDOW)
    acc_w = []
    for h0 in range(0, HG, PAIR):
        s_w = _dot_nt(qs[h0 * TQ:(h0 + PAIR) * TQ], kw_ref[0, 0, pl.ds(start, WK), :])
        pw = []
        for h in range(PAIR):
            sh = biased(s_w, kpos_w, mask_w, h, head=h0 + h)
            pw.append(jnp.exp2((sh - jnp.max(sh, axis=-1, keepdims=True)).astype(BF16)))
        acc_w.append(_dot(jnp.concatenate(pw, axis=0), vw_ref[0, 0, pl.ds(start, WK), :]))
    acc_w = jnp.concatenate(acc_w, axis=0)

    o_cmp = oc_ref[...].astype(F32)
    outs = []
    for h in range(HG):
        rows = slice(h * TQ, (h + 1) * TQ)
        a_s, a_w = acc_sc[rows], acc_w[rows]
        r_s, r_w = gt / a_s, gt / a_w
        outs.append(o_cmp[:, h * DH:(h + 1) * DH]
                    + r_s[:, DH + 3 * h + 1:DH + 3 * h + 2] * a_s[:, :DH]
                    + r_w[:, DH + 3 * h + 2:DH + 3 * h + 3] * a_w[:, :DH])
    o_ref[...] = jnp.concatenate(outs, axis=-1).astype(o_ref.dtype)


def _nsa_main(act, q, sel, oc, gates, slope_tab, expand, ksl, vsl, kw, vw, B, T):
    G, HG, DH, TQ = NSA_GROUPS, NSA_HPG, NSA_DH, NSA_Q_TILE
    nq = T // TQ
    n_slc = sel.shape[3]
    n_ktiles = T // NSA_KEY_TILE
    kv = lambda last: pl.BlockSpec((1, 1, T, last), lambda b, g, i, a: (b, g, 0, 0))
    grid_spec = pltpu.PrefetchScalarGridSpec(
        num_scalar_prefetch=1,
        grid=(B, G, nq),
        in_specs=[pl.BlockSpec((1, 1, HG, TQ, DH), lambda b, g, i, a: (b, g, 0, i, 0)),
                  pl.BlockSpec((1, 1, TQ, n_slc), lambda b, g, i, a: (b, g, i, 0)),
                  pl.BlockSpec((TQ, HG * DH), lambda b, g, i, a: (b * nq + i, g)),
                  pl.BlockSpec((1, 1, TQ, LANES), lambda b, g, i, a: (b, g, i, 0)),
                  pl.BlockSpec((1, HG * TQ, LANES), lambda b, g, i, a: (g, 0, 0)),
                  pl.BlockSpec((n_slc, T), lambda b, g, i, a: (0, 0)),
                  kv(DH), kv(LANES), kv(DH), kv(LANES)],
        out_specs=pl.BlockSpec((TQ, HG * DH), lambda b, g, i, a: (b * nq + i, g)),
        scratch_shapes=[pltpu.VMEM((HG * TQ, LANES), F32), pltpu.VMEM((HG * TQ, LANES), F32),
                        pltpu.VMEM((HG * TQ, NSA_KEY_TILE), F32), pltpu.VMEM((HG * TQ, NSA_KEY_TILE), F32),
                        pltpu.VMEM((TQ, NSA_KEY_TILE), F32), pltpu.VMEM((TQ, NSA_KEY_TILE), F32),
                        pltpu.SMEM((n_ktiles + 1,), jnp.int32)],
    )
    return pl.pallas_call(
        functools.partial(_nsa_main_kernel, n_ktiles=n_ktiles),
        grid_spec=grid_spec,
        out_shape=jax.ShapeDtypeStruct((B * T, NSA_HEADS * DH), BF16),
        compiler_params=_cparams("parallel", "parallel", "arbitrary"),
        name="nsa_select_window",
    )(act, q, sel, oc, gates, slope_tab, expand, ksl, vsl, kw, vw)


def _nsa_mixer(x, B, T, attn_norm, w_in, q_norm, kcmp_norm, kslc_norm, kwin_norm, pos_k, pos_v,
               k_w1, k_b1, k_w2, v_w1, v_b1, v_w2):
    G, HG, DH, TQ = NSA_GROUPS, NSA_HPG, NSA_DH, NSA_Q_TILE
    q, kc, vc, ksl, vsl, kw, vw, gates = _nsa_in(x, attn_norm, w_in, q_norm, kslc_norm, kwin_norm, B, T)

    nrow = T // CMP_STRIDE
    half = CMP_STRIDE * DH
    kcmp, vcmp = _compress(kc.reshape(B, G, nrow, half), vc.reshape(B, G, nrow, half),
                           pos_k, pos_v, k_w1, k_b1, k_w2, v_w1, v_b1, v_w2, kcmp_norm)

    n_slc = T // SLC_BLOCK
    n_top = min(SLC_TOP, n_slc)
    n_slc_pad = max(n_slc, LANES)
    slopes = 2.0 ** (-8.0 * jnp.arange(1, NSA_HEADS + 1, dtype=F32) / NSA_HEADS) * LOG2E
    slope_tab = jnp.broadcast_to(slopes.reshape(G, HG, 1, 1), (G, HG, TQ, LANES)).reshape(G, HG * TQ, LANES)
    cj = jnp.arange(nrow)[:, None] * CMP_STRIDE
    si = jnp.arange(n_slc_pad)[None, :] * SLC_BLOCK
    overlap = ((cj <= si + SLC_BLOCK - 1) & (cj + CMP_BLOCK - 1 >= si)
               & (jnp.arange(nrow)[:, None] < nrow - 1)).astype(BF16)
    blocks_per_tile = NSA_KEY_TILE // SLC_BLOCK
    group_mat = (jnp.arange(n_slc_pad)[:, None] // blocks_per_tile
                 == jnp.arange(LANES)[None, :]).astype(BF16)
    expand = (jnp.arange(n_slc_pad)[:, None] == jnp.arange(T)[None, :] // SLC_BLOCK).astype(BF16)

    oc, sel, act = _nsa_cmp(q, kcmp, vcmp, gates, slope_tab, overlap, group_mat, B, T, n_top)
    n_ktiles = T // NSA_KEY_TILE
    act_i = act[:, :, :, 0, :n_ktiles].astype(jnp.int32).reshape(-1)
    return _nsa_main(act_i, q, sel, oc, gates, slope_tab, expand, ksl, vsl, kw, vw, B, T)


def _rope_layout(w):
    half = QK_ROPE // 2
    z = jnp.zeros(w.shape[:-1] + (LANES // 2 - half,), w.dtype)
    return jnp.concatenate([w[..., :half], z, w[..., half:], z], axis=-1)


def _head_layout(w):
    w = w.reshape(w.shape[:-1] + (MLA_HEADS, MLA_QK))
    w = jnp.concatenate([w[..., :QK_NOPE], _rope_layout(w[..., QK_NOPE:])], axis=-1)
    return w.reshape(w.shape[:-2] + (MLA_HEADS * MLA_QK_PAD,))


def _roped(rope, gr_ref, cos_ref, sin_ref):
    r = rope * gr_ref[...]
    return r * cos_ref[...] + pltpu.roll(r, LANES // 2, 1) * sin_ref[...]


def _head_inv_rms(nope, rope_sq):
    ss = jnp.sum(nope * nope + rope_sq, axis=-1, keepdims=True)
    return lax.rsqrt(ss * (1.0 / MLA_QK) + NORM_EPS)


def _mla_q_kernel(x_ref, an_ref, wa_ref, qan_ref, wb_ref, gn_ref, gr_ref, cos_ref, sin_ref, o_ref):
    qa = _dot(_rms(x_ref[...], an_ref[...]).astype(BF16), wa_ref[...])
    q = _dot(_rms(qa, qan_ref[...]).astype(BF16), wb_ref[...])
    for h in range(MLA_HEADS):
        nope = q[:, h * MLA_QK_PAD:h * MLA_QK_PAD + QK_NOPE]
        rope = q[:, h * MLA_QK_PAD + QK_NOPE:(h + 1) * MLA_QK_PAD]
        inv = _head_inv_rms(nope, rope * rope)
        o_ref[0, h] = jnp.concatenate([nope * inv * gn_ref[...], _roped(rope * inv, gr_ref, cos_ref, sin_ref)],
                                      axis=-1).astype(BF16)


def _mla_kv_kernel(x_ref, n_ref, wa_ref, cn_ref, wb_ref, gn_ref, gr_ref, cos_ref, sin_ref, k_ref, v_ref):
    kv_a = _dot(_rms(x_ref[...], n_ref[...]).astype(BF16), wa_ref[...])
    kv = _dot(_rms(kv_a[:, :KV_LORA], cn_ref[...]).astype(BF16), wb_ref[...])
    rope = kv_a[:, KV_LORA:]
    rope_sq = rope * rope
    roped = _roped(rope, gr_ref, cos_ref, sin_ref)
    for h in range(MLA_HEADS):
        base = h * (QK_NOPE + MLA_V)
        nope = kv[:, base:base + QK_NOPE]
        inv = _head_inv_rms(nope, rope_sq)
        k_ref[0, h] = jnp.concatenate([nope * inv * gn_ref[...], inv * roped], axis=-1).astype(BF16)
        v = kv[:, base + QK_NOPE:base + QK_NOPE + MLA_V]
        v_ref[0, h] = jnp.concatenate([v, jnp.ones_like(v)], axis=-1).astype(BF16)


def _mla_specs(tm, nt, d, weights):
    full = lambda shape: pl.BlockSpec(shape, lambda i: (0, 0))
    row = lambda: pl.BlockSpec((1, LANES), lambda i: (0, 0))
    tab = lambda: pl.BlockSpec((tm, LANES), lambda i: (i % nt, 0))
    return ([pl.BlockSpec((tm, d), lambda i: (i, 0))] + [full(w.shape) for w in weights]
            + [row(), row(), tab(), tab()])


def _mla_q(x, attn_norm, w_q_a, q_a_norm, w_q_b, gain, cos_l, sin_l, B, T, *, tm=512):
    nt = T // tm
    d = x.shape[1]
    weights = (attn_norm.reshape(1, d), w_q_a, q_a_norm.reshape(1, -1), w_q_b)
    gain = gain * (MLA_QK ** -0.5 * LOG2E)
    return pl.pallas_call(
        _mla_q_kernel,
        grid=(B * nt,),
        in_specs=_mla_specs(tm, nt, d, weights),
        out_specs=pl.BlockSpec((1, MLA_HEADS, tm, MLA_QK_PAD), lambda i: (i // nt, 0, i % nt, 0)),
        out_shape=jax.ShapeDtypeStruct((B, MLA_HEADS, T, MLA_QK_PAD), BF16),
        compiler_params=_cparams("parallel"),
        name="mla_q_path",
    )(x, *weights, gain[:QK_NOPE].reshape(1, LANES), _rope_layout(gain[QK_NOPE:]).reshape(1, LANES),
      cos_l, sin_l)


def _mla_kv(x, kv_norm, w_kv_a, kv_c_norm, w_kv_b, gain, cos_l, sin_l, B, T, *, tm=512):
    nt = T // tm
    d = x.shape[1]
    weights = (kv_norm.reshape(1, d), w_kv_a, kv_c_norm.reshape(1, -1), w_kv_b)
    return pl.pallas_call(
        _mla_kv_kernel,
        grid=(B * nt,),
        in_specs=_mla_specs(tm, nt, d, weights),
        out_specs=[pl.BlockSpec((1, MLA_HEADS, tm, MLA_QK_PAD), lambda i: (i // nt, 0, i % nt, 0)),
                   pl.BlockSpec((1, MLA_HEADS, tm, 2 * MLA_V), lambda i: (i // nt, 0, i % nt, 0))],
        out_shape=[jax.ShapeDtypeStruct((B, MLA_HEADS, T, MLA_QK_PAD), BF16),
                   jax.ShapeDtypeStruct((B, MLA_HEADS, T, 2 * MLA_V), BF16)],
        compiler_params=_cparams("parallel"),
        name="mla_kv_path",
    )(x, *weights, gain[:QK_NOPE].reshape(1, LANES), _rope_layout(gain[QK_NOPE:]).reshape(1, LANES),
      cos_l, sin_l)


def _two_slot_pipeline(last, stage, finish):
    def pair(j, carry):
        stage(2 * j, 0)
        stage(2 * j + 1, 1)
        return carry

    lax.fori_loop(0, last // 2, pair, 0)

    @pl.when(last % 2 == 1)
    def _():
        stage(last - 1, 0)
        finish(last, 1)

    @pl.when(last % 2 == 0)
    def _():
        finish(last, 0)


def _mla_attn_kernel(q_ref, k_ref, v_ref, o_ref, m_sc, acc_sc, s0_sc, s1_sc):
    TQ = MLA_TILE
    qi = pl.program_id(2)
    q = q_ref[0, 0]
    s_slots = (s0_sc, s1_sc)
    m_sc[...] = jnp.full_like(m_sc, NEG_INF)
    acc_sc[...] = jnp.zeros_like(acc_sc)

    def scores(i, slot):
        k0 = pl.multiple_of(i * TQ, TQ)
        s_slots[slot][...] = _dot_nt(q, k_ref[0, 0, pl.ds(k0, TQ), :])

    def update(i, slot, masked):
        k0 = pl.multiple_of(i * TQ, TQ)

        def load():
            s = s_slots[slot][...]
            if masked:
                qq = lax.broadcasted_iota(jnp.int32, (TQ, TQ), 0)
                kk = lax.broadcasted_iota(jnp.int32, (TQ, TQ), 1)
                s = jnp.where(kk <= qq, s, NEG_INF)
            return s

        m_old = m_sc[...]
        m_new = jnp.maximum(m_old, jnp.max(load(), axis=-1, keepdims=True))
        p = jnp.exp2((load() - jnp.concatenate([m_new] * (TQ // LANES), axis=-1)).astype(BF16))
        alpha = jnp.exp2(m_old - m_new)
        acc_sc[...] = (jnp.concatenate([alpha] * (acc_sc.shape[1] // LANES), axis=-1) * acc_sc[...]
                       + _dot(p, v_ref[0, 0, pl.ds(k0, TQ), :]))
        m_sc[...] = m_new

    def stage(i, slot):
        scores(i + 1, 1 - slot)
        update(i, slot, False)

    scores(0, 0)
    _two_slot_pipeline(qi, stage, lambda i, slot: update(i, slot, True))
    o_ref[...] = (acc_sc[:, :MLA_V] / acc_sc[:, MLA_V:]).astype(o_ref.dtype)


def _mla_attn(q, k, v, B, T):
    H, TQ = MLA_HEADS, MLA_TILE
    nq = T // TQ
    return pl.pallas_call(
        _mla_attn_kernel,
        grid=(B, H, nq),
        in_specs=[pl.BlockSpec((1, 1, TQ, MLA_QK_PAD), lambda b, h, i: (b, h, i, 0)),
                  pl.BlockSpec((1, 1, T, MLA_QK_PAD), lambda b, h, i: (b, h, 0, 0)),
                  pl.BlockSpec((1, 1, T, 2 * MLA_V), lambda b, h, i: (b, h, 0, 0))],
        out_specs=pl.BlockSpec((TQ, MLA_V), lambda b, h, i: (b * nq + i, h)),
        out_shape=jax.ShapeDtypeStruct((B * T, H * MLA_V), BF16),
        scratch_shapes=[pltpu.VMEM((TQ, LANES), F32), pltpu.VMEM((TQ, 2 * MLA_V), F32),
                        pltpu.VMEM((TQ, TQ), F32), pltpu.VMEM((TQ, TQ), F32)],
        compiler_params=_cparams("parallel", "parallel", "arbitrary"),
        name="mla_flash_attn",
    )(q, k, v)


def _pad_cols(w, mult=LANES):
    pad = -w.shape[1] % mult
    return jnp.pad(w, ((0, 0), (0, pad))) if pad else w


def kernel(x, a_attn_norm, a_w_in, a_q_norm, a_kcmp_norm, a_kslc_norm, a_kwin_norm, a_cmp_pos_k, a_cmp_pos_v, a_cmp_k_w1, a_cmp_k_b1, a_cmp_k_w2, a_cmp_v_w1, a_cmp_v_b1, a_cmp_v_w2, a_w_out, kv_norm, kv_w_a, kv_c_norm, kv_w_b, kv_k_norm, b_attn_norm, b_w_q_a, b_q_a_norm, b_w_q_b, b_q_norm, b_w_out, ffn_norm, ffn_w_gate_up, ffn_w_down):
    B, T, D = x.shape
    n_a = a_w_in.shape[0]
    n_b = b_w_q_a.shape[0]
    xs = x.reshape(B * T, D)

    inv = ROPE_THETA ** (-jnp.arange(0, QK_ROPE, 2, dtype=F32) / QK_ROPE)
    ang = jnp.arange(T, dtype=F32)[:, None] * inv[None, :]
    cos, sin = jnp.cos(ang), jnp.sin(ang)
    cos2 = _rope_layout(jnp.concatenate([cos, cos], axis=-1))
    sin2 = _rope_layout(jnp.concatenate([-sin, sin], axis=-1))

    k_shared = v_shared = None
    for layer in range(n_a + n_b):
        if layer < n_a:
            i = layer
            o = _nsa_mixer(xs, B, T, a_attn_norm[i], _pad_cols(a_w_in[i]).astype(BF16),
                           a_q_norm[i], a_kcmp_norm[i], a_kslc_norm[i], a_kwin_norm[i],
                           a_cmp_pos_k[i], a_cmp_pos_v[i], a_cmp_k_w1[i], a_cmp_k_b1[i], a_cmp_k_w2[i],
                           a_cmp_v_w1[i], a_cmp_v_b1[i], a_cmp_v_w2[i])
            w_out = a_w_out[i]
        else:
            j = layer - n_a
            q = _mla_q(xs, b_attn_norm[j], b_w_q_a[j].astype(BF16), b_q_a_norm[j],
                       _head_layout(b_w_q_b[j]).astype(BF16), b_q_norm[j], cos2, sin2, B, T)
            o = _mla_attn(q, k_shared, v_shared, B, T)
            w_out = b_w_out[j]
        xs = _out_ffn(xs, o, w_out.astype(BF16), ffn_norm[layer],
                      ffn_w_gate_up[layer].astype(BF16), ffn_w_down[layer].astype(BF16))
        if layer == n_a - 1:
            w_kv_a = jnp.concatenate([kv_w_a[:, :KV_LORA], _rope_layout(kv_w_a[:, KV_LORA:])], axis=-1)
            k_shared, v_shared = _mla_kv(xs, kv_norm, w_kv_a.astype(BF16), kv_c_norm, kv_w_b.astype(BF16),
                                         kv_k_norm, cos2, sin2, B, T)
    return xs.reshape(B, T, D)
```

```python
import functools

import jax
import jax.numpy as jnp
from jax import lax
from jax.experimental import pallas as pl
from jax.experimental.pallas import tpu as pltpu

F32 = jnp.float32
BF16 = jnp.bfloat16

NORM_EPS = 1e-6
NEG_INF = -1e30
LANES = 128
SUBLANES = 8

NSA_HEADS = 16
NSA_GROUPS = 4
NSA_HPG = NSA_HEADS // NSA_GROUPS
NSA_DH = 64
CMP_BLOCK = 32
CMP_STRIDE = 16
CMP_HIDDEN = 256
SLC_BLOCK = 64
SLC_TOP = 16
WINDOW = 512
FORCE_SCORE = 1e4
NSA_Q_TILE = 256
NSA_CMP_TILE = 1024
NSA_KEY_TILE = 256
PAIR = 2
LOG2E = 1.4426950408889634

MLA_HEADS = 8
QK_NOPE = 128
QK_ROPE = 64
MLA_QK = QK_NOPE + QK_ROPE
MLA_QK_PAD = QK_NOPE + LANES
MLA_V = 128
Q_LORA = 384
KV_LORA = 256
ROPE_THETA = 10000.0
MLA_TILE = 512

VMEM_LIMIT = 56 * 1024 * 1024


def _cparams(*sem):
    return pltpu.CompilerParams(dimension_semantics=sem, vmem_limit_bytes=VMEM_LIMIT)


def _rms(x, g):
    return x * lax.rsqrt(jnp.mean(x * x, axis=-1, keepdims=True) + NORM_EPS) * g


def _dot(a, b):
    return jnp.dot(a, b, preferred_element_type=F32)


def _dot_nt(a, b):
    return lax.dot_general(a, b, (((1,), (1,)), ((), ())), preferred_element_type=F32)


def _out_ffn_kernel(x_ref, a_ref, wo_ref, g_ref, wgu_ref, wd_ref, o_ref):
    hid = wd_ref.shape[0]
    x = x_ref[...] + _dot(a_ref[...], wo_ref[...])
    h = _rms(x, g_ref[...]).astype(BF16)
    gate = _dot(h, wgu_ref[:, :hid])
    up = _dot(h, wgu_ref[:, hid:])
    a = (gate * jax.nn.sigmoid(gate) * up).astype(BF16)
    o_ref[...] = x + _dot(a, wd_ref[...])


def _out_ffn(x, mixed, w_out, gain, w_gate_up, w_down, *, tm=512):
    n, d = x.shape
    hid = w_down.shape[0]
    dm = mixed.shape[1]
    assert n % tm == 0 and hid % LANES == 0
    resident = lambda shape: pl.BlockSpec(shape, lambda i: (0, 0), pipeline_mode=pl.Buffered(1))
    return pl.pallas_call(
        _out_ffn_kernel,
        grid=(n // tm,),
        in_specs=[pl.BlockSpec((tm, d), lambda i: (i, 0)),
                  pl.BlockSpec((tm, dm), lambda i: (i, 0)),
                  resident((dm, d)),
                  pl.BlockSpec((1, d), lambda i: (0, 0)),
                  resident((d, 2 * hid)), resident((hid, d))],
        out_specs=pl.BlockSpec((tm, d), lambda i: (i, 0)),
        out_shape=jax.ShapeDtypeStruct((n, d), F32),
        compiler_params=_cparams("parallel"),
        name="out_proj_swiglu_ffn",
    )(x, mixed, w_out, gain.reshape(1, d), w_gate_up, w_down)


def _split2(x):
    hi = x.astype(BF16)
    return hi, (x - hi.astype(F32)).astype(BF16)


def _nsa_in_kernel(x_ref, an_ref, w_ref, gain_ref, gsum_ref, gexp_ref,
                   q_ref, kc_ref, vc_ref, ksl_ref, vsl_ref, kw_ref, vw_ref, gt_ref):
    G, HG, DH = NSA_GROUPS, NSA_HPG, NSA_DH
    qw, kvw = NSA_HEADS * DH, G * DH
    tm = x_ref.shape[0]
    p = _dot(_rms(x_ref[...], an_ref[...]).astype(BF16), w_ref[...])
    ones_col = jnp.ones((tm, LANES - DH), F32)
    n_gate = HG * 3
    gate_pad = jnp.zeros((tm, LANES - DH - n_gate), F32)

    def piece(base, g):
        return p[:, base + g * DH: base + (g + 1) * DH]

    xn = jnp.concatenate([p[:, :qw], p[:, qw + 2 * kvw:qw + 3 * kvw], p[:, qw + 4 * kvw:qw + 5 * kvw]], axis=-1)
    hi, lo = _split2(xn * xn)
    ss = _dot(hi, gsum_ref[...]) + _dot(lo, gsum_ref[...])
    hi, lo = _split2(lax.rsqrt(ss * (1.0 / DH) + NORM_EPS))
    xn = xn * (_dot(hi, gexp_ref[...]) + _dot(lo, gexp_ref[...])) * gain_ref[...]

    for g in range(G):
        for h in range(HG):
            q_ref[0, g, h] = xn[:, (g * HG + h) * DH:(g * HG + h + 1) * DH].astype(BF16)
        kc_ref[0, g] = piece(qw, g).astype(BF16)
        vc_ref[0, g] = piece(qw + kvw, g).astype(BF16)
        ksl_ref[0, g] = xn[:, qw + g * DH:qw + (g + 1) * DH].astype(BF16)
        vsl_ref[0, g] = jnp.concatenate([piece(qw + 3 * kvw, g), ones_col], axis=-1).astype(BF16)
        kw_ref[0, g] = xn[:, qw + kvw + g * DH:qw + kvw + (g + 1) * DH].astype(BF16)
        vw_ref[0, g] = jnp.concatenate([piece(qw + 5 * kvw, g), ones_col], axis=-1).astype(BF16)
        gb = qw + 6 * kvw + g * n_gate
        gt_ref[0, g] = jnp.concatenate([jnp.zeros((tm, DH), F32), jax.nn.sigmoid(p[:, gb:gb + n_gate]),
                                        gate_pad], axis=-1)


def _nsa_in(x, attn_norm, w_in, q_norm, kslc_norm, kwin_norm, B, T, *, tm=512):
    G, HG, DH = NSA_GROUPS, NSA_HPG, NSA_DH
    nt = T // tm
    d, width = w_in.shape
    n_norm = (NSA_HEADS + 2 * G) * DH
    gain = jnp.concatenate([jnp.tile(q_norm, NSA_HEADS) * (DH ** -0.5 * LOG2E),
                            jnp.tile(kslc_norm, G), jnp.tile(kwin_norm, G)]).reshape(1, n_norm)
    gsum = (jnp.arange(n_norm)[:, None] // DH == jnp.arange(LANES)[None, :]).astype(BF16)
    full = lambda shape: pl.BlockSpec(shape, lambda i: (0, 0))
    per_group = lambda last: pl.BlockSpec((1, G, tm, last), lambda i: (i // nt, 0, i % nt, 0))
    shp = lambda last, dt: jax.ShapeDtypeStruct((B, G, T, last), dt)
    return pl.pallas_call(
        _nsa_in_kernel,
        grid=(B * nt,),
        in_specs=[pl.BlockSpec((tm, d), lambda i: (i, 0)), full((1, d)), full((d, width)),
                  full((1, n_norm)), full((n_norm, LANES)), full((LANES, n_norm))],
        out_specs=[pl.BlockSpec((1, G, HG, tm, DH), lambda i: (i // nt, 0, 0, i % nt, 0)),
                   per_group(DH), per_group(DH), per_group(DH), per_group(LANES),
                   per_group(DH), per_group(LANES), per_group(LANES)],
        out_shape=[jax.ShapeDtypeStruct((B, G, HG, T, DH), BF16),
                   shp(DH, BF16), shp(DH, BF16), shp(DH, BF16), shp(LANES, BF16),
                   shp(DH, BF16), shp(LANES, BF16), shp(LANES, F32)],
        compiler_params=_cparams("parallel"),
        name="nsa_in_proj_split",
    )(x, attn_norm.reshape(1, d), w_in, gain, gsum, gsum.T)


def _compress_kernel(rk_ref, rv_ref, pek_ref, pev_ref, kw1_ref, kb1_ref, kw2_ref,
                     vw1_ref, vb1_ref, vw2_ref, kn_ref, kcmp_ref, vcmp_ref):
    half = CMP_STRIDE * NSA_DH
    nrow = rk_ref.shape[2]

    def mlp(r_ref, pe_ref, w1_ref, b1_ref, w2_ref):
        r = r_ref[0, 0].astype(BF16)
        ya = _dot(r, w1_ref[:half, :])
        yb = _dot(r, w1_ref[half:, :])
        pe = jnp.broadcast_to(pe_ref[...], (SUBLANES, 2 * half)).astype(BF16)
        c = _dot(pe, w1_ref[...])[0:1] + b1_ref[...]
        hid = ya + pltpu.roll(yb, nrow - 1, 0) + c
        return _dot(jax.nn.gelu(hid).astype(BF16), w2_ref[...])

    kcmp_ref[0, 0] = _rms(mlp(rk_ref, pek_ref, kw1_ref, kb1_ref, kw2_ref), kn_ref[...]).astype(BF16)
    vcmp_ref[0, 0] = mlp(rv_ref, pev_ref, vw1_ref, vb1_ref, vw2_ref).astype(BF16)


def _compress(rk, rv, pe_k, pe_v, k_w1, k_b1, k_w2, v_w1, v_b1, v_w2, kcmp_norm):
    B, G, nrow, half = rk.shape
    DH, HID = NSA_DH, CMP_HIDDEN
    full = lambda shape: pl.BlockSpec(shape, lambda b, g: (0,) * len(shape))
    r_spec = pl.BlockSpec((1, 1, nrow, half), lambda b, g: (b, g, 0, 0))
    o_spec = pl.BlockSpec((1, 1, nrow, DH), lambda b, g: (b, g, 0, 0))
    return pl.pallas_call(
        _compress_kernel,
        grid=(B, G),
        in_specs=[r_spec, r_spec, full((1, 2 * half)), full((1, 2 * half)),
                  full((2 * half, HID)), full((1, HID)), full((HID, DH)),
                  full((2 * half, HID)), full((1, HID)), full((HID, DH)), full((1, DH))],
        out_specs=[o_spec, o_spec],
        out_shape=[jax.ShapeDtypeStruct((B, G, nrow, DH), BF16)] * 2,
        compiler_params=_cparams("parallel", "parallel"),
        name="nsa_compress",
    )(rk, rv, pe_k.reshape(1, -1), pe_v.reshape(1, -1),
      k_w1.astype(BF16), k_b1.reshape(1, HID), k_w2.astype(BF16),
      v_w1.astype(BF16), v_b1.reshape(1, HID), v_w2.astype(BF16), kcmp_norm.reshape(1, DH))


def _split3(x):
    hi = x.astype(BF16)
    r1 = x - hi.astype(F32)
    mid = r1.astype(BF16)
    lo = (r1 - mid.astype(F32)).astype(BF16)
    return hi, mid, lo


def _nsa_cmp_kernel(q_ref, kc_ref, vc_ref, gt_ref, sl_ref, ov_ref, gm_ref,
                    oc_ref, sel_ref, act_ref, *, n_top):
    HG, DH, TQ, TM = NSA_HPG, NSA_DH, NSA_CMP_TILE, NSA_Q_TILE
    s0 = pl.program_id(2) * TQ
    ncmp = kc_ref.shape[2]
    n_slc = ov_ref.shape[1]
    qs = q_ref[0, 0].reshape(HG * TQ, DH)
    gt = gt_ref[0, 0]

    def attend(nc):
        j = lax.broadcasted_iota(jnp.int32, (1, nc), 1)
        t = s0 + lax.broadcasted_iota(jnp.int32, (TQ, 1), 0)
        mask = j * CMP_STRIDE + (CMP_BLOCK - 1) <= t
        mid = (j * CMP_STRIDE - s0).astype(F32) + 0.5 * (CMP_BLOCK - 1)
        has_key = t >= CMP_BLOCK - 1
        s = _dot_nt(qs, kc_ref[0, 0, :nc, :])
        psum = jnp.zeros((TQ, nc), F32)
        ps = []
        for h in range(HG):
            rows = slice(h * TQ, (h + 1) * TQ)
            slope_row = jnp.concatenate([sl_ref[0, h * TM:h * TM + 1, :]] * (nc // LANES), axis=-1)
            sh = jnp.where(mask, s[rows] + slope_row * mid, NEG_INF)
            m = jnp.max(sh, axis=-1, keepdims=True)
            e = jnp.exp2(sh - m)
            l = jnp.sum(e, axis=-1, keepdims=True)
            inv = jnp.where(has_key, 1.0 / l, 0.0)
            p = e * inv
            psum = psum + p
            ps.append(p.astype(BF16))
        o_all = _dot(jnp.concatenate(ps, axis=0), vc_ref[0, 0, :nc, :])
        oc_ref[...] = jnp.concatenate(
            [o_all[h * TQ:(h + 1) * TQ] * gt[:, DH + 3 * h:DH + 3 * h + 1] for h in range(HG)],
            axis=-1).astype(oc_ref.dtype)
        hi, md, lo = _split3(psum)
        ov = ov_ref[:nc, :]
        imp = _dot(hi, ov) + _dot(md, ov) + _dot(lo, ov)
        select(imp, nc * CMP_STRIDE // SLC_BLOCK)

    def select(imp, nb):
        imp_t = imp.T[:nb]
        blk = lax.broadcasted_iota(jnp.int32, (nb, TQ), 0)
        tq = s0 + lax.broadcasted_iota(jnp.int32, (nb, TQ), 1)
        cur = lax.shift_right_logical(tq, SLC_BLOCK.bit_length() - 1)
        forced = (blk == 0) | (blk == cur) | (blk == cur - 1)
        taken = -2.0
        score = jnp.where(forced, taken, jnp.where(blk * SLC_BLOCK <= tq, imp_t, -1.0))

        def pick(_, score):
            mx = jnp.max(score, axis=0, keepdims=True)
            first = jnp.min(jnp.where(score == mx, blk, nb), axis=0, keepdims=True)
            return jnp.where(blk == first, taken, score)

        score = lax.fori_loop(0, n_top - 3, pick, score, unroll=True)
        sel_t = jnp.where(score == taken, 1.0, 0.0)
        if nb < n_slc:
            sel_t = jnp.concatenate([sel_t, jnp.zeros((n_slc - nb, TQ), F32)], axis=0)
        sel = sel_t.T.astype(BF16)
        sel_ref[0, 0] = sel
        for sub in range(TQ // TM):
            cnt = _dot(jnp.ones((SUBLANES, TM), BF16), sel[sub * TM:(sub + 1) * TM])
            act_ref[0, 0, sub] = _dot((cnt > 0.0).astype(BF16), gm_ref[...])

    need = (s0 + TQ) // CMP_STRIDE
    widths = list(range(LANES, ncmp + 1, LANES))
    for idx, nc in enumerate(widths):
        lo_w = widths[idx - 1] if idx else 0
        pl.when((need > lo_w) & (need <= nc))(functools.partial(attend, nc))


def _nsa_cmp(q, kcmp, vcmp, gates, slope_tab, overlap, group_mat, B, T, n_top):
    G, HG, DH, TQ, TM = NSA_GROUPS, NSA_HPG, NSA_DH, NSA_CMP_TILE, NSA_Q_TILE
    nq = T // TQ
    ncmp = kcmp.shape[2]
    n_slc = overlap.shape[1]
    assert n_top >= 3 and ncmp % LANES == 0 and HG < FORCE_SCORE
    return pl.pallas_call(
        functools.partial(_nsa_cmp_kernel, n_top=n_top),
        grid=(B, G, nq),
        in_specs=[pl.BlockSpec((1, 1, HG, TQ, DH), lambda b, g, i: (b, g, 0, i, 0)),
                  pl.BlockSpec((1, 1, ncmp, DH), lambda b, g, i: (b, g, 0, 0)),
                  pl.BlockSpec((1, 1, ncmp, DH), lambda b, g, i: (b, g, 0, 0)),
                  pl.BlockSpec((1, 1, TQ, LANES), lambda b, g, i: (b, g, i, 0)),
                  pl.BlockSpec((1, HG * TM, LANES), lambda b, g, i: (g, 0, 0)),
                  pl.BlockSpec((ncmp, n_slc), lambda b, g, i: (0, 0)),
                  pl.BlockSpec((n_slc, LANES), lambda b, g, i: (0, 0))],
        out_specs=[pl.BlockSpec((TQ, HG * DH), lambda b, g, i: (b * nq + i, g)),
                   pl.BlockSpec((1, 1, TQ, n_slc), lambda b, g, i: (b, g, i, 0)),
                   pl.BlockSpec((1, 1, TQ // TM, SUBLANES, LANES), lambda b, g, i: (b, g, i, 0, 0))],
        out_shape=[jax.ShapeDtypeStruct((B * T, NSA_HEADS * DH), BF16),
                   jax.ShapeDtypeStruct((B, G, T, n_slc), BF16),
                   jax.ShapeDtypeStruct((B, G, T // TM, SUBLANES, LANES), F32)],
        compiler_params=_cparams("parallel", "parallel", "arbitrary"),
        name="nsa_cmp_select",
    )(q, kcmp, vcmp, gates, slope_tab, overlap, group_mat)


def _nsa_main_kernel(list_ref, cnt_ref, q_ref, sel_ref, oc_ref, gt_ref, sl_ref, ex_ref,
                     ksl_ref, vsl_ref, kw_ref, vw_ref, o_ref,
                     m_sc, acc_sc, s0_sc, s1_sc, pk0_sc, pk1_sc, *, n_ktiles):
    HG, DH, TQ, KT = NSA_HPG, NSA_DH, NSA_Q_TILE, NSA_KEY_TILE
    s_slots, pk_slots = (s0_sc, s1_sc), (pk0_sc, pk1_sc)
    WK = WINDOW + TQ
    b, g, qi = pl.program_id(0), pl.program_id(1), pl.program_id(2)
    nq = pl.num_programs(2)
    s0 = qi * TQ
    qs = q_ref[0, 0].reshape(HG * TQ, DH)
    selb = sel_ref[0, 0]
    gt = gt_ref[0, 0]
    t_col = s0 + lax.broadcasted_iota(jnp.int32, (TQ, 1), 0)

    def biased(s, kpos, mask, blk, head=None):
        head = blk if head is None else head
        width = kpos.shape[1]
        slope_row = jnp.concatenate([sl_ref[0, head * TQ:head * TQ + 1, :]] * (width // LANES), axis=-1)
        sh = s[blk * TQ:(blk + 1) * TQ] + slope_row * (kpos - s0).astype(F32)
        return jnp.where(mask, sh, NEG_INF)

    step = (b * NSA_GROUPS + g) * nq + qi
    last = cnt_ref[step] - 1

    def key_start(j):
        return pl.multiple_of(list_ref[step * n_ktiles + j] * KT, KT)

    def scores(j, slot):
        k0 = key_start(j)
        s_slots[slot][...] = _dot_nt(qs, ksl_ref[0, 0, pl.ds(k0, KT), :])
        pk_slots[slot][...] = _dot(selb, ex_ref[:, pl.ds(k0, KT)])

    def update(j, slot, first=False):
        k0 = key_start(j)
        kpos = k0 + lax.broadcasted_iota(jnp.int32, (1, KT), 1)
        mask = (pk_slots[slot][...] > 0.5) & (kpos <= t_col)
        ps, alphas = [], []
        for h in range(HG):
            rows = slice(h * TQ, (h + 1) * TQ)
            sh = biased(s_slots[slot], kpos, mask, h)
            m_new = jnp.broadcast_to(jnp.max(sh, axis=-1, keepdims=True), (TQ, LANES))
            if not first:
                m_old = m_sc[rows]
                m_new = jnp.maximum(m_old, m_new)
                alphas.append(jnp.exp2(m_old - m_new))
            ps.append(jnp.exp2((sh - jnp.concatenate([m_new] * (KT // LANES), axis=-1)).astype(BF16)))
            m_sc[rows] = m_new
        pv = _dot(jnp.concatenate(ps, axis=0), vsl_ref[0, 0, pl.ds(k0, KT), :])
        acc_sc[...] = pv if first else jnp.concatenate(alphas, axis=0) * acc_sc[...] + pv

    def stage(j, slot):
        scores(j + 1, 1 - slot)
        update(j, slot)

    scores(0, 0)

    @pl.when(last == 0)
    def _():
        update(0, 0, first=True)

    @pl.when(last > 0)
    def _():
        scores(1, 1)
        update(0, 0, first=True)
        _two_slot_pipeline(1, last, stage, update)

    start = pl.multiple_of(jnp.maximum(s0 - WINDOW, 0), TQ)
    kpos_w = start + lax.broadcasted_iota(jnp.int32, (1, WK), 1)
    mask_w = (kpos_w <= t_col) & (kpos_w > t_col - WINDOW)
    acc_w = []
    for h0 in range(0, HG, PAIR):
        s_w = _dot_nt(qs[h0 * TQ:(h0 + PAIR) * TQ], kw_ref[0, 0, pl.ds(start, WK), :])
        pw = []
        for h in range(PAIR):
            sh = biased(s_w, kpos_w, mask_w, h, head=h0 + h)
            pw.append(jnp.exp2((sh - jnp.max(sh, axis=-1, keepdims=True)).astype(BF16)))
        acc_w.append(_dot(jnp.concatenate(pw, axis=0), vw_ref[0, 0, pl.ds(start, WK), :]))
    acc_w = jnp.concatenate(acc_w, axis=0)

    o_cmp = oc_ref[...].astype(F32)
    outs = []
    for h in range(HG):
        rows = slice(h * TQ, (h + 1) * TQ)
        a_s, a_w = acc_sc[rows], acc_w[rows]
        r_s, r_w = gt / a_s, gt / a_w
        outs.append(o_cmp[:, h * DH:(h + 1) * DH]
                    + r_s[:, DH + 3 * h + 1:DH + 3 * h + 2] * a_s[:, :DH]
                    + r_w[:, DH + 3 * h + 2:DH + 3 * h + 3] * a_w[:, :DH])
    o_ref[...] = jnp.concatenate(outs, axis=-1).astype(o_ref.dtype)


def _nsa_main(tile_list, tile_count, q, sel, oc, gates, slope_tab, expand, ksl, vsl, kw, vw, B, T):
    G, HG, DH, TQ = NSA_GROUPS, NSA_HPG, NSA_DH, NSA_Q_TILE
    nq = T // TQ
    n_slc = sel.shape[3]
    n_ktiles = T // NSA_KEY_TILE
    kv = lambda last: pl.BlockSpec((1, 1, T, last), lambda b, g, i, tl, tc: (b, g, 0, 0))
    grid_spec = pltpu.PrefetchScalarGridSpec(
        num_scalar_prefetch=2,
        grid=(B, G, nq),
        in_specs=[pl.BlockSpec((1, 1, HG, TQ, DH), lambda b, g, i, tl, tc: (b, g, 0, i, 0)),
                  pl.BlockSpec((1, 1, TQ, n_slc), lambda b, g, i, tl, tc: (b, g, i, 0)),
                  pl.BlockSpec((TQ, HG * DH), lambda b, g, i, tl, tc: (b * nq + i, g)),
                  pl.BlockSpec((1, 1, TQ, LANES), lambda b, g, i, tl, tc: (b, g, i, 0)),
                  pl.BlockSpec((1, HG * TQ, LANES), lambda b, g, i, tl, tc: (g, 0, 0)),
                  pl.BlockSpec((n_slc, T), lambda b, g, i, tl, tc: (0, 0)),
                  kv(DH), kv(LANES), kv(DH), kv(LANES)],
        out_specs=pl.BlockSpec((TQ, HG * DH), lambda b, g, i, tl, tc: (b * nq + i, g)),
        scratch_shapes=[pltpu.VMEM((HG * TQ, LANES), F32), pltpu.VMEM((HG * TQ, LANES), F32),
                        pltpu.VMEM((HG * TQ, NSA_KEY_TILE), F32), pltpu.VMEM((HG * TQ, NSA_KEY_TILE), F32),
                        pltpu.VMEM((TQ, NSA_KEY_TILE), F32), pltpu.VMEM((TQ, NSA_KEY_TILE), F32)],
    )
    return pl.pallas_call(
        functools.partial(_nsa_main_kernel, n_ktiles=n_ktiles),
        grid_spec=grid_spec,
        out_shape=jax.ShapeDtypeStruct((B * T, NSA_HEADS * DH), BF16),
        compiler_params=_cparams("parallel", "parallel", "arbitrary"),
        name="nsa_select_window",
    )(tile_list, tile_count, q, sel, oc, gates, slope_tab, expand, ksl, vsl, kw, vw)


def _nsa_mixer(x, B, T, attn_norm, w_in, q_norm, kcmp_norm, kslc_norm, kwin_norm, pos_k, pos_v,
               k_w1, k_b1, k_w2, v_w1, v_b1, v_w2):
    G, HG, DH, TQ = NSA_GROUPS, NSA_HPG, NSA_DH, NSA_Q_TILE
    assert T % NSA_CMP_TILE == 0 and T % MLA_TILE == 0 and T >= WINDOW + TQ and NSA_KEY_TILE == TQ
    q, kc, vc, ksl, vsl, kw, vw, gates = _nsa_in(x, attn_norm, w_in, q_norm, kslc_norm, kwin_norm, B, T)

    nrow = T // CMP_STRIDE
    half = CMP_STRIDE * DH
    kcmp, vcmp = _compress(kc.reshape(B, G, nrow, half), vc.reshape(B, G, nrow, half),
                           pos_k, pos_v, k_w1, k_b1, k_w2, v_w1, v_b1, v_w2, kcmp_norm)

    n_slc = T // SLC_BLOCK
    n_top = min(SLC_TOP, n_slc)
    n_slc_pad = max(n_slc, LANES)
    slopes = 2.0 ** (-8.0 * jnp.arange(1, NSA_HEADS + 1, dtype=F32) / NSA_HEADS) * LOG2E
    slope_tab = jnp.broadcast_to(slopes.reshape(G, HG, 1, 1), (G, HG, TQ, LANES)).reshape(G, HG * TQ, LANES)
    cj = jnp.arange(nrow)[:, None] * CMP_STRIDE
    si = jnp.arange(n_slc_pad)[None, :] * SLC_BLOCK
    overlap = ((cj <= si + SLC_BLOCK - 1) & (cj + CMP_BLOCK - 1 >= si)
               & (jnp.arange(nrow)[:, None] < nrow - 1)).astype(BF16)
    blocks_per_tile = NSA_KEY_TILE // SLC_BLOCK
    group_mat = (jnp.arange(n_slc_pad)[:, None] // blocks_per_tile
                 == jnp.arange(LANES)[None, :]).astype(BF16)
    expand = (jnp.arange(n_slc_pad)[:, None] == jnp.arange(T)[None, :] // SLC_BLOCK).astype(BF16)

    oc, sel, act = _nsa_cmp(q, kcmp, vcmp, gates, slope_tab, overlap, group_mat, B, T, n_top)
    n_ktiles = T // NSA_KEY_TILE
    tile = jnp.arange(n_ktiles)
    reach = tile[None, :] * NSA_KEY_TILE < (jnp.arange(T // TQ)[:, None] + 1) * TQ
    visit = ((act[:, :, :, 0, :n_ktiles] > 0.0) & reach) | (tile == 0)
    slot = jnp.cumsum(visit, axis=-1) - 1
    tile_list = jnp.sum(jnp.where(visit[..., :, None] & (slot[..., :, None] == tile), tile[:, None], 0), axis=-2)
    tile_count = jnp.sum(visit, axis=-1)
    return _nsa_main(tile_list.astype(jnp.int32).reshape(-1), tile_count.astype(jnp.int32).reshape(-1),
                     q, sel, oc, gates, slope_tab, expand, ksl, vsl, kw, vw, B, T)


def _rope_layout(w):
    half = QK_ROPE // 2
    z = jnp.zeros(w.shape[:-1] + (LANES // 2 - half,), w.dtype)
    return jnp.concatenate([w[..., :half], z, w[..., half:], z], axis=-1)


def _head_layout(w):
    w = w.reshape(w.shape[:-1] + (MLA_HEADS, MLA_QK))
    w = jnp.concatenate([w[..., :QK_NOPE], _rope_layout(w[..., QK_NOPE:])], axis=-1)
    return w.reshape(w.shape[:-2] + (MLA_HEADS * MLA_QK_PAD,))


def _roped(rope, gr_ref, cos_ref, sin_ref):
    r = rope * gr_ref[...]
    return r * cos_ref[...] + pltpu.roll(r, LANES // 2, 1) * sin_ref[...]


def _head_inv_rms(nope, rope_sq):
    ss = jnp.sum(nope * nope + rope_sq, axis=-1, keepdims=True)
    return lax.rsqrt(ss * (1.0 / MLA_QK) + NORM_EPS)


def _mla_q_kernel(x_ref, an_ref, wa_ref, qan_ref, wb_ref, gn_ref, gr_ref, cos_ref, sin_ref, o_ref):
    qa = _dot(_rms(x_ref[...], an_ref[...]).astype(BF16), wa_ref[...])
    q = _dot(_rms(qa, qan_ref[...]).astype(BF16), wb_ref[...])
    for h in range(MLA_HEADS):
        nope = q[:, h * MLA_QK_PAD:h * MLA_QK_PAD + QK_NOPE]
        rope = q[:, h * MLA_QK_PAD + QK_NOPE:(h + 1) * MLA_QK_PAD]
        inv = _head_inv_rms(nope, rope * rope)
        o_ref[0, h] = jnp.concatenate([nope * inv * gn_ref[...], _roped(rope * inv, gr_ref, cos_ref, sin_ref)],
                                      axis=-1).astype(BF16)


def _mla_kv_kernel(x_ref, n_ref, wa_ref, cn_ref, wb_ref, gn_ref, gr_ref, cos_ref, sin_ref, k_ref, v_ref):
    kv_a = _dot(_rms(x_ref[...], n_ref[...]).astype(BF16), wa_ref[...])
    kv = _dot(_rms(kv_a[:, :KV_LORA], cn_ref[...]).astype(BF16), wb_ref[...])
    rope = kv_a[:, KV_LORA:]
    rope_sq = rope * rope
    roped = _roped(rope, gr_ref, cos_ref, sin_ref)
    for h in range(MLA_HEADS):
        base = h * (QK_NOPE + MLA_V)
        nope = kv[:, base:base + QK_NOPE]
        inv = _head_inv_rms(nope, rope_sq)
        k_ref[0, h] = jnp.concatenate([nope * inv * gn_ref[...], inv * roped], axis=-1).astype(BF16)
        v = kv[:, base + QK_NOPE:base + QK_NOPE + MLA_V]
        v_ref[0, h] = jnp.concatenate([v, jnp.ones_like(v)], axis=-1).astype(BF16)


def _mla_specs(tm, nt, d, weights):
    full = lambda shape: pl.BlockSpec(shape, lambda i: (0, 0))
    row = lambda: pl.BlockSpec((1, LANES), lambda i: (0, 0))
    tab = lambda: pl.BlockSpec((tm, LANES), lambda i: (i % nt, 0))
    return ([pl.BlockSpec((tm, d), lambda i: (i, 0))] + [full(w.shape) for w in weights]
            + [row(), row(), tab(), tab()])


def _mla_q(x, attn_norm, w_q_a, q_a_norm, w_q_b, gain, cos_l, sin_l, B, T, *, tm=512):
    nt = T // tm
    d = x.shape[1]
    weights = (attn_norm.reshape(1, d), w_q_a, q_a_norm.reshape(1, -1), w_q_b)
    gain = gain * (MLA_QK ** -0.5 * LOG2E)
    return pl.pallas_call(
        _mla_q_kernel,
        grid=(B * nt,),
        in_specs=_mla_specs(tm, nt, d, weights),
        out_specs=pl.BlockSpec((1, MLA_HEADS, tm, MLA_QK_PAD), lambda i: (i // nt, 0, i % nt, 0)),
        out_shape=jax.ShapeDtypeStruct((B, MLA_HEADS, T, MLA_QK_PAD), BF16),
        compiler_params=_cparams("parallel"),
        name="mla_q_path",
    )(x, *weights, gain[:QK_NOPE].reshape(1, LANES), _rope_layout(gain[QK_NOPE:]).reshape(1, LANES),
      cos_l, sin_l)


def _mla_kv(x, kv_norm, w_kv_a, kv_c_norm, w_kv_b, gain, cos_l, sin_l, B, T, *, tm=512):
    nt = T // tm
    d = x.shape[1]
    weights = (kv_norm.reshape(1, d), w_kv_a, kv_c_norm.reshape(1, -1), w_kv_b)
    return pl.pallas_call(
        _mla_kv_kernel,
        grid=(B * nt,),
        in_specs=_mla_specs(tm, nt, d, weights),
        out_specs=[pl.BlockSpec((1, MLA_HEADS, tm, MLA_QK_PAD), lambda i: (i // nt, 0, i % nt, 0)),
                   pl.BlockSpec((1, MLA_HEADS, tm, 2 * MLA_V), lambda i: (i // nt, 0, i % nt, 0))],
        out_shape=[jax.ShapeDtypeStruct((B, MLA_HEADS, T, MLA_QK_PAD), BF16),
                   jax.ShapeDtypeStruct((B, MLA_HEADS, T, 2 * MLA_V), BF16)],
        compiler_params=_cparams("parallel"),
        name="mla_kv_path",
    )(x, *weights, gain[:QK_NOPE].reshape(1, LANES), _rope_layout(gain[QK_NOPE:]).reshape(1, LANES),
      cos_l, sin_l)


def _two_slot_pipeline(first, last, stage, finish):
    a = first % 2
    n = last - first

    def pair(j, carry):
        stage(first + 2 * j, a)
        stage(first + 2 * j + 1, 1 - a)
        return carry

    lax.fori_loop(0, n // 2, pair, 0)

    @pl.when(n % 2 == 1)
    def _():
        stage(last - 1, a)
        finish(last, 1 - a)

    @pl.when(n % 2 == 0)
    def _():
        finish(last, a)


def _mla_attn_kernel(q_ref, k_ref, v_ref, o_ref, m_sc, acc_sc, s0_sc, s1_sc):
    TQ = MLA_TILE
    qi = pl.program_id(2)
    q = q_ref[0, 0]
    s_slots = (s0_sc, s1_sc)
    m_sc[...] = jnp.full_like(m_sc, NEG_INF)
    acc_sc[...] = jnp.zeros_like(acc_sc)

    def scores(i, slot):
        k0 = pl.multiple_of(i * TQ, TQ)
        s_slots[slot][...] = _dot_nt(q, k_ref[0, 0, pl.ds(k0, TQ), :])

    def update(i, slot, masked):
        k0 = pl.multiple_of(i * TQ, TQ)

        def load():
            s = s_slots[slot][...]
            if masked:
                qq = lax.broadcasted_iota(jnp.int32, (TQ, TQ), 0)
                kk = lax.broadcasted_iota(jnp.int32, (TQ, TQ), 1)
                s = jnp.where(kk <= qq, s, NEG_INF)
            return s

        m_old = m_sc[...]
        m_new = jnp.maximum(m_old, jnp.max(load(), axis=-1, keepdims=True))
        p = jnp.exp2((load() - jnp.concatenate([m_new] * (TQ // LANES), axis=-1)).astype(BF16))
        alpha = jnp.exp2(m_old - m_new)
        acc_sc[...] = (jnp.concatenate([alpha] * (acc_sc.shape[1] // LANES), axis=-1) * acc_sc[...]
                       + _dot(p, v_ref[0, 0, pl.ds(k0, TQ), :]))
        m_sc[...] = m_new

    def stage(i, slot):
        scores(i + 1, 1 - slot)
        update(i, slot, False)

    scores(0, 0)
    _two_slot_pipeline(0, qi, stage, lambda i, slot: update(i, slot, True))
    o_ref[...] = (acc_sc[:, :MLA_V] / acc_sc[:, MLA_V:]).astype(o_ref.dtype)


def _mla_attn(q, k, v, B, T):
    H, TQ = MLA_HEADS, MLA_TILE
    nq = T // TQ
    return pl.pallas_call(
        _mla_attn_kernel,
        grid=(B, H, nq),
        in_specs=[pl.BlockSpec((1, 1, TQ, MLA_QK_PAD), lambda b, h, i: (b, h, i, 0)),
                  pl.BlockSpec((1, 1, T, MLA_QK_PAD), lambda b, h, i: (b, h, 0, 0)),
                  pl.BlockSpec((1, 1, T, 2 * MLA_V), lambda b, h, i: (b, h, 0, 0))],
        out_specs=pl.BlockSpec((TQ, MLA_V), lambda b, h, i: (b * nq + i, h)),
        out_shape=jax.ShapeDtypeStruct((B * T, H * MLA_V), BF16),
        scratch_shapes=[pltpu.VMEM((TQ, LANES), F32), pltpu.VMEM((TQ, 2 * MLA_V), F32),
                        pltpu.VMEM((TQ, TQ), F32), pltpu.VMEM((TQ, TQ), F32)],
        compiler_params=_cparams("parallel", "parallel", "arbitrary"),
        name="mla_flash_attn",
    )(q, k, v)


def _pad_cols(w, mult=LANES):
    pad = -w.shape[1] % mult
    return jnp.pad(w, ((0, 0), (0, pad))) if pad else w


def kernel(x, a_attn_norm, a_w_in, a_q_norm, a_kcmp_norm, a_kslc_norm, a_kwin_norm, a_cmp_pos_k, a_cmp_pos_v, a_cmp_k_w1, a_cmp_k_b1, a_cmp_k_w2, a_cmp_v_w1, a_cmp_v_b1, a_cmp_v_w2, a_w_out, kv_norm, kv_w_a, kv_c_norm, kv_w_b, kv_k_norm, b_attn_norm, b_w_q_a, b_q_a_norm, b_w_q_b, b_q_norm, b_w_out, ffn_norm, ffn_w_gate_up, ffn_w_down):
    B, T, D = x.shape
    n_a = a_w_in.shape[0]
    n_b = b_w_q_a.shape[0]
    xs = x.reshape(B * T, D)

    inv = ROPE_THETA ** (-jnp.arange(0, QK_ROPE, 2, dtype=F32) / QK_ROPE)
    ang = jnp.arange(T, dtype=F32)[:, None] * inv[None, :]
    cos, sin = jnp.cos(ang), jnp.sin(ang)
    cos2 = _rope_layout(jnp.concatenate([cos, cos], axis=-1))
    sin2 = _rope_layout(jnp.concatenate([-sin, sin], axis=-1))

    k_shared = v_shared = None
    for layer in range(n_a + n_b):
        if layer < n_a:
            i = layer
            o = _nsa_mixer(xs, B, T, a_attn_norm[i], _pad_cols(a_w_in[i]).astype(BF16),
                           a_q_norm[i], a_kcmp_norm[i], a_kslc_norm[i], a_kwin_norm[i],
                           a_cmp_pos_k[i], a_cmp_pos_v[i], a_cmp_k_w1[i], a_cmp_k_b1[i], a_cmp_k_w2[i],
                           a_cmp_v_w1[i], a_cmp_v_b1[i], a_cmp_v_w2[i])
            w_out = a_w_out[i]
        else:
            j = layer - n_a
            q = _mla_q(xs, b_attn_norm[j], b_w_q_a[j].astype(BF16), b_q_a_norm[j],
                       _head_layout(b_w_q_b[j]).astype(BF16), b_q_norm[j], cos2, sin2, B, T)
            o = _mla_attn(q, k_shared, v_shared, B, T)
            w_out = b_w_out[j]
        xs = _out_ffn(xs, o, w_out.astype(BF16), ffn_norm[layer],
                      ffn_w_gate_up[layer].astype(BF16), ffn_w_down[layer].astype(BF16))
        if layer == n_a - 1:
            w_kv_a = jnp.concatenate([kv_w_a[:, :KV_LORA], _rope_layout(kv_w_a[:, KV_LORA:])], axis=-1)
            k_shared, v_shared = _mla_kv(xs, kv_norm, w_kv_a.astype(BF16), kv_c_norm, kv_w_b.astype(BF16),
                                         kv_k_norm, cos2, sin2, B, T)
    return xs.reshape(B, T, D)
```

```python
import functools

import jax
import jax.numpy as jnp
from jax import lax
from jax.experimental import pallas as pl
from jax.experimental.pallas import tpu as pltpu

F32 = jnp.float32
BF16 = jnp.bfloat16

NORM_EPS = 1e-6
NEG_INF = -1e30
LANES = 128
SUBLANES = 8

NSA_HEADS = 16
NSA_GROUPS = 4
NSA_HPG = NSA_HEADS // NSA_GROUPS
NSA_DH = 64
CMP_BLOCK = 32
CMP_STRIDE = 16
CMP_HIDDEN = 256
SLC_BLOCK = 64
SLC_TOP = 16
WINDOW = 512
FORCE_SCORE = 1e4
NSA_Q_TILE = 256
NSA_CMP_TILE = 1024
NSA_KEY_TILE = 256
PAIR = 2
LOG2E = 1.4426950408889634

MLA_HEADS = 8
QK_NOPE = 128
QK_ROPE = 64
MLA_QK = QK_NOPE + QK_ROPE
MLA_QK_PAD = QK_NOPE + LANES
MLA_V = 128
Q_LORA = 384
KV_LORA = 256
ROPE_THETA = 10000.0
MLA_TILE = 512

VMEM_LIMIT = 56 * 1024 * 1024


def _cparams(*sem):
    return pltpu.CompilerParams(dimension_semantics=sem, vmem_limit_bytes=VMEM_LIMIT)


def _rms(x, g):
    return x * lax.rsqrt(jnp.mean(x * x, axis=-1, keepdims=True) + NORM_EPS) * g


def _dot(a, b):
    return jnp.dot(a, b, preferred_element_type=F32)


def _dot_nt(a, b):
    return lax.dot_general(a, b, (((1,), (1,)), ((), ())), preferred_element_type=F32)


def _out_ffn_kernel(x_ref, a_ref, wo_ref, g_ref, wgu_ref, wd_ref, o_ref):
    hid = wd_ref.shape[0]
    x = x_ref[...] + _dot(a_ref[...], wo_ref[...])
    h = _rms(x, g_ref[...]).astype(BF16)
    gate = _dot(h, wgu_ref[:, :hid])
    up = _dot(h, wgu_ref[:, hid:])
    a = (gate * jax.nn.sigmoid(gate) * up).astype(BF16)
    o_ref[...] = x + _dot(a, wd_ref[...])


def _out_ffn(x, mixed, w_out, gain, w_gate_up, w_down, *, tm=512):
    n, d = x.shape
    hid = w_down.shape[0]
    dm = mixed.shape[1]
    assert n % tm == 0 and hid % LANES == 0
    resident = lambda shape: pl.BlockSpec(shape, lambda i: (0, 0), pipeline_mode=pl.Buffered(1))
    return pl.pallas_call(
        _out_ffn_kernel,
        grid=(n // tm,),
        in_specs=[pl.BlockSpec((tm, d), lambda i: (i, 0)),
                  pl.BlockSpec((tm, dm), lambda i: (i, 0)),
                  resident((dm, d)),
                  pl.BlockSpec((1, d), lambda i: (0, 0)),
                  resident((d, 2 * hid)), resident((hid, d))],
        out_specs=pl.BlockSpec((tm, d), lambda i: (i, 0)),
        out_shape=jax.ShapeDtypeStruct((n, d), F32),
        compiler_params=_cparams("parallel"),
        name="out_proj_swiglu_ffn",
    )(x, mixed, w_out, gain.reshape(1, d), w_gate_up, w_down)


def _split2(x):
    hi = x.astype(BF16)
    return hi, (x - hi.astype(F32)).astype(BF16)


def _nsa_in_kernel(x_ref, an_ref, w_ref, gain_ref, gsum_ref, gexp_ref,
                   q_ref, kc_ref, vc_ref, ksl_ref, vsl_ref, kw_ref, vw_ref, gt_ref, rows_sc):
    G, HG, DH = NSA_GROUPS, NSA_HPG, NSA_DH
    qw, kvw = NSA_HEADS * DH, G * DH
    tm = x_ref.shape[0]
    p = _dot(_rms(x_ref[...], an_ref[...]).astype(BF16), w_ref[...])
    ones_col = jnp.ones((tm, LANES - DH), F32)
    n_gate = HG * 3
    gate_pad = jnp.zeros((tm, LANES - DH - n_gate), F32)

    def piece(base, g):
        return p[:, base + g * DH: base + (g + 1) * DH]

    xn = jnp.concatenate([p[:, :qw], p[:, qw + 2 * kvw:qw + 3 * kvw], p[:, qw + 4 * kvw:qw + 5 * kvw]], axis=-1)
    hi, lo = _split2(xn * xn)
    ss = _dot(hi, gsum_ref[...]) + _dot(lo, gsum_ref[...])
    hi, lo = _split2(lax.rsqrt(ss * (1.0 / DH) + NORM_EPS))
    xn = xn * (_dot(hi, gexp_ref[...]) + _dot(lo, gexp_ref[...])) * gain_ref[...]

    for g in range(G):
        for h in range(HG):
            q_ref[0, g, h] = xn[:, (g * HG + h) * DH:(g * HG + h + 1) * DH].astype(BF16)
        for which, (base, out_ref) in enumerate(((qw, kc_ref), (qw + kvw, vc_ref))):
            stage_ref = rows_sc.at[2 * g + which]
            stage_ref[...] = piece(base, g)
            for m in range(0, CMP_STRIDE, 2):
                pair = [stage_ref[pl.ds(m + d, tm // CMP_STRIDE, stride=CMP_STRIDE), :] for d in range(2)]
                out_ref[0, g, :, m * DH:(m + 2) * DH] = jnp.concatenate(pair, axis=-1).astype(BF16)
        ksl_ref[0, g] = xn[:, qw + g * DH:qw + (g + 1) * DH].astype(BF16)
        vsl_ref[0, g] = jnp.concatenate([piece(qw + 3 * kvw, g), ones_col], axis=-1).astype(BF16)
        kw_ref[0, g] = xn[:, qw + kvw + g * DH:qw + kvw + (g + 1) * DH].astype(BF16)
        vw_ref[0, g] = jnp.concatenate([piece(qw + 5 * kvw, g), ones_col], axis=-1).astype(BF16)
        gb = qw + 6 * kvw + g * n_gate
        gt_ref[0, g] = jnp.concatenate([jnp.zeros((tm, DH), F32), jax.nn.sigmoid(p[:, gb:gb + n_gate]),
                                        gate_pad], axis=-1)


def _nsa_in(x, attn_norm, w_in, q_norm, kslc_norm, kwin_norm, B, T, *, tm=512):
    G, HG, DH = NSA_GROUPS, NSA_HPG, NSA_DH
    nt = T // tm
    d, width = w_in.shape
    n_norm = (NSA_HEADS + 2 * G) * DH
    gain = jnp.concatenate([jnp.tile(q_norm, NSA_HEADS) * (DH ** -0.5 * LOG2E),
                            jnp.tile(kslc_norm, G), jnp.tile(kwin_norm, G)]).reshape(1, n_norm)
    gsum = (jnp.arange(n_norm)[:, None] // DH == jnp.arange(LANES)[None, :]).astype(BF16)
    full = lambda shape: pl.BlockSpec(shape, lambda i: (0, 0))
    per_group = lambda last: pl.BlockSpec((1, G, tm, last), lambda i: (i // nt, 0, i % nt, 0))
    shp = lambda last, dt: jax.ShapeDtypeStruct((B, G, T, last), dt)
    half = CMP_STRIDE * DH
    cmp_spec = pl.BlockSpec((1, G, tm // CMP_STRIDE, half), lambda i: (i // nt, 0, i % nt, 0))
    cmp_shape = jax.ShapeDtypeStruct((B, G, T // CMP_STRIDE, half), BF16)
    return pl.pallas_call(
        _nsa_in_kernel,
        grid=(B * nt,),
        in_specs=[pl.BlockSpec((tm, d), lambda i: (i, 0)), full((1, d)), full((d, width)),
                  full((1, n_norm)), full((n_norm, LANES)), full((LANES, n_norm))],
        out_specs=[pl.BlockSpec((1, G, HG, tm, DH), lambda i: (i // nt, 0, 0, i % nt, 0)),
                   cmp_spec, cmp_spec, per_group(DH), per_group(LANES),
                   per_group(DH), per_group(LANES), per_group(LANES)],
        out_shape=[jax.ShapeDtypeStruct((B, G, HG, T, DH), BF16),
                   cmp_shape, cmp_shape, shp(DH, BF16), shp(LANES, BF16),
                   shp(DH, BF16), shp(LANES, BF16), shp(LANES, F32)],
        scratch_shapes=[pltpu.VMEM((2 * G, tm, DH), F32)],
        compiler_params=_cparams("parallel"),
        name="nsa_in_proj_split",
    )(x, attn_norm.reshape(1, d), w_in, gain, gsum, gsum.T)


def _compress_kernel(rk_ref, rv_ref, pek_ref, pev_ref, kw1_ref, kb1_ref, kw2_ref,
                     vw1_ref, vb1_ref, vw2_ref, kn_ref, kcmp_ref, vcmp_ref):
    half = CMP_STRIDE * NSA_DH
    nrow = rk_ref.shape[2]

    def mlp(r_ref, pe_ref, w1_ref, b1_ref, w2_ref):
        r = r_ref[0, 0].astype(BF16)
        ya = _dot(r, w1_ref[:half, :])
        yb = _dot(r, w1_ref[half:, :])
        pe = jnp.broadcast_to(pe_ref[...], (SUBLANES, 2 * half)).astype(BF16)
        c = _dot(pe, w1_ref[...])[0:1] + b1_ref[...]
        hid = ya + pltpu.roll(yb, nrow - 1, 0) + c
        return _dot(jax.nn.gelu(hid).astype(BF16), w2_ref[...])

    kcmp_ref[0, 0] = _rms(mlp(rk_ref, pek_ref, kw1_ref, kb1_ref, kw2_ref), kn_ref[...]).astype(BF16)
    vcmp_ref[0, 0] = mlp(rv_ref, pev_ref, vw1_ref, vb1_ref, vw2_ref).astype(BF16)


def _compress(rk, rv, pe_k, pe_v, k_w1, k_b1, k_w2, v_w1, v_b1, v_w2, kcmp_norm):
    B, G, nrow, half = rk.shape
    DH, HID = NSA_DH, CMP_HIDDEN
    full = lambda shape: pl.BlockSpec(shape, lambda b, g: (0,) * len(shape))
    r_spec = pl.BlockSpec((1, 1, nrow, half), lambda b, g: (b, g, 0, 0))
    o_spec = pl.BlockSpec((1, 1, nrow, DH), lambda b, g: (b, g, 0, 0))
    return pl.pallas_call(
        _compress_kernel,
        grid=(B, G),
        in_specs=[r_spec, r_spec, full((1, 2 * half)), full((1, 2 * half)),
                  full((2 * half, HID)), full((1, HID)), full((HID, DH)),
                  full((2 * half, HID)), full((1, HID)), full((HID, DH)), full((1, DH))],
        out_specs=[o_spec, o_spec],
        out_shape=[jax.ShapeDtypeStruct((B, G, nrow, DH), BF16)] * 2,
        compiler_params=_cparams("parallel", "parallel"),
        name="nsa_compress",
    )(rk, rv, pe_k.reshape(1, -1), pe_v.reshape(1, -1),
      k_w1.astype(BF16), k_b1.reshape(1, HID), k_w2.astype(BF16),
      v_w1.astype(BF16), v_b1.reshape(1, HID), v_w2.astype(BF16), kcmp_norm.reshape(1, DH))


def _split3(x):
    hi = x.astype(BF16)
    r1 = x - hi.astype(F32)
    mid = r1.astype(BF16)
    lo = (r1 - mid.astype(F32)).astype(BF16)
    return hi, mid, lo


def _nsa_cmp_kernel(q_ref, kc_ref, vc_ref, gt_ref, sl_ref, ov_ref, gm_ref,
                    oc_ref, sel_ref, act_ref, *, n_top):
    HG, DH, TQ, TM = NSA_HPG, NSA_DH, NSA_CMP_TILE, NSA_Q_TILE
    s0 = pl.program_id(2) * TQ
    ncmp = kc_ref.shape[2]
    n_slc = ov_ref.shape[1]
    qs = q_ref[0, 0].reshape(HG * TQ, DH)
    gt = gt_ref[0, 0]

    def attend(nc):
        j = lax.broadcasted_iota(jnp.int32, (1, nc), 1)
        t = s0 + lax.broadcasted_iota(jnp.int32, (TQ, 1), 0)
        mask = j * CMP_STRIDE + (CMP_BLOCK - 1) <= t
        mid = (j * CMP_STRIDE - s0).astype(F32) + 0.5 * (CMP_BLOCK - 1)
        has_key = t >= CMP_BLOCK - 1
        s = _dot_nt(qs, kc_ref[0, 0, :nc, :])
        psum = jnp.zeros((TQ, nc), F32)
        ps = []
        for h in range(HG):
            rows = slice(h * TQ, (h + 1) * TQ)
            slope_row = jnp.concatenate([sl_ref[0, h * TM:h * TM + 1, :]] * (nc // LANES), axis=-1)
            sh = jnp.where(mask, s[rows] + slope_row * mid, NEG_INF)
            m = jnp.max(sh, axis=-1, keepdims=True)
            e = jnp.exp2(sh - m)
            l = jnp.sum(e, axis=-1, keepdims=True)
            inv = jnp.where(has_key, 1.0 / l, 0.0)
            p = e * inv
            psum = psum + p
            ps.append(p.astype(BF16))
        o_all = _dot(jnp.concatenate(ps, axis=0), vc_ref[0, 0, :nc, :])
        oc_ref[...] = jnp.concatenate(
            [o_all[h * TQ:(h + 1) * TQ] * gt[:, DH + 3 * h:DH + 3 * h + 1] for h in range(HG)],
            axis=-1).astype(oc_ref.dtype)
        hi, md, lo = _split3(psum)
        ov = ov_ref[:nc, :]
        imp = _dot(hi, ov) + _dot(md, ov) + _dot(lo, ov)
        select(imp, nc * CMP_STRIDE // SLC_BLOCK)

    def select(imp, nb):
        imp_t = imp.T[:nb]
        blk = lax.broadcasted_iota(jnp.int32, (nb, TQ), 0)
        tq = s0 + lax.broadcasted_iota(jnp.int32, (nb, TQ), 1)
        cur = lax.shift_right_logical(tq, SLC_BLOCK.bit_length() - 1)
        forced = (blk == 0) | (blk == cur) | (blk == cur - 1)
        taken = -2.0
        score = jnp.where(forced, taken, jnp.where(blk * SLC_BLOCK <= tq, imp_t, -1.0))

        def pick(_, score):
            mx = jnp.max(score, axis=0, keepdims=True)
            first = jnp.min(jnp.where(score == mx, blk, nb), axis=0, keepdims=True)
            return jnp.where(blk == first, taken, score)

        score = lax.fori_loop(0, n_top - 3, pick, score, unroll=True)
        sel_t = jnp.where(score == taken, 1.0, 0.0)
        if nb < n_slc:
            sel_t = jnp.concatenate([sel_t, jnp.zeros((n_slc - nb, TQ), F32)], axis=0)
        sel = sel_t.T.astype(BF16)
        sel_ref[0, 0] = sel
        for sub in range(TQ // TM):
            cnt = _dot(jnp.ones((SUBLANES, TM), BF16), sel[sub * TM:(sub + 1) * TM])
            act_ref[0, 0, sub] = _dot((cnt > 0.0).astype(BF16), gm_ref[...])

    need = (s0 + TQ) // CMP_STRIDE
    widths = list(range(LANES, ncmp + 1, LANES))
    for idx, nc in enumerate(widths):
        lo_w = widths[idx - 1] if idx else 0
        pl.when((need > lo_w) & (need <= nc))(functools.partial(attend, nc))


def _nsa_cmp(q, kcmp, vcmp, gates, slope_tab, overlap, group_mat, B, T, n_top):
    G, HG, DH, TQ, TM = NSA_GROUPS, NSA_HPG, NSA_DH, NSA_CMP_TILE, NSA_Q_TILE
    nq = T // TQ
    ncmp = kcmp.shape[2]
    n_slc = overlap.shape[1]
    assert n_top >= 3 and ncmp % LANES == 0 and HG < FORCE_SCORE
    return pl.pallas_call(
        functools.partial(_nsa_cmp_kernel, n_top=n_top),
        grid=(B, G, nq),
        in_specs=[pl.BlockSpec((1, 1, HG, TQ, DH), lambda b, g, i: (b, g, 0, i, 0)),
                  pl.BlockSpec((1, 1, ncmp, DH), lambda b, g, i: (b, g, 0, 0)),
                  pl.BlockSpec((1, 1, ncmp, DH), lambda b, g, i: (b, g, 0, 0)),
                  pl.BlockSpec((1, 1, TQ, LANES), lambda b, g, i: (b, g, i, 0)),
                  pl.BlockSpec((1, HG * TM, LANES), lambda b, g, i: (g, 0, 0)),
                  pl.BlockSpec((ncmp, n_slc), lambda b, g, i: (0, 0)),
                  pl.BlockSpec((n_slc, LANES), lambda b, g, i: (0, 0))],
        out_specs=[pl.BlockSpec((TQ, HG * DH), lambda b, g, i: (b * nq + i, g)),
                   pl.BlockSpec((1, 1, TQ, n_slc), lambda b, g, i: (b, g, i, 0)),
                   pl.BlockSpec((1, 1, TQ // TM, SUBLANES, LANES), lambda b, g, i: (b, g, i, 0, 0))],
        out_shape=[jax.ShapeDtypeStruct((B * T, NSA_HEADS * DH), BF16),
                   jax.ShapeDtypeStruct((B, G, T, n_slc), BF16),
                   jax.ShapeDtypeStruct((B, G, T // TM, SUBLANES, LANES), F32)],
        compiler_params=_cparams("parallel", "parallel", "arbitrary"),
        name="nsa_cmp_select",
    )(q, kcmp, vcmp, gates, slope_tab, overlap, group_mat)


def _nsa_main_kernel(list_ref, cnt_ref, q_ref, sel_ref, oc_ref, gt_ref, sl_ref, ex_ref,
                     ksl_ref, vsl_ref, kw_ref, vw_ref, o_ref,
                     m_sc, acc_sc, s0_sc, s1_sc, pk0_sc, pk1_sc, *, n_ktiles):
    HG, DH, TQ, KT = NSA_HPG, NSA_DH, NSA_Q_TILE, NSA_KEY_TILE
    s_slots, pk_slots = (s0_sc, s1_sc), (pk0_sc, pk1_sc)
    WK = WINDOW + TQ
    b, g, qi = pl.program_id(0), pl.program_id(1), pl.program_id(2)
    nq = pl.num_programs(2)
    s0 = qi * TQ
    qs = q_ref[0, 0].reshape(HG * TQ, DH)
    selb = sel_ref[0, 0]
    gt = gt_ref[0, 0]
    t_col = s0 + lax.broadcasted_iota(jnp.int32, (TQ, 1), 0)

    def biased(s, kpos, mask, blk, head=None):
        head = blk if head is None else head
        width = kpos.shape[1]
        slope_row = jnp.concatenate([sl_ref[0, head * TQ:head * TQ + 1, :]] * (width // LANES), axis=-1)
        sh = s[blk * TQ:(blk + 1) * TQ] + slope_row * (kpos - s0).astype(F32)
        return jnp.where(mask, sh, NEG_INF)

    step = (b * NSA_GROUPS + g) * nq + qi
    last = cnt_ref[step] - 1

    def key_start(j):
        return pl.multiple_of(list_ref[step * n_ktiles + j] * KT, KT)

    def scores(j, slot):
        k0 = key_start(j)
        s_slots[slot][...] = _dot_nt(qs, ksl_ref[0, 0, pl.ds(k0, KT), :])
        pk_slots[slot][...] = _dot(selb, ex_ref[:, pl.ds(k0, KT)])

    def update(j, slot, first=False):
        k0 = key_start(j)
        kpos = k0 + lax.broadcasted_iota(jnp.int32, (1, KT), 1)
        mask = (pk_slots[slot][...] > 0.5) & (kpos <= t_col)
        ps, alphas = [], []
        for h in range(HG):
            rows = slice(h * TQ, (h + 1) * TQ)
            sh = biased(s_slots[slot], kpos, mask, h)
            m_new = jnp.broadcast_to(jnp.max(sh, axis=-1, keepdims=True), (TQ, LANES))
            if not first:
                m_old = m_sc[rows]
                m_new = jnp.maximum(m_old, m_new)
                alphas.append(jnp.exp2(m_old - m_new))
            ps.append(jnp.exp2((sh - jnp.concatenate([m_new] * (KT // LANES), axis=-1)).astype(BF16)))
            m_sc[rows] = m_new
        pv = _dot(jnp.concatenate(ps, axis=0), vsl_ref[0, 0, pl.ds(k0, KT), :])
        acc_sc[...] = pv if first else jnp.concatenate(alphas, axis=0) * acc_sc[...] + pv

    def stage(j, slot):
        scores(j + 1, 1 - slot)
        update(j, slot)

    scores(0, 0)

    @pl.when(last == 0)
    def _():
        update(0, 0, first=True)

    @pl.when(last > 0)
    def _():
        scores(1, 1)
        update(0, 0, first=True)
        _two_slot_pipeline(1, last, stage, update)

    start = pl.multiple_of(jnp.maximum(s0 - WINDOW, 0), TQ)
    kpos_w = start + lax.broadcasted_iota(jnp.int32, (1, WK), 1)
    mask_w = (kpos_w <= t_col) & (kpos_w > t_col - WINDOW)
    acc_w = []
    for h0 in range(0, HG, PAIR):
        s_w = _dot_nt(qs[h0 * TQ:(h0 + PAIR) * TQ], kw_ref[0, 0, pl.ds(start, WK), :])
        pw = []
        for h in range(PAIR):
            sh = biased(s_w, kpos_w, mask_w, h, head=h0 + h)
            pw.append(jnp.exp2((sh - jnp.max(sh, axis=-1, keepdims=True)).astype(BF16)))
        acc_w.append(_dot(jnp.concatenate(pw, axis=0), vw_ref[0, 0, pl.ds(start, WK), :]))
    acc_w = jnp.concatenate(acc_w, axis=0)

    o_cmp = oc_ref[...].astype(F32)
    outs = []
    for h in range(HG):
        rows = slice(h * TQ, (h + 1) * TQ)
        a_s, a_w = acc_sc[rows], acc_w[rows]
        r_s, r_w = gt / a_s, gt / a_w
        outs.append(o_cmp[:, h * DH:(h + 1) * DH]
                    + r_s[:, DH + 3 * h + 1:DH + 3 * h + 2] * a_s[:, :DH]
                    + r_w[:, DH + 3 * h + 2:DH + 3 * h + 3] * a_w[:, :DH])
    o_ref[...] = jnp.concatenate(outs, axis=-1).astype(o_ref.dtype)


def _nsa_main(tile_list, tile_count, q, sel, oc, gates, slope_tab, expand, ksl, vsl, kw, vw, B, T):
    G, HG, DH, TQ = NSA_GROUPS, NSA_HPG, NSA_DH, NSA_Q_TILE
    nq = T // TQ
    n_slc = sel.shape[3]
    n_ktiles = T // NSA_KEY_TILE
    kv = lambda last: pl.BlockSpec((1, 1, T, last), lambda b, g, i, tl, tc: (b, g, 0, 0))
    grid_spec = pltpu.PrefetchScalarGridSpec(
        num_scalar_prefetch=2,
        grid=(B, G, nq),
        in_specs=[pl.BlockSpec((1, 1, HG, TQ, DH), lambda b, g, i, tl, tc: (b, g, 0, i, 0)),
                  pl.BlockSpec((1, 1, TQ, n_slc), lambda b, g, i, tl, tc: (b, g, i, 0)),
                  pl.BlockSpec((TQ, HG * DH), lambda b, g, i, tl, tc: (b * nq + i, g)),
                  pl.BlockSpec((1, 1, TQ, LANES), lambda b, g, i, tl, tc: (b, g, i, 0)),
                  pl.BlockSpec((1, HG * TQ, LANES), lambda b, g, i, tl, tc: (g, 0, 0)),
                  pl.BlockSpec((n_slc, T), lambda b, g, i, tl, tc: (0, 0)),
                  kv(DH), kv(LANES), kv(DH), kv(LANES)],
        out_specs=pl.BlockSpec((TQ, HG * DH), lambda b, g, i, tl, tc: (b * nq + i, g)),
        scratch_shapes=[pltpu.VMEM((HG * TQ, LANES), F32), pltpu.VMEM((HG * TQ, LANES), F32),
                        pltpu.VMEM((HG * TQ, NSA_KEY_TILE), F32), pltpu.VMEM((HG * TQ, NSA_KEY_TILE), F32),
                        pltpu.VMEM((TQ, NSA_KEY_TILE), F32), pltpu.VMEM((TQ, NSA_KEY_TILE), F32)],
    )
    return pl.pallas_call(
        functools.partial(_nsa_main_kernel, n_ktiles=n_ktiles),
        grid_spec=grid_spec,
        out_shape=jax.ShapeDtypeStruct((B * T, NSA_HEADS * DH), BF16),
        compiler_params=_cparams("parallel", "parallel", "arbitrary"),
        name="nsa_select_window",
    )(tile_list, tile_count, q, sel, oc, gates, slope_tab, expand, ksl, vsl, kw, vw)


def _nsa_mixer(x, B, T, attn_norm, w_in, q_norm, kcmp_norm, kslc_norm, kwin_norm, pos_k, pos_v,
               k_w1, k_b1, k_w2, v_w1, v_b1, v_w2):
    G, HG, DH, TQ = NSA_GROUPS, NSA_HPG, NSA_DH, NSA_Q_TILE
    assert T % NSA_CMP_TILE == 0 and T % MLA_TILE == 0 and T >= WINDOW + TQ and NSA_KEY_TILE == TQ
    q, kc, vc, ksl, vsl, kw, vw, gates = _nsa_in(x, attn_norm, w_in, q_norm, kslc_norm, kwin_norm, B, T)

    nrow = T // CMP_STRIDE
    kcmp, vcmp = _compress(kc, vc, pos_k, pos_v, k_w1, k_b1, k_w2, v_w1, v_b1, v_w2, kcmp_norm)

    n_slc = T // SLC_BLOCK
    n_top = min(SLC_TOP, n_slc)
    n_slc_pad = max(n_slc, LANES)
    slopes = 2.0 ** (-8.0 * jnp.arange(1, NSA_HEADS + 1, dtype=F32) / NSA_HEADS) * LOG2E
    slope_tab = jnp.broadcast_to(slopes.reshape(G, HG, 1, 1), (G, HG, TQ, LANES)).reshape(G, HG * TQ, LANES)
    cj = jnp.arange(nrow)[:, None] * CMP_STRIDE
    si = jnp.arange(n_slc_pad)[None, :] * SLC_BLOCK
    overlap = ((cj <= si + SLC_BLOCK - 1) & (cj + CMP_BLOCK - 1 >= si)
               & (jnp.arange(nrow)[:, None] < nrow - 1)).astype(BF16)
    blocks_per_tile = NSA_KEY_TILE // SLC_BLOCK
    group_mat = (jnp.arange(n_slc_pad)[:, None] // blocks_per_tile
                 == jnp.arange(LANES)[None, :]).astype(BF16)
    expand = (jnp.arange(n_slc_pad)[:, None] == jnp.arange(T)[None, :] // SLC_BLOCK).astype(BF16)

    oc, sel, act = _nsa_cmp(q, kcmp, vcmp, gates, slope_tab, overlap, group_mat, B, T, n_top)
    n_ktiles = T // NSA_KEY_TILE
    tile = jnp.arange(n_ktiles)
    reach = tile[None, :] * NSA_KEY_TILE < (jnp.arange(T // TQ)[:, None] + 1) * TQ
    visit = ((act[:, :, :, 0, :n_ktiles] > 0.0) & reach) | (tile == 0)
    slot = jnp.cumsum(visit, axis=-1) - 1
    tile_list = jnp.sum(jnp.where(visit[..., :, None] & (slot[..., :, None] == tile), tile[:, None], 0), axis=-2)
    tile_count = jnp.sum(visit, axis=-1)
    return _nsa_main(tile_list.astype(jnp.int32).reshape(-1), tile_count.astype(jnp.int32).reshape(-1),
                     q, sel, oc, gates, slope_tab, expand, ksl, vsl, kw, vw, B, T)


def _rope_layout(w):
    half = QK_ROPE // 2
    z = jnp.zeros(w.shape[:-1] + (LANES // 2 - half,), w.dtype)
    return jnp.concatenate([w[..., :half], z, w[..., half:], z], axis=-1)


def _head_layout(w):
    w = w.reshape(w.shape[:-1] + (MLA_HEADS, MLA_QK))
    w = jnp.concatenate([w[..., :QK_NOPE], _rope_layout(w[..., QK_NOPE:])], axis=-1)
    return w.reshape(w.shape[:-2] + (MLA_HEADS * MLA_QK_PAD,))


def _roped(rope, gr_ref, cos_ref, sin_ref):
    r = rope * gr_ref[...]
    return r * cos_ref[...] + pltpu.roll(r, LANES // 2, 1) * sin_ref[...]


def _head_inv_rms(nope, rope_sq):
    ss = jnp.sum(nope * nope + rope_sq, axis=-1, keepdims=True)
    return lax.rsqrt(ss * (1.0 / MLA_QK) + NORM_EPS)


def _mla_q_kernel(x_ref, an_ref, wa_ref, qan_ref, wb_ref, gn_ref, gr_ref, cos_ref, sin_ref, o_ref):
    qa = _dot(_rms(x_ref[...], an_ref[...]).astype(BF16), wa_ref[...])
    q = _dot(_rms(qa, qan_ref[...]).astype(BF16), wb_ref[...])
    for h in range(MLA_HEADS):
        nope = q[:, h * MLA_QK_PAD:h * MLA_QK_PAD + QK_NOPE]
        rope = q[:, h * MLA_QK_PAD + QK_NOPE:(h + 1) * MLA_QK_PAD]
        inv = _head_inv_rms(nope, rope * rope)
        o_ref[0, h] = jnp.concatenate([nope * inv * gn_ref[...], _roped(rope * inv, gr_ref, cos_ref, sin_ref)],
                                      axis=-1).astype(BF16)


def _mla_kv_kernel(x_ref, n_ref, wa_ref, cn_ref, wb_ref, gn_ref, gr_ref, cos_ref, sin_ref, k_ref, v_ref):
    kv_a = _dot(_rms(x_ref[...], n_ref[...]).astype(BF16), wa_ref[...])
    kv = _dot(_rms(kv_a[:, :KV_LORA], cn_ref[...]).astype(BF16), wb_ref[...])
    rope = kv_a[:, KV_LORA:]
    rope_sq = rope * rope
    roped = _roped(rope, gr_ref, cos_ref, sin_ref)
    for h in range(MLA_HEADS):
        base = h * (QK_NOPE + MLA_V)
        nope = kv[:, base:base + QK_NOPE]
        inv = _head_inv_rms(nope, rope_sq)
        k_ref[0, h] = jnp.concatenate([nope * inv * gn_ref[...], inv * roped], axis=-1).astype(BF16)
        v = kv[:, base + QK_NOPE:base + QK_NOPE + MLA_V]
        v_ref[0, h] = jnp.concatenate([v, jnp.ones_like(v)], axis=-1).astype(BF16)


def _mla_specs(tm, nt, d, weights):
    full = lambda shape: pl.BlockSpec(shape, lambda i: (0, 0))
    row = lambda: pl.BlockSpec((1, LANES), lambda i: (0, 0))
    tab = lambda: pl.BlockSpec((tm, LANES), lambda i: (i % nt, 0))
    return ([pl.BlockSpec((tm, d), lambda i: (i, 0))] + [full(w.shape) for w in weights]
            + [row(), row(), tab(), tab()])


def _mla_q(x, attn_norm, w_q_a, q_a_norm, w_q_b, gain, cos_l, sin_l, B, T, *, tm=512):
    nt = T // tm
    d = x.shape[1]
    weights = (attn_norm.reshape(1, d), w_q_a, q_a_norm.reshape(1, -1), w_q_b)
    gain = gain * (MLA_QK ** -0.5 * LOG2E)
    return pl.pallas_call(
        _mla_q_kernel,
        grid=(B * nt,),
        in_specs=_mla_specs(tm, nt, d, weights),
        out_specs=pl.BlockSpec((1, MLA_HEADS, tm, MLA_QK_PAD), lambda i: (i // nt, 0, i % nt, 0)),
        out_shape=jax.ShapeDtypeStruct((B, MLA_HEADS, T, MLA_QK_PAD), BF16),
        compiler_params=_cparams("parallel"),
        name="mla_q_path",
    )(x, *weights, gain[:QK_NOPE].reshape(1, LANES), _rope_layout(gain[QK_NOPE:]).reshape(1, LANES),
      cos_l, sin_l)


def _mla_kv(x, kv_norm, w_kv_a, kv_c_norm, w_kv_b, gain, cos_l, sin_l, B, T, *, tm=512):
    nt = T // tm
    d = x.shape[1]
    weights = (kv_norm.reshape(1, d), w_kv_a, kv_c_norm.reshape(1, -1), w_kv_b)
    return pl.pallas_call(
        _mla_kv_kernel,
        grid=(B * nt,),
        in_specs=_mla_specs(tm, nt, d, weights),
        out_specs=[pl.BlockSpec((1, MLA_HEADS, tm, MLA_QK_PAD), lambda i: (i // nt, 0, i % nt, 0)),
                   pl.BlockSpec((1, MLA_HEADS, tm, 2 * MLA_V), lambda i: (i // nt, 0, i % nt, 0))],
        out_shape=[jax.ShapeDtypeStruct((B, MLA_HEADS, T, MLA_QK_PAD), BF16),
                   jax.ShapeDtypeStruct((B, MLA_HEADS, T, 2 * MLA_V), BF16)],
        compiler_params=_cparams("parallel"),
        name="mla_kv_path",
    )(x, *weights, gain[:QK_NOPE].reshape(1, LANES), _rope_layout(gain[QK_NOPE:]).reshape(1, LANES),
      cos_l, sin_l)


def _two_slot_pipeline(first, last, stage, finish):
    a = first % 2
    n = last - first

    def pair(j, carry):
        stage(first + 2 * j, a)
        stage(first + 2 * j + 1, 1 - a)
        return carry

    lax.fori_loop(0, n // 2, pair, 0)

    @pl.when(n % 2 == 1)
    def _():
        stage(last - 1, a)
        finish(last, 1 - a)

    @pl.when(n % 2 == 0)
    def _():
        finish(last, a)


def _mla_attn_kernel(q_ref, k_ref, v_ref, o_ref, m_sc, acc_sc, s0_sc, s1_sc):
    TQ = MLA_TILE
    qi = pl.program_id(2)
    q = q_ref[0, 0]
    s_slots = (s0_sc, s1_sc)
    m_sc[...] = jnp.full_like(m_sc, NEG_INF)
    acc_sc[...] = jnp.zeros_like(acc_sc)

    def scores(i, slot):
        k0 = pl.multiple_of(i * TQ, TQ)
        s_slots[slot][...] = _dot_nt(q, k_ref[0, 0, pl.ds(k0, TQ), :])

    def update(i, slot, masked):
        k0 = pl.multiple_of(i * TQ, TQ)

        def load():
            s = s_slots[slot][...]
            if masked:
                qq = lax.broadcasted_iota(jnp.int32, (TQ, TQ), 0)
                kk = lax.broadcasted_iota(jnp.int32, (TQ, TQ), 1)
                s = jnp.where(kk <= qq, s, NEG_INF)
            return s

        m_old = m_sc[...]
        m_new = jnp.maximum(m_old, jnp.max(load(), axis=-1, keepdims=True))
        p = jnp.exp2((load() - jnp.concatenate([m_new] * (TQ // LANES), axis=-1)).astype(BF16))
        alpha = jnp.exp2(m_old - m_new)
        acc_sc[...] = (jnp.concatenate([alpha] * (acc_sc.shape[1] // LANES), axis=-1) * acc_sc[...]
                       + _dot(p, v_ref[0, 0, pl.ds(k0, TQ), :]))
        m_sc[...] = m_new

    def stage(i, slot):
        scores(i + 1, 1 - slot)
        update(i, slot, False)

    scores(0, 0)
    _two_slot_pipeline(0, qi, stage, lambda i, slot: update(i, slot, True))
    o_ref[...] = (acc_sc[:, :MLA_V] / acc_sc[:, MLA_V:]).astype(o_ref.dtype)


def _mla_attn(q, k, v, B, T):
    H, TQ = MLA_HEADS, MLA_TILE
    nq = T // TQ
    return pl.pallas_call(
        _mla_attn_kernel,
        grid=(B, H, nq),
        in_specs=[pl.BlockSpec((1, 1, TQ, MLA_QK_PAD), lambda b, h, i: (b, h, i, 0)),
                  pl.BlockSpec((1, 1, T, MLA_QK_PAD), lambda b, h, i: (b, h, 0, 0)),
                  pl.BlockSpec((1, 1, T, 2 * MLA_V), lambda b, h, i: (b, h, 0, 0))],
        out_specs=pl.BlockSpec((TQ, MLA_V), lambda b, h, i: (b * nq + i, h)),
        out_shape=jax.ShapeDtypeStruct((B * T, H * MLA_V), BF16),
        scratch_shapes=[pltpu.VMEM((TQ, LANES), F32), pltpu.VMEM((TQ, 2 * MLA_V), F32),
                        pltpu.VMEM((TQ, TQ), F32), pltpu.VMEM((TQ, TQ), F32)],
        compiler_params=_cparams("parallel", "parallel", "arbitrary"),
        name="mla_flash_attn",
    )(q, k, v)


def _pad_cols(w, mult=LANES):
    pad = -w.shape[1] % mult
    return jnp.pad(w, ((0, 0), (0, pad))) if pad else w


def kernel(x, a_attn_norm, a_w_in, a_q_norm, a_kcmp_norm, a_kslc_norm, a_kwin_norm, a_cmp_pos_k, a_cmp_pos_v, a_cmp_k_w1, a_cmp_k_b1, a_cmp_k_w2, a_cmp_v_w1, a_cmp_v_b1, a_cmp_v_w2, a_w_out, kv_norm, kv_w_a, kv_c_norm, kv_w_b, kv_k_norm, b_attn_norm, b_w_q_a, b_q_a_norm, b_w_q_b, b_q_norm, b_w_out, ffn_norm, ffn_w_gate_up, ffn_w_down):
    B, T, D = x.shape
    n_a = a_w_in.shape[0]
    n_b = b_w_q_a.shape[0]
    xs = x.reshape(B * T, D)

    inv = ROPE_THETA ** (-jnp.arange(0, QK_ROPE, 2, dtype=F32) / QK_ROPE)
    ang = jnp.arange(T, dtype=F32)[:, None] * inv[None, :]
    cos, sin = jnp.cos(ang), jnp.sin(ang)
    cos2 = _rope_layout(jnp.concatenate([cos, cos], axis=-1))
    sin2 = _rope_layout(jnp.concatenate([-sin, sin], axis=-1))

    k_shared = v_shared = None
    for layer in range(n_a + n_b):
        if layer < n_a:
            i = layer
            o = _nsa_mixer(xs, B, T, a_attn_norm[i], _pad_cols(a_w_in[i]).astype(BF16),
                           a_q_norm[i], a_kcmp_norm[i], a_kslc_norm[i], a_kwin_norm[i],
                           a_cmp_pos_k[i], a_cmp_pos_v[i], a_cmp_k_w1[i], a_cmp_k_b1[i], a_cmp_k_w2[i],
                           a_cmp_v_w1[i], a_cmp_v_b1[i], a_cmp_v_w2[i])
            w_out = a_w_out[i]
        else:
            j = layer - n_a
            q = _mla_q(xs, b_attn_norm[j], b_w_q_a[j].astype(BF16), b_q_a_norm[j],
                       _head_layout(b_w_q_b[j]).astype(BF16), b_q_norm[j], cos2, sin2, B, T)
            o = _mla_attn(q, k_shared, v_shared, B, T)
            w_out = b_w_out[j]
        xs = _out_ffn(xs, o, w_out.astype(BF16), ffn_norm[layer],
                      ffn_w_gate_up[layer].astype(BF16), ffn_w_down[layer].astype(BF16))
        if layer == n_a - 1:
            w_kv_a = jnp.concatenate([kv_w_a[:, :KV_LORA], _rope_layout(kv_w_a[:, KV_LORA:])], axis=-1)
            k_shared, v_shared = _mla_kv(xs, kv_norm, w_kv_a.astype(BF16), kv_c_norm, kv_w_b.astype(BF16),
                                         kv_k_norm, cos2, sin2, B, T)
    return xs.reshape(B, T, D)
```

```python
import functools

import jax
import jax.numpy as jnp
from jax import lax
from jax.experimental import pallas as pl
from jax.experimental.pallas import tpu as pltpu

F32 = jnp.float32
BF16 = jnp.bfloat16

NORM_EPS = 1e-6
NEG_INF = -1e30
LANES = 128
SUBLANES = 8

NSA_HEADS = 16
NSA_GROUPS = 4
NSA_HPG = NSA_HEADS // NSA_GROUPS
NSA_DH = 64
CMP_BLOCK = 32
CMP_STRIDE = 16
CMP_HIDDEN = 256
SLC_BLOCK = 64
SLC_TOP = 16
WINDOW = 512
FORCE_SCORE = 1e4
NSA_Q_TILE = 256
NSA_CMP_TILE = 1024
NSA_KEY_TILE = 256
PAIR = 2
LOG2E = 1.4426950408889634

MLA_HEADS = 8
QK_NOPE = 128
QK_ROPE = 64
MLA_QK = QK_NOPE + QK_ROPE
MLA_QK_PAD = QK_NOPE + LANES
MLA_V = 128
Q_LORA = 384
KV_LORA = 256
ROPE_THETA = 10000.0
MLA_TILE = 1024

VMEM_LIMIT = 56 * 1024 * 1024


def _cparams(*sem):
    return pltpu.CompilerParams(dimension_semantics=sem, vmem_limit_bytes=VMEM_LIMIT)


def _rms(x, g):
    return x * lax.rsqrt(jnp.mean(x * x, axis=-1, keepdims=True) + NORM_EPS) * g


def _dot(a, b):
    return jnp.dot(a, b, preferred_element_type=F32)


def _dot_nt(a, b):
    return lax.dot_general(a, b, (((1,), (1,)), ((), ())), preferred_element_type=F32)


def _out_ffn_kernel(x_ref, a_ref, wo_ref, g_ref, wgu_ref, wd_ref, o_ref):
    hid = wd_ref.shape[0]
    x = x_ref[...] + _dot(a_ref[...], wo_ref[...])
    h = _rms(x, g_ref[...]).astype(BF16)
    gate = _dot(h, wgu_ref[:, :hid])
    up = _dot(h, wgu_ref[:, hid:])
    a = (gate * jax.nn.sigmoid(gate) * up).astype(BF16)
    o_ref[...] = x + _dot(a, wd_ref[...])


def _out_ffn(x, mixed, w_out, gain, w_gate_up, w_down, *, tm=512):
    n, d = x.shape
    hid = w_down.shape[0]
    dm = mixed.shape[1]
    assert n % tm == 0 and hid % LANES == 0
    resident = lambda shape: pl.BlockSpec(shape, lambda i: (0, 0), pipeline_mode=pl.Buffered(1))
    return pl.pallas_call(
        _out_ffn_kernel,
        grid=(n // tm,),
        in_specs=[pl.BlockSpec((tm, d), lambda i: (i, 0)),
                  pl.BlockSpec((tm, dm), lambda i: (i, 0)),
                  resident((dm, d)),
                  pl.BlockSpec((1, d), lambda i: (0, 0)),
                  resident((d, 2 * hid)), resident((hid, d))],
        out_specs=pl.BlockSpec((tm, d), lambda i: (i, 0)),
        out_shape=jax.ShapeDtypeStruct((n, d), F32),
        compiler_params=_cparams("parallel"),
        name="out_proj_swiglu_ffn",
    )(x, mixed, w_out, gain.reshape(1, d), w_gate_up, w_down)


def _split2(x):
    hi = x.astype(BF16)
    return hi, (x - hi.astype(F32)).astype(BF16)


def _nsa_in_kernel(x_ref, an_ref, w_ref, gain_ref, gsum_ref, gexp_ref,
                   q_ref, kc_ref, vc_ref, ksl_ref, vsl_ref, kw_ref, vw_ref, gt_ref, rows_sc):
    G, HG, DH = NSA_GROUPS, NSA_HPG, NSA_DH
    qw, kvw = NSA_HEADS * DH, G * DH
    tm = x_ref.shape[0]
    p = _dot(_rms(x_ref[...], an_ref[...]).astype(BF16), w_ref[...])
    ones_col = jnp.ones((tm, LANES - DH), F32)
    n_gate = HG * 3
    gate_pad = jnp.zeros((tm, LANES - DH - n_gate), F32)

    def piece(base, g):
        return p[:, base + g * DH: base + (g + 1) * DH]

    xn = jnp.concatenate([p[:, :qw], p[:, qw + 2 * kvw:qw + 3 * kvw], p[:, qw + 4 * kvw:qw + 5 * kvw]], axis=-1)
    hi, lo = _split2(xn * xn)
    ss = _dot(hi, gsum_ref[...]) + _dot(lo, gsum_ref[...])
    hi, lo = _split2(lax.rsqrt(ss * (1.0 / DH) + NORM_EPS))
    xn = xn * (_dot(hi, gexp_ref[...]) + _dot(lo, gexp_ref[...])) * gain_ref[...]

    for g in range(G):
        for h in range(HG):
            q_ref[0, g, h] = xn[:, (g * HG + h) * DH:(g * HG + h + 1) * DH].astype(BF16)
        for which, (base, out_ref) in enumerate(((qw, kc_ref), (qw + kvw, vc_ref))):
            stage_ref = rows_sc.at[2 * g + which]
            stage_ref[...] = piece(base, g)
            for m in range(0, CMP_STRIDE, 2):
                pair = [stage_ref[pl.ds(m + d, tm // CMP_STRIDE, stride=CMP_STRIDE), :] for d in range(2)]
                out_ref[0, g, :, m * DH:(m + 2) * DH] = jnp.concatenate(pair, axis=-1).astype(BF16)
        ksl_ref[0, g] = xn[:, qw + g * DH:qw + (g + 1) * DH].astype(BF16)
        vsl_ref[0, g] = jnp.concatenate([piece(qw + 3 * kvw, g), ones_col], axis=-1).astype(BF16)
        kw_ref[0, g] = xn[:, qw + kvw + g * DH:qw + kvw + (g + 1) * DH].astype(BF16)
        vw_ref[0, g] = jnp.concatenate([piece(qw + 5 * kvw, g), ones_col], axis=-1).astype(BF16)
        gb = qw + 6 * kvw + g * n_gate
        gt_ref[0, g] = jnp.concatenate([jnp.zeros((tm, DH), F32), jax.nn.sigmoid(p[:, gb:gb + n_gate]),
                                        gate_pad], axis=-1)


def _nsa_in(x, attn_norm, w_in, q_norm, kslc_norm, kwin_norm, B, T, *, tm=512):
    G, HG, DH = NSA_GROUPS, NSA_HPG, NSA_DH
    nt = T // tm
    d, width = w_in.shape
    n_norm = (NSA_HEADS + 2 * G) * DH
    gain = jnp.concatenate([jnp.tile(q_norm, NSA_HEADS) * (DH ** -0.5 * LOG2E),
                            jnp.tile(kslc_norm, G), jnp.tile(kwin_norm, G)]).reshape(1, n_norm)
    gsum = (jnp.arange(n_norm)[:, None] // DH == jnp.arange(LANES)[None, :]).astype(BF16)
    full = lambda shape: pl.BlockSpec(shape, lambda i: (0, 0))
    per_group = lambda last: pl.BlockSpec((1, G, tm, last), lambda i: (i // nt, 0, i % nt, 0))
    shp = lambda last, dt: jax.ShapeDtypeStruct((B, G, T, last), dt)
    half = CMP_STRIDE * DH
    cmp_spec = pl.BlockSpec((1, G, tm // CMP_STRIDE, half), lambda i: (i // nt, 0, i % nt, 0))
    cmp_shape = jax.ShapeDtypeStruct((B, G, T // CMP_STRIDE, half), BF16)
    return pl.pallas_call(
        _nsa_in_kernel,
        grid=(B * nt,),
        in_specs=[pl.BlockSpec((tm, d), lambda i: (i, 0)), full((1, d)), full((d, width)),
                  full((1, n_norm)), full((n_norm, LANES)), full((LANES, n_norm))],
        out_specs=[pl.BlockSpec((1, G, HG, tm, DH), lambda i: (i // nt, 0, 0, i % nt, 0)),
                   cmp_spec, cmp_spec, per_group(DH), per_group(LANES),
                   per_group(DH), per_group(LANES), per_group(LANES)],
        out_shape=[jax.ShapeDtypeStruct((B, G, HG, T, DH), BF16),
                   cmp_shape, cmp_shape, shp(DH, BF16), shp(LANES, BF16),
                   shp(DH, BF16), shp(LANES, BF16), shp(LANES, F32)],
        scratch_shapes=[pltpu.VMEM((2 * G, tm, DH), F32)],
        compiler_params=_cparams("parallel"),
        name="nsa_in_proj_split",
    )(x, attn_norm.reshape(1, d), w_in, gain, gsum, gsum.T)


def _compress_kernel(rk_ref, rv_ref, pek_ref, pev_ref, kw1_ref, kb1_ref, kw2_ref,
                     vw1_ref, vb1_ref, vw2_ref, kn_ref, kcmp_ref, vcmp_ref):
    half = CMP_STRIDE * NSA_DH
    nrow = rk_ref.shape[2]

    def mlp(r_ref, pe_ref, w1_ref, b1_ref, w2_ref):
        r = r_ref[0, 0].astype(BF16)
        ya = _dot(r, w1_ref[:half, :])
        yb = _dot(r, w1_ref[half:, :])
        pe = jnp.broadcast_to(pe_ref[...], (SUBLANES, 2 * half)).astype(BF16)
        c = _dot(pe, w1_ref[...])[0:1] + b1_ref[...]
        hid = ya + pltpu.roll(yb, nrow - 1, 0) + c
        return _dot(jax.nn.gelu(hid).astype(BF16), w2_ref[...])

    kcmp_ref[0, 0] = _rms(mlp(rk_ref, pek_ref, kw1_ref, kb1_ref, kw2_ref), kn_ref[...]).astype(BF16)
    vcmp_ref[0, 0] = mlp(rv_ref, pev_ref, vw1_ref, vb1_ref, vw2_ref).astype(BF16)


def _compress(rk, rv, pe_k, pe_v, k_w1, k_b1, k_w2, v_w1, v_b1, v_w2, kcmp_norm):
    B, G, nrow, half = rk.shape
    DH, HID = NSA_DH, CMP_HIDDEN
    full = lambda shape: pl.BlockSpec(shape, lambda b, g: (0,) * len(shape))
    r_spec = pl.BlockSpec((1, 1, nrow, half), lambda b, g: (b, g, 0, 0))
    o_spec = pl.BlockSpec((1, 1, nrow, DH), lambda b, g: (b, g, 0, 0))
    return pl.pallas_call(
        _compress_kernel,
        grid=(B, G),
        in_specs=[r_spec, r_spec, full((1, 2 * half)), full((1, 2 * half)),
                  full((2 * half, HID)), full((1, HID)), full((HID, DH)),
                  full((2 * half, HID)), full((1, HID)), full((HID, DH)), full((1, DH))],
        out_specs=[o_spec, o_spec],
        out_shape=[jax.ShapeDtypeStruct((B, G, nrow, DH), BF16)] * 2,
        compiler_params=_cparams("parallel", "parallel"),
        name="nsa_compress",
    )(rk, rv, pe_k.reshape(1, -1), pe_v.reshape(1, -1),
      k_w1.astype(BF16), k_b1.reshape(1, HID), k_w2.astype(BF16),
      v_w1.astype(BF16), v_b1.reshape(1, HID), v_w2.astype(BF16), kcmp_norm.reshape(1, DH))


def _split3(x):
    hi = x.astype(BF16)
    r1 = x - hi.astype(F32)
    mid = r1.astype(BF16)
    lo = (r1 - mid.astype(F32)).astype(BF16)
    return hi, mid, lo


def _nsa_cmp_kernel(q_ref, kc_ref, vc_ref, gt_ref, sl_ref, ov_ref, gm_ref,
                    oc_ref, sel_ref, act_ref, *, n_top):
    HG, DH, TQ, TM = NSA_HPG, NSA_DH, NSA_CMP_TILE, NSA_Q_TILE
    s0 = pl.program_id(2) * TQ
    ncmp = kc_ref.shape[2]
    n_slc = ov_ref.shape[1]
    qs = q_ref[0, 0].reshape(HG * TQ, DH)
    gt = gt_ref[0, 0]

    def attend(nc):
        j = lax.broadcasted_iota(jnp.int32, (1, nc), 1)
        t = s0 + lax.broadcasted_iota(jnp.int32, (TQ, 1), 0)
        mask = j * CMP_STRIDE + (CMP_BLOCK - 1) <= t
        mid = (j * CMP_STRIDE - s0).astype(F32) + 0.5 * (CMP_BLOCK - 1)
        has_key = t >= CMP_BLOCK - 1
        s = _dot_nt(qs, kc_ref[0, 0, :nc, :])
        psum = jnp.zeros((TQ, nc), F32)
        ps = []
        for h in range(HG):
            rows = slice(h * TQ, (h + 1) * TQ)
            slope_row = jnp.concatenate([sl_ref[0, h * TM:h * TM + 1, :]] * (nc // LANES), axis=-1)
            sh = jnp.where(mask, s[rows] + slope_row * mid, NEG_INF)
            m = jnp.max(sh, axis=-1, keepdims=True)
            e = jnp.exp2(sh - m)
            l = jnp.sum(e, axis=-1, keepdims=True)
            inv = jnp.where(has_key, 1.0 / l, 0.0)
            p = e * inv
            psum = psum + p
            ps.append(p.astype(BF16))
        o_all = _dot(jnp.concatenate(ps, axis=0), vc_ref[0, 0, :nc, :])
        oc_ref[...] = jnp.concatenate(
            [o_all[h * TQ:(h + 1) * TQ] * gt[:, DH + 3 * h:DH + 3 * h + 1] for h in range(HG)],
            axis=-1).astype(oc_ref.dtype)
        hi, md, lo = _split3(psum)
        ov = ov_ref[:nc, :]
        imp = _dot(hi, ov) + _dot(md, ov) + _dot(lo, ov)
        select(imp, nc * CMP_STRIDE // SLC_BLOCK)

    def select(imp, nb):
        imp_t = imp.T[:nb]
        blk = lax.broadcasted_iota(jnp.int32, (nb, TQ), 0)
        tq = s0 + lax.broadcasted_iota(jnp.int32, (nb, TQ), 1)
        cur = lax.shift_right_logical(tq, SLC_BLOCK.bit_length() - 1)
        forced = (blk == 0) | (blk == cur) | (blk == cur - 1)
        taken = -2.0
        score = jnp.where(forced, taken, jnp.where(blk * SLC_BLOCK <= tq, imp_t, -1.0))

        def pick(_, score):
            mx = jnp.max(score, axis=0, keepdims=True)
            first = jnp.min(jnp.where(score == mx, blk, nb), axis=0, keepdims=True)
            return jnp.where(blk == first, taken, score)

        score = lax.fori_loop(0, n_top - 3, pick, score, unroll=True)
        sel_t = jnp.where(score == taken, 1.0, 0.0)
        if nb < n_slc:
            sel_t = jnp.concatenate([sel_t, jnp.zeros((n_slc - nb, TQ), F32)], axis=0)
        sel = sel_t.T.astype(BF16)
        sel_ref[0, 0] = sel
        for sub in range(TQ // TM):
            cnt = _dot(jnp.ones((SUBLANES, TM), BF16), sel[sub * TM:(sub + 1) * TM])
            act_ref[0, 0, sub] = _dot((cnt > 0.0).astype(BF16), gm_ref[...])

    need = (s0 + TQ) // CMP_STRIDE
    widths = list(range(LANES, ncmp + 1, LANES))
    for idx, nc in enumerate(widths):
        lo_w = widths[idx - 1] if idx else 0
        pl.when((need > lo_w) & (need <= nc))(functools.partial(attend, nc))


def _nsa_cmp(q, kcmp, vcmp, gates, slope_tab, overlap, group_mat, B, T, n_top):
    G, HG, DH, TQ, TM = NSA_GROUPS, NSA_HPG, NSA_DH, NSA_CMP_TILE, NSA_Q_TILE
    nq = T // TQ
    ncmp = kcmp.shape[2]
    n_slc = overlap.shape[1]
    assert n_top >= 3 and ncmp % LANES == 0 and HG < FORCE_SCORE
    return pl.pallas_call(
        functools.partial(_nsa_cmp_kernel, n_top=n_top),
        grid=(B, G, nq),
        in_specs=[pl.BlockSpec((1, 1, HG, TQ, DH), lambda b, g, i: (b, g, 0, i, 0)),
                  pl.BlockSpec((1, 1, ncmp, DH), lambda b, g, i: (b, g, 0, 0)),
                  pl.BlockSpec((1, 1, ncmp, DH), lambda b, g, i: (b, g, 0, 0)),
                  pl.BlockSpec((1, 1, TQ, LANES), lambda b, g, i: (b, g, i, 0)),
                  pl.BlockSpec((1, HG * TM, LANES), lambda b, g, i: (g, 0, 0)),
                  pl.BlockSpec((ncmp, n_slc), lambda b, g, i: (0, 0)),
                  pl.BlockSpec((n_slc, LANES), lambda b, g, i: (0, 0))],
        out_specs=[pl.BlockSpec((TQ, HG * DH), lambda b, g, i: (b * nq + i, g)),
                   pl.BlockSpec((1, 1, TQ, n_slc), lambda b, g, i: (b, g, i, 0)),
                   pl.BlockSpec((1, 1, TQ // TM, SUBLANES, LANES), lambda b, g, i: (b, g, i, 0, 0))],
        out_shape=[jax.ShapeDtypeStruct((B * T, NSA_HEADS * DH), BF16),
                   jax.ShapeDtypeStruct((B, G, T, n_slc), BF16),
                   jax.ShapeDtypeStruct((B, G, T // TM, SUBLANES, LANES), F32)],
        compiler_params=_cparams("parallel", "parallel", "arbitrary"),
        name="nsa_cmp_select",
    )(q, kcmp, vcmp, gates, slope_tab, overlap, group_mat)


def _nsa_main_kernel(list_ref, cnt_ref, q_ref, sel_ref, oc_ref, gt_ref, sl_ref, ex_ref,
                     ksl_ref, vsl_ref, kw_ref, vw_ref, o_ref,
                     m_sc, acc_sc, s0_sc, s1_sc, pk0_sc, pk1_sc, *, n_ktiles):
    HG, DH, TQ, KT = NSA_HPG, NSA_DH, NSA_Q_TILE, NSA_KEY_TILE
    s_slots, pk_slots = (s0_sc, s1_sc), (pk0_sc, pk1_sc)
    WK = WINDOW + TQ
    b, g, qi = pl.program_id(0), pl.program_id(1), pl.program_id(2)
    nq = pl.num_programs(2)
    s0 = qi * TQ
    qs = q_ref[0, 0].reshape(HG * TQ, DH)
    selb = sel_ref[0, 0]
    gt = gt_ref[0, 0]
    t_col = s0 + lax.broadcasted_iota(jnp.int32, (TQ, 1), 0)

    def biased(s, kpos, mask, blk, head=None):
        head = blk if head is None else head
        width = kpos.shape[1]
        slope_row = jnp.concatenate([sl_ref[0, head * TQ:head * TQ + 1, :]] * (width // LANES), axis=-1)
        sh = s[blk * TQ:(blk + 1) * TQ] + slope_row * (kpos - s0).astype(F32)
        return jnp.where(mask, sh, NEG_INF)

    step = (b * NSA_GROUPS + g) * nq + qi
    last = cnt_ref[step] - 1

    def key_start(j):
        return pl.multiple_of(list_ref[step * n_ktiles + j] * KT, KT)

    def scores(j, slot):
        k0 = key_start(j)
        s_slots[slot][...] = _dot_nt(qs, ksl_ref[0, 0, pl.ds(k0, KT), :])
        pk_slots[slot][...] = _dot(selb, ex_ref[:, pl.ds(k0, KT)])

    def update(j, slot, first=False):
        k0 = key_start(j)
        kpos = k0 + lax.broadcasted_iota(jnp.int32, (1, KT), 1)
        mask = (pk_slots[slot][...] > 0.5) & (kpos <= t_col)
        ps, alphas = [], []
        for h in range(HG):
            rows = slice(h * TQ, (h + 1) * TQ)
            sh = biased(s_slots[slot], kpos, mask, h)
            m_new = jnp.broadcast_to(jnp.max(sh, axis=-1, keepdims=True), (TQ, LANES))
            if not first:
                m_old = m_sc[rows]
                m_new = jnp.maximum(m_old, m_new)
                alphas.append(jnp.exp2(m_old - m_new))
            ps.append(jnp.exp2((sh - jnp.concatenate([m_new] * (KT // LANES), axis=-1)).astype(BF16)))
            m_sc[rows] = m_new
        pv = _dot(jnp.concatenate(ps, axis=0), vsl_ref[0, 0, pl.ds(k0, KT), :])
        acc_sc[...] = pv if first else jnp.concatenate(alphas, axis=0) * acc_sc[...] + pv

    def stage(j, slot):
        scores(j + 1, 1 - slot)
        update(j, slot)

    scores(0, 0)

    @pl.when(last == 0)
    def _():
        update(0, 0, first=True)

    @pl.when(last > 0)
    def _():
        scores(1, 1)
        update(0, 0, first=True)
        _two_slot_pipeline(1, last, stage, update)

    start = pl.multiple_of(jnp.maximum(s0 - WINDOW, 0), TQ)
    kpos_w = start + lax.broadcasted_iota(jnp.int32, (1, WK), 1)
    mask_w = (kpos_w <= t_col) & (kpos_w > t_col - WINDOW)
    acc_w = []
    for h0 in range(0, HG, PAIR):
        s_w = _dot_nt(qs[h0 * TQ:(h0 + PAIR) * TQ], kw_ref[0, 0, pl.ds(start, WK), :])
        pw = []
        for h in range(PAIR):
            sh = biased(s_w, kpos_w, mask_w, h, head=h0 + h)
            pw.append(jnp.exp2((sh - jnp.max(sh, axis=-1, keepdims=True)).astype(BF16)))
        acc_w.append(_dot(jnp.concatenate(pw, axis=0), vw_ref[0, 0, pl.ds(start, WK), :]))
    acc_w = jnp.concatenate(acc_w, axis=0)

    o_cmp = oc_ref[...].astype(F32)
    outs = []
    for h in range(HG):
        rows = slice(h * TQ, (h + 1) * TQ)
        a_s, a_w = acc_sc[rows], acc_w[rows]
        r_s, r_w = gt / a_s, gt / a_w
        outs.append(o_cmp[:, h * DH:(h + 1) * DH]
                    + r_s[:, DH + 3 * h + 1:DH + 3 * h + 2] * a_s[:, :DH]
                    + r_w[:, DH + 3 * h + 2:DH + 3 * h + 3] * a_w[:, :DH])
    o_ref[...] = jnp.concatenate(outs, axis=-1).astype(o_ref.dtype)


def _nsa_main(tile_list, tile_count, q, sel, oc, gates, slope_tab, expand, ksl, vsl, kw, vw, B, T):
    G, HG, DH, TQ = NSA_GROUPS, NSA_HPG, NSA_DH, NSA_Q_TILE
    nq = T // TQ
    n_slc = sel.shape[3]
    n_ktiles = T // NSA_KEY_TILE
    kv = lambda last: pl.BlockSpec((1, 1, T, last), lambda b, g, i, tl, tc: (b, g, 0, 0))
    grid_spec = pltpu.PrefetchScalarGridSpec(
        num_scalar_prefetch=2,
        grid=(B, G, nq),
        in_specs=[pl.BlockSpec((1, 1, HG, TQ, DH), lambda b, g, i, tl, tc: (b, g, 0, i, 0)),
                  pl.BlockSpec((1, 1, TQ, n_slc), lambda b, g, i, tl, tc: (b, g, i, 0)),
                  pl.BlockSpec((TQ, HG * DH), lambda b, g, i, tl, tc: (b * nq + i, g)),
                  pl.BlockSpec((1, 1, TQ, LANES), lambda b, g, i, tl, tc: (b, g, i, 0)),
                  pl.BlockSpec((1, HG * TQ, LANES), lambda b, g, i, tl, tc: (g, 0, 0)),
                  pl.BlockSpec((n_slc, T), lambda b, g, i, tl, tc: (0, 0)),
                  kv(DH), kv(LANES), kv(DH), kv(LANES)],
        out_specs=pl.BlockSpec((TQ, HG * DH), lambda b, g, i, tl, tc: (b * nq + i, g)),
        scratch_shapes=[pltpu.VMEM((HG * TQ, LANES), F32), pltpu.VMEM((HG * TQ, LANES), F32),
                        pltpu.VMEM((HG * TQ, NSA_KEY_TILE), F32), pltpu.VMEM((HG * TQ, NSA_KEY_TILE), F32),
                        pltpu.VMEM((TQ, NSA_KEY_TILE), F32), pltpu.VMEM((TQ, NSA_KEY_TILE), F32)],
    )
    return pl.pallas_call(
        functools.partial(_nsa_main_kernel, n_ktiles=n_ktiles),
        grid_spec=grid_spec,
        out_shape=jax.ShapeDtypeStruct((B * T, NSA_HEADS * DH), BF16),
        compiler_params=_cparams("parallel", "parallel", "arbitrary"),
        name="nsa_select_window",
    )(tile_list, tile_count, q, sel, oc, gates, slope_tab, expand, ksl, vsl, kw, vw)


def _nsa_mixer(x, B, T, attn_norm, w_in, q_norm, kcmp_norm, kslc_norm, kwin_norm, pos_k, pos_v,
               k_w1, k_b1, k_w2, v_w1, v_b1, v_w2):
    G, HG, DH, TQ = NSA_GROUPS, NSA_HPG, NSA_DH, NSA_Q_TILE
    assert T % NSA_CMP_TILE == 0 and T % MLA_TILE == 0 and T >= WINDOW + TQ and NSA_KEY_TILE == TQ
    q, kc, vc, ksl, vsl, kw, vw, gates = _nsa_in(x, attn_norm, w_in, q_norm, kslc_norm, kwin_norm, B, T)

    nrow = T // CMP_STRIDE
    kcmp, vcmp = _compress(kc, vc, pos_k, pos_v, k_w1, k_b1, k_w2, v_w1, v_b1, v_w2, kcmp_norm)

    n_slc = T // SLC_BLOCK
    n_top = min(SLC_TOP, n_slc)
    n_slc_pad = max(n_slc, LANES)
    slopes = 2.0 ** (-8.0 * jnp.arange(1, NSA_HEADS + 1, dtype=F32) / NSA_HEADS) * LOG2E
    slope_tab = jnp.broadcast_to(slopes.reshape(G, HG, 1, 1), (G, HG, TQ, LANES)).reshape(G, HG * TQ, LANES)
    cj = jnp.arange(nrow)[:, None] * CMP_STRIDE
    si = jnp.arange(n_slc_pad)[None, :] * SLC_BLOCK
    overlap = ((cj <= si + SLC_BLOCK - 1) & (cj + CMP_BLOCK - 1 >= si)
               & (jnp.arange(nrow)[:, None] < nrow - 1)).astype(BF16)
    blocks_per_tile = NSA_KEY_TILE // SLC_BLOCK
    group_mat = (jnp.arange(n_slc_pad)[:, None] // blocks_per_tile
                 == jnp.arange(LANES)[None, :]).astype(BF16)
    expand = (jnp.arange(n_slc_pad)[:, None] == jnp.arange(T)[None, :] // SLC_BLOCK).astype(BF16)

    oc, sel, act = _nsa_cmp(q, kcmp, vcmp, gates, slope_tab, overlap, group_mat, B, T, n_top)
    n_ktiles = T // NSA_KEY_TILE
    tile = jnp.arange(n_ktiles)
    reach = tile[None, :] * NSA_KEY_TILE < (jnp.arange(T // TQ)[:, None] + 1) * TQ
    visit = ((act[:, :, :, 0, :n_ktiles] > 0.0) & reach) | (tile == 0)
    slot = jnp.cumsum(visit, axis=-1) - 1
    tile_list = jnp.sum(jnp.where(visit[..., :, None] & (slot[..., :, None] == tile), tile[:, None], 0), axis=-2)
    tile_count = jnp.sum(visit, axis=-1)
    return _nsa_main(tile_list.astype(jnp.int32).reshape(-1), tile_count.astype(jnp.int32).reshape(-1),
                     q, sel, oc, gates, slope_tab, expand, ksl, vsl, kw, vw, B, T)


def _rope_layout(w):
    half = QK_ROPE // 2
    z = jnp.zeros(w.shape[:-1] + (LANES // 2 - half,), w.dtype)
    return jnp.concatenate([w[..., :half], z, w[..., half:], z], axis=-1)


def _head_layout(w):
    w = w.reshape(w.shape[:-1] + (MLA_HEADS, MLA_QK))
    w = jnp.concatenate([w[..., :QK_NOPE], _rope_layout(w[..., QK_NOPE:])], axis=-1)
    return w.reshape(w.shape[:-2] + (MLA_HEADS * MLA_QK_PAD,))


def _roped(rope, gr_ref, cos_ref, sin_ref):
    r = rope * gr_ref[...]
    return r * cos_ref[...] + pltpu.roll(r, LANES // 2, 1) * sin_ref[...]


def _head_inv_rms(nope, rope_sq):
    ss = jnp.sum(nope * nope + rope_sq, axis=-1, keepdims=True)
    return lax.rsqrt(ss * (1.0 / MLA_QK) + NORM_EPS)


def _mla_q_kernel(x_ref, an_ref, wa_ref, qan_ref, wb_ref, gn_ref, gr_ref, cos_ref, sin_ref, o_ref):
    qa = _dot(_rms(x_ref[...], an_ref[...]).astype(BF16), wa_ref[...])
    q = _dot(_rms(qa, qan_ref[...]).astype(BF16), wb_ref[...])
    for h in range(MLA_HEADS):
        nope = q[:, h * MLA_QK_PAD:h * MLA_QK_PAD + QK_NOPE]
        rope = q[:, h * MLA_QK_PAD + QK_NOPE:(h + 1) * MLA_QK_PAD]
        inv = _head_inv_rms(nope, rope * rope)
        o_ref[0, h] = jnp.concatenate([nope * inv * gn_ref[...], _roped(rope * inv, gr_ref, cos_ref, sin_ref)],
                                      axis=-1).astype(BF16)


def _mla_kv_kernel(x_ref, n_ref, wa_ref, cn_ref, wb_ref, gn_ref, gr_ref, cos_ref, sin_ref, k_ref, v_ref):
    kv_a = _dot(_rms(x_ref[...], n_ref[...]).astype(BF16), wa_ref[...])
    kv = _dot(_rms(kv_a[:, :KV_LORA], cn_ref[...]).astype(BF16), wb_ref[...])
    rope = kv_a[:, KV_LORA:]
    rope_sq = rope * rope
    roped = _roped(rope, gr_ref, cos_ref, sin_ref)
    for h in range(MLA_HEADS):
        base = h * (QK_NOPE + MLA_V)
        nope = kv[:, base:base + QK_NOPE]
        inv = _head_inv_rms(nope, rope_sq)
        k_ref[0, h] = jnp.concatenate([nope * inv * gn_ref[...], inv * roped], axis=-1).astype(BF16)
        v = kv[:, base + QK_NOPE:base + QK_NOPE + MLA_V]
        v_ref[0, h] = jnp.concatenate([v, jnp.ones_like(v)], axis=-1).astype(BF16)


def _mla_specs(tm, nt, d, weights):
    full = lambda shape: pl.BlockSpec(shape, lambda i: (0, 0))
    row = lambda: pl.BlockSpec((1, LANES), lambda i: (0, 0))
    tab = lambda: pl.BlockSpec((tm, LANES), lambda i: (i % nt, 0))
    return ([pl.BlockSpec((tm, d), lambda i: (i, 0))] + [full(w.shape) for w in weights]
            + [row(), row(), tab(), tab()])


def _mla_q(x, attn_norm, w_q_a, q_a_norm, w_q_b, gain, cos_l, sin_l, B, T, *, tm=512):
    nt = T // tm
    d = x.shape[1]
    weights = (attn_norm.reshape(1, d), w_q_a, q_a_norm.reshape(1, -1), w_q_b)
    gain = gain * (MLA_QK ** -0.5 * LOG2E)
    return pl.pallas_call(
        _mla_q_kernel,
        grid=(B * nt,),
        in_specs=_mla_specs(tm, nt, d, weights),
        out_specs=pl.BlockSpec((1, MLA_HEADS, tm, MLA_QK_PAD), lambda i: (i // nt, 0, i % nt, 0)),
        out_shape=jax.ShapeDtypeStruct((B, MLA_HEADS, T, MLA_QK_PAD), BF16),
        compiler_params=_cparams("parallel"),
        name="mla_q_path",
    )(x, *weights, gain[:QK_NOPE].reshape(1, LANES), _rope_layout(gain[QK_NOPE:]).reshape(1, LANES),
      cos_l, sin_l)


def _mla_kv(x, kv_norm, w_kv_a, kv_c_norm, w_kv_b, gain, cos_l, sin_l, B, T, *, tm=512):
    nt = T // tm
    d = x.shape[1]
    weights = (kv_norm.reshape(1, d), w_kv_a, kv_c_norm.reshape(1, -1), w_kv_b)
    return pl.pallas_call(
        _mla_kv_kernel,
        grid=(B * nt,),
        in_specs=_mla_specs(tm, nt, d, weights),
        out_specs=[pl.BlockSpec((1, MLA_HEADS, tm, MLA_QK_PAD), lambda i: (i // nt, 0, i % nt, 0)),
                   pl.BlockSpec((1, MLA_HEADS, tm, 2 * MLA_V), lambda i: (i // nt, 0, i % nt, 0))],
        out_shape=[jax.ShapeDtypeStruct((B, MLA_HEADS, T, MLA_QK_PAD), BF16),
                   jax.ShapeDtypeStruct((B, MLA_HEADS, T, 2 * MLA_V), BF16)],
        compiler_params=_cparams("parallel"),
        name="mla_kv_path",
    )(x, *weights, gain[:QK_NOPE].reshape(1, LANES), _rope_layout(gain[QK_NOPE:]).reshape(1, LANES),
      cos_l, sin_l)


def _two_slot_pipeline(first, last, stage, finish):
    a = first % 2
    n = last - first

    def pair(j, carry):
        stage(first + 2 * j, a)
        stage(first + 2 * j + 1, 1 - a)
        return carry

    lax.fori_loop(0, n // 2, pair, 0)

    @pl.when(n % 2 == 1)
    def _():
        stage(last - 1, a)
        finish(last, 1 - a)

    @pl.when(n % 2 == 0)
    def _():
        finish(last, a)


def _mla_attn_kernel(q_ref, k_ref, v_ref, o_ref, m_sc, acc_sc, s0_sc, s1_sc):
    TQ = MLA_TILE
    qi = pl.program_id(2)
    q = q_ref[0, 0]
    s_slots = (s0_sc, s1_sc)
    m_sc[...] = jnp.full_like(m_sc, NEG_INF)
    acc_sc[...] = jnp.zeros_like(acc_sc)

    def scores(i, slot):
        k0 = pl.multiple_of(i * TQ, TQ)
        s_slots[slot][...] = _dot_nt(q, k_ref[0, 0, pl.ds(k0, TQ), :])

    def update(i, slot, masked):
        k0 = pl.multiple_of(i * TQ, TQ)

        def load():
            s = s_slots[slot][...]
            if masked:
                qq = lax.broadcasted_iota(jnp.int32, (TQ, TQ), 0)
                kk = lax.broadcasted_iota(jnp.int32, (TQ, TQ), 1)
                s = jnp.where(kk <= qq, s, NEG_INF)
            return s

        m_old = m_sc[...]
        m_new = jnp.maximum(m_old, jnp.max(load(), axis=-1, keepdims=True))
        p = jnp.exp2((load() - jnp.concatenate([m_new] * (TQ // LANES), axis=-1)).astype(BF16))
        alpha = jnp.exp2(m_old - m_new)
        acc_sc[...] = (jnp.concatenate([alpha] * (acc_sc.shape[1] // LANES), axis=-1) * acc_sc[...]
                       + _dot(p, v_ref[0, 0, pl.ds(k0, TQ), :]))
        m_sc[...] = m_new

    def stage(i, slot):
        scores(i + 1, 1 - slot)
        update(i, slot, False)

    scores(0, 0)
    _two_slot_pipeline(0, qi, stage, lambda i, slot: update(i, slot, True))
    o_ref[...] = (acc_sc[:, :MLA_V] / acc_sc[:, MLA_V:]).astype(o_ref.dtype)


def _mla_attn(q, k, v, B, T):
    H, TQ = MLA_HEADS, MLA_TILE
    nq = T // TQ
    return pl.pallas_call(
        _mla_attn_kernel,
        grid=(B, H, nq),
        in_specs=[pl.BlockSpec((1, 1, TQ, MLA_QK_PAD), lambda b, h, i: (b, h, i, 0)),
                  pl.BlockSpec((1, 1, T, MLA_QK_PAD), lambda b, h, i: (b, h, 0, 0)),
                  pl.BlockSpec((1, 1, T, 2 * MLA_V), lambda b, h, i: (b, h, 0, 0))],
        out_specs=pl.BlockSpec((TQ, MLA_V), lambda b, h, i: (b * nq + i, h)),
        out_shape=jax.ShapeDtypeStruct((B * T, H * MLA_V), BF16),
        scratch_shapes=[pltpu.VMEM((TQ, LANES), F32), pltpu.VMEM((TQ, 2 * MLA_V), F32),
                        pltpu.VMEM((TQ, TQ), F32), pltpu.VMEM((TQ, TQ), F32)],
        compiler_params=_cparams("parallel", "parallel", "arbitrary"),
        name="mla_flash_attn",
    )(q, k, v)


def _pad_cols(w, mult=LANES):
    pad = -w.shape[1] % mult
    return jnp.pad(w, ((0, 0), (0, pad))) if pad else w


def kernel(x, a_attn_norm, a_w_in, a_q_norm, a_kcmp_norm, a_kslc_norm, a_kwin_norm, a_cmp_pos_k, a_cmp_pos_v, a_cmp_k_w1, a_cmp_k_b1, a_cmp_k_w2, a_cmp_v_w1, a_cmp_v_b1, a_cmp_v_w2, a_w_out, kv_norm, kv_w_a, kv_c_norm, kv_w_b, kv_k_norm, b_attn_norm, b_w_q_a, b_q_a_norm, b_w_q_b, b_q_norm, b_w_out, ffn_norm, ffn_w_gate_up, ffn_w_down):
    B, T, D = x.shape
    n_a = a_w_in.shape[0]
    n_b = b_w_q_a.shape[0]
    xs = x.reshape(B * T, D)

    inv = ROPE_THETA ** (-jnp.arange(0, QK_ROPE, 2, dtype=F32) / QK_ROPE)
    ang = jnp.arange(T, dtype=F32)[:, None] * inv[None, :]
    cos, sin = jnp.cos(ang), jnp.sin(ang)
    cos2 = _rope_layout(jnp.concatenate([cos, cos], axis=-1))
    sin2 = _rope_layout(jnp.concatenate([-sin, sin], axis=-1))

    k_shared = v_shared = None
    for layer in range(n_a + n_b):
        if layer < n_a:
            i = layer
            o = _nsa_mixer(xs, B, T, a_attn_norm[i], _pad_cols(a_w_in[i]).astype(BF16),
                           a_q_norm[i], a_kcmp_norm[i], a_kslc_norm[i], a_kwin_norm[i],
                           a_cmp_pos_k[i], a_cmp_pos_v[i], a_cmp_k_w1[i], a_cmp_k_b1[i], a_cmp_k_w2[i],
                           a_cmp_v_w1[i], a_cmp_v_b1[i], a_cmp_v_w2[i])
            w_out = a_w_out[i]
        else:
            j = layer - n_a
            q = _mla_q(xs, b_attn_norm[j], b_w_q_a[j].astype(BF16), b_q_a_norm[j],
                       _head_layout(b_w_q_b[j]).astype(BF16), b_q_norm[j], cos2, sin2, B, T)
            o = _mla_attn(q, k_shared, v_shared, B, T)
            w_out = b_w_out[j]
        xs = _out_ffn(xs, o, w_out.astype(BF16), ffn_norm[layer],
                      ffn_w_gate_up[layer].astype(BF16), ffn_w_down[layer].astype(BF16))
        if layer == n_a - 1:
            w_kv_a = jnp.concatenate([kv_w_a[:, :KV_LORA], _rope_layout(kv_w_a[:, KV_LORA:])], axis=-1)
            k_shared, v_shared = _mla_kv(xs, kv_norm, w_kv_a.astype(BF16), kv_c_norm, kv_w_b.astype(BF16),
                                         kv_k_norm, cos2, sin2, B, T)
    return xs.reshape(B, T, D)
```

```python
import functools

import jax
import jax.numpy as jnp
from jax import lax
from jax.experimental import pallas as pl
from jax.experimental.pallas import tpu as pltpu

F32 = jnp.float32
BF16 = jnp.bfloat16

NORM_EPS = 1e-6
NEG_INF = -1e30
LANES = 128
SUBLANES = 8

NSA_HEADS = 16
NSA_GROUPS = 4
NSA_HPG = NSA_HEADS // NSA_GROUPS
NSA_DH = 64
CMP_BLOCK = 32
CMP_STRIDE = 16
CMP_HIDDEN = 256
SLC_BLOCK = 64
SLC_TOP = 16
WINDOW = 512
FORCE_SCORE = 1e4
NSA_Q_TILE = 256
NSA_CMP_TILE = 1024
NSA_KEY_TILE = 256
PAIR = 2
LOG2E = 1.4426950408889634

MLA_HEADS = 8
QK_NOPE = 128
QK_ROPE = 64
MLA_QK = QK_NOPE + QK_ROPE
MLA_QK_PAD = QK_NOPE + LANES
MLA_V = 128
Q_LORA = 384
KV_LORA = 256
ROPE_THETA = 10000.0
MLA_TILE = 1024

VMEM_LIMIT = 56 * 1024 * 1024


def _cparams(*sem):
    return pltpu.CompilerParams(dimension_semantics=sem, vmem_limit_bytes=VMEM_LIMIT)


def _rms(x, g):
    return x * lax.rsqrt(jnp.mean(x * x, axis=-1, keepdims=True) + NORM_EPS) * g


def _dot(a, b):
    return jnp.dot(a, b, preferred_element_type=F32)


def _dot_nt(a, b):
    return lax.dot_general(a, b, (((1,), (1,)), ((), ())), preferred_element_type=F32)


def _out_ffn_kernel(x_ref, a_ref, wo_ref, g_ref, wgu_ref, wd_ref, o_ref):
    hid = wd_ref.shape[0]
    x = x_ref[...] + _dot(a_ref[...], wo_ref[...])
    h = _rms(x, g_ref[...]).astype(BF16)
    gate = _dot(h, wgu_ref[:, :hid])
    up = _dot(h, wgu_ref[:, hid:])
    a = (gate * jax.nn.sigmoid(gate) * up).astype(BF16)
    o_ref[...] = x + _dot(a, wd_ref[...])


def _out_ffn(x, mixed, w_out, gain, w_gate_up, w_down, *, tm=512):
    n, d = x.shape
    hid = w_down.shape[0]
    dm = mixed.shape[1]
    assert n % tm == 0 and hid % LANES == 0
    resident = lambda shape: pl.BlockSpec(shape, lambda i: (0, 0), pipeline_mode=pl.Buffered(1))
    return pl.pallas_call(
        _out_ffn_kernel,
        grid=(n // tm,),
        in_specs=[pl.BlockSpec((tm, d), lambda i: (i, 0)),
                  pl.BlockSpec((tm, dm), lambda i: (i, 0)),
                  resident((dm, d)),
                  pl.BlockSpec((1, d), lambda i: (0, 0)),
                  resident((d, 2 * hid)), resident((hid, d))],
        out_specs=pl.BlockSpec((tm, d), lambda i: (i, 0)),
        out_shape=jax.ShapeDtypeStruct((n, d), F32),
        compiler_params=_cparams("parallel"),
        name="out_proj_swiglu_ffn",
    )(x, mixed, w_out, gain.reshape(1, d), w_gate_up, w_down)


def _split2(x):
    hi = x.astype(BF16)
    return hi, (x - hi.astype(F32)).astype(BF16)


def _nsa_in_kernel(x_ref, an_ref, w_ref, gain_ref, gsum_ref, gexp_ref,
                   q_ref, kc_ref, vc_ref, ksl_ref, vsl_ref, kw_ref, vw_ref, gt_ref, rows_sc):
    G, HG, DH = NSA_GROUPS, NSA_HPG, NSA_DH
    qw, kvw = NSA_HEADS * DH, G * DH
    tm = x_ref.shape[0]
    p = _dot(_rms(x_ref[...], an_ref[...]).astype(BF16), w_ref[...])
    ones_col = jnp.ones((tm, LANES - DH), F32)
    n_gate = HG * 3
    gate_pad = jnp.zeros((tm, LANES - DH - n_gate), F32)

    def piece(base, g):
        return p[:, base + g * DH: base + (g + 1) * DH]

    xn = jnp.concatenate([p[:, :qw], p[:, qw + 2 * kvw:qw + 3 * kvw], p[:, qw + 4 * kvw:qw + 5 * kvw]], axis=-1)
    hi, lo = _split2(xn * xn)
    ss = _dot(hi, gsum_ref[...]) + _dot(lo, gsum_ref[...])
    hi, lo = _split2(lax.rsqrt(ss * (1.0 / DH) + NORM_EPS))
    xn = xn * (_dot(hi, gexp_ref[...]) + _dot(lo, gexp_ref[...])) * gain_ref[...]

    for g in range(G):
        for h in range(HG):
            q_ref[0, g, h] = xn[:, (g * HG + h) * DH:(g * HG + h + 1) * DH].astype(BF16)
        for which, (base, out_ref) in enumerate(((qw, kc_ref), (qw + kvw, vc_ref))):
            stage_ref = rows_sc.at[2 * g + which]
            stage_ref[...] = piece(base, g)
            for m in range(0, CMP_STRIDE, 2):
                pair = [stage_ref[pl.ds(m + d, tm // CMP_STRIDE, stride=CMP_STRIDE), :] for d in range(2)]
                out_ref[0, g, :, m * DH:(m + 2) * DH] = jnp.concatenate(pair, axis=-1).astype(BF16)
        ksl_ref[0, g] = xn[:, qw + g * DH:qw + (g + 1) * DH].astype(BF16)
        vsl_ref[0, g] = jnp.concatenate([piece(qw + 3 * kvw, g), ones_col], axis=-1).astype(BF16)
        kw_ref[0, g] = xn[:, qw + kvw + g * DH:qw + kvw + (g + 1) * DH].astype(BF16)
        vw_ref[0, g] = jnp.concatenate([piece(qw + 5 * kvw, g), ones_col], axis=-1).astype(BF16)
        gb = qw + 6 * kvw + g * n_gate
        gt_ref[0, g] = jnp.concatenate([jnp.zeros((tm, DH), F32), jax.nn.sigmoid(p[:, gb:gb + n_gate]),
                                        gate_pad], axis=-1)


def _nsa_in(x, attn_norm, w_in, q_norm, kslc_norm, kwin_norm, B, T, *, tm=512):
    G, HG, DH = NSA_GROUPS, NSA_HPG, NSA_DH
    nt = T // tm
    d, width = w_in.shape
    n_norm = (NSA_HEADS + 2 * G) * DH
    gain = jnp.concatenate([jnp.tile(q_norm, NSA_HEADS) * (DH ** -0.5 * LOG2E),
                            jnp.tile(kslc_norm, G), jnp.tile(kwin_norm, G)]).reshape(1, n_norm)
    gsum = (jnp.arange(n_norm)[:, None] // DH == jnp.arange(LANES)[None, :]).astype(BF16)
    full = lambda shape: pl.BlockSpec(shape, lambda i: (0, 0))
    per_group = lambda last: pl.BlockSpec((1, G, tm, last), lambda i: (i // nt, 0, i % nt, 0))
    shp = lambda last, dt: jax.ShapeDtypeStruct((B, G, T, last), dt)
    half = CMP_STRIDE * DH
    cmp_spec = pl.BlockSpec((1, G, tm // CMP_STRIDE, half), lambda i: (i // nt, 0, i % nt, 0))
    cmp_shape = jax.ShapeDtypeStruct((B, G, T // CMP_STRIDE, half), BF16)
    return pl.pallas_call(
        _nsa_in_kernel,
        grid=(B * nt,),
        in_specs=[pl.BlockSpec((tm, d), lambda i: (i, 0)), full((1, d)), full((d, width)),
                  full((1, n_norm)), full((n_norm, LANES)), full((LANES, n_norm))],
        out_specs=[pl.BlockSpec((1, G, HG, tm, DH), lambda i: (i // nt, 0, 0, i % nt, 0)),
                   cmp_spec, cmp_spec, per_group(DH), per_group(LANES),
                   per_group(DH), per_group(LANES), per_group(LANES)],
        out_shape=[jax.ShapeDtypeStruct((B, G, HG, T, DH), BF16),
                   cmp_shape, cmp_shape, shp(DH, BF16), shp(LANES, BF16),
                   shp(DH, BF16), shp(LANES, BF16), shp(LANES, F32)],
        scratch_shapes=[pltpu.VMEM((2 * G, tm, DH), F32)],
        compiler_params=_cparams("parallel"),
        name="nsa_in_proj_split",
    )(x, attn_norm.reshape(1, d), w_in, gain, gsum, gsum.T)


def _compress_kernel(rk_ref, rv_ref, pek_ref, pev_ref, kw1_ref, kb1_ref, kw2_ref,
                     vw1_ref, vb1_ref, vw2_ref, kn_ref, kcmp_ref, vcmp_ref):
    half = CMP_STRIDE * NSA_DH
    nrow = rk_ref.shape[2]

    def mlp(r_ref, pe_ref, w1_ref, b1_ref, w2_ref):
        r = r_ref[0, 0].astype(BF16)
        ya = _dot(r, w1_ref[:half, :])
        yb = _dot(r, w1_ref[half:, :])
        pe = jnp.broadcast_to(pe_ref[...], (SUBLANES, 2 * half)).astype(BF16)
        c = _dot(pe, w1_ref[...])[0:1] + b1_ref[...]
        hid = ya + pltpu.roll(yb, nrow - 1, 0) + c
        return _dot(jax.nn.gelu(hid).astype(BF16), w2_ref[...])

    kcmp_ref[0, 0] = _rms(mlp(rk_ref, pek_ref, kw1_ref, kb1_ref, kw2_ref), kn_ref[...]).astype(BF16)
    vcmp_ref[0, 0] = mlp(rv_ref, pev_ref, vw1_ref, vb1_ref, vw2_ref).astype(BF16)


def _compress(rk, rv, pe_k, pe_v, k_w1, k_b1, k_w2, v_w1, v_b1, v_w2, kcmp_norm):
    B, G, nrow, half = rk.shape
    DH, HID = NSA_DH, CMP_HIDDEN
    full = lambda shape: pl.BlockSpec(shape, lambda b, g: (0,) * len(shape))
    r_spec = pl.BlockSpec((1, 1, nrow, half), lambda b, g: (b, g, 0, 0))
    o_spec = pl.BlockSpec((1, 1, nrow, DH), lambda b, g: (b, g, 0, 0))
    return pl.pallas_call(
        _compress_kernel,
        grid=(B, G),
        in_specs=[r_spec, r_spec, full((1, 2 * half)), full((1, 2 * half)),
                  full((2 * half, HID)), full((1, HID)), full((HID, DH)),
                  full((2 * half, HID)), full((1, HID)), full((HID, DH)), full((1, DH))],
        out_specs=[o_spec, o_spec],
        out_shape=[jax.ShapeDtypeStruct((B, G, nrow, DH), BF16)] * 2,
        compiler_params=_cparams("parallel", "parallel"),
        name="nsa_compress",
    )(rk, rv, pe_k.reshape(1, -1), pe_v.reshape(1, -1),
      k_w1.astype(BF16), k_b1.reshape(1, HID), k_w2.astype(BF16),
      v_w1.astype(BF16), v_b1.reshape(1, HID), v_w2.astype(BF16), kcmp_norm.reshape(1, DH))


def _split3(x):
    hi = x.astype(BF16)
    r1 = x - hi.astype(F32)
    mid = r1.astype(BF16)
    lo = (r1 - mid.astype(F32)).astype(BF16)
    return hi, mid, lo


def _nsa_cmp_kernel(q_ref, kc_ref, vc_ref, gt_ref, sl_ref, ov_ref, gm_ref,
                    oc_ref, sel_ref, act_ref, *, n_top):
    HG, DH, TQ, TM = NSA_HPG, NSA_DH, NSA_CMP_TILE, NSA_Q_TILE
    s0 = pl.program_id(2) * TQ
    ncmp = kc_ref.shape[2]
    n_slc = ov_ref.shape[1]
    qs = q_ref[0, 0].reshape(HG * TQ, DH)
    gt = gt_ref[0, 0]

    def attend(nc):
        j = lax.broadcasted_iota(jnp.int32, (1, nc), 1)
        t = s0 + lax.broadcasted_iota(jnp.int32, (TQ, 1), 0)
        mask = j * CMP_STRIDE + (CMP_BLOCK - 1) <= t
        mid = (j * CMP_STRIDE - s0).astype(F32) + 0.5 * (CMP_BLOCK - 1)
        has_key = t >= CMP_BLOCK - 1
        s = _dot_nt(qs, kc_ref[0, 0, :nc, :])
        psum = jnp.zeros((TQ, nc), F32)
        ps = []
        for h in range(HG):
            rows = slice(h * TQ, (h + 1) * TQ)
            slope_row = jnp.concatenate([sl_ref[0, h * TM:h * TM + 1, :]] * (nc // LANES), axis=-1)
            sh = jnp.where(mask, s[rows] + slope_row * mid, NEG_INF)
            m = jnp.max(sh, axis=-1, keepdims=True)
            e = jnp.exp2(sh - m)
            l = jnp.sum(e, axis=-1, keepdims=True)
            inv = jnp.where(has_key, 1.0 / l, 0.0)
            p = e * inv
            psum = psum + p
            ps.append(p.astype(BF16))
        o_all = _dot(jnp.concatenate(ps, axis=0), vc_ref[0, 0, :nc, :])
        oc_ref[...] = jnp.concatenate(
            [o_all[h * TQ:(h + 1) * TQ] * gt[:, DH + 3 * h:DH + 3 * h + 1] for h in range(HG)],
            axis=-1).astype(oc_ref.dtype)
        hi, md, lo = _split3(psum)
        ov = ov_ref[:nc, :]
        imp = _dot(hi, ov) + _dot(md, ov) + _dot(lo, ov)
        select(imp, nc * CMP_STRIDE // SLC_BLOCK)

    def select(imp, nb):
        imp_t = imp.T[:nb]
        blk = lax.broadcasted_iota(jnp.int32, (nb, TQ), 0)
        tq = s0 + lax.broadcasted_iota(jnp.int32, (nb, TQ), 1)
        cur = lax.shift_right_logical(tq, SLC_BLOCK.bit_length() - 1)
        forced = (blk == 0) | (blk == cur) | (blk == cur - 1)
        taken = -2.0
        score = jnp.where(forced, taken, jnp.where(blk * SLC_BLOCK <= tq, imp_t, -1.0))

        def pick(_, score):
            mx = jnp.max(score, axis=0, keepdims=True)
            first = jnp.min(jnp.where(score == mx, blk, nb), axis=0, keepdims=True)
            return jnp.where(blk == first, taken, score)

        score = lax.fori_loop(0, n_top - 3, pick, score, unroll=True)
        sel_t = jnp.where(score == taken, 1.0, 0.0)
        if nb < n_slc:
            sel_t = jnp.concatenate([sel_t, jnp.zeros((n_slc - nb, TQ), F32)], axis=0)
        sel = sel_t.T.astype(BF16)
        sel_ref[0, 0] = sel
        for sub in range(TQ // TM):
            cnt = _dot(jnp.ones((SUBLANES, TM), BF16), sel[sub * TM:(sub + 1) * TM])
            act_ref[0, 0, sub] = _dot((cnt > 0.0).astype(BF16), gm_ref[...])

    need = (s0 + TQ) // CMP_STRIDE
    widths = list(range(LANES, ncmp + 1, LANES))
    for idx, nc in enumerate(widths):
        lo_w = widths[idx - 1] if idx else 0
        pl.when((need > lo_w) & (need <= nc))(functools.partial(attend, nc))


def _nsa_cmp(q, kcmp, vcmp, gates, slope_tab, overlap, group_mat, B, T, n_top):
    G, HG, DH, TQ, TM = NSA_GROUPS, NSA_HPG, NSA_DH, NSA_CMP_TILE, NSA_Q_TILE
    nq = T // TQ
    ncmp = kcmp.shape[2]
    n_slc = overlap.shape[1]
    assert n_top >= 3 and ncmp % LANES == 0 and HG < FORCE_SCORE
    return pl.pallas_call(
        functools.partial(_nsa_cmp_kernel, n_top=n_top),
        grid=(B, G, nq),
        in_specs=[pl.BlockSpec((1, 1, HG, TQ, DH), lambda b, g, i: (b, g, 0, i, 0)),
                  pl.BlockSpec((1, 1, ncmp, DH), lambda b, g, i: (b, g, 0, 0)),
                  pl.BlockSpec((1, 1, ncmp, DH), lambda b, g, i: (b, g, 0, 0)),
                  pl.BlockSpec((1, 1, TQ, LANES), lambda b, g, i: (b, g, i, 0)),
                  pl.BlockSpec((1, HG * TM, LANES), lambda b, g, i: (g, 0, 0)),
                  pl.BlockSpec((ncmp, n_slc), lambda b, g, i: (0, 0)),
                  pl.BlockSpec((n_slc, LANES), lambda b, g, i: (0, 0))],
        out_specs=[pl.BlockSpec((TQ, HG * DH), lambda b, g, i: (b * nq + i, g)),
                   pl.BlockSpec((1, 1, TQ, n_slc), lambda b, g, i: (b, g, i, 0)),
                   pl.BlockSpec((1, 1, TQ // TM, SUBLANES, LANES), lambda b, g, i: (b, g, i, 0, 0))],
        out_shape=[jax.ShapeDtypeStruct((B * T, NSA_HEADS * DH), BF16),
                   jax.ShapeDtypeStruct((B, G, T, n_slc), BF16),
                   jax.ShapeDtypeStruct((B, G, T // TM, SUBLANES, LANES), F32)],
        compiler_params=_cparams("parallel", "parallel", "arbitrary"),
        name="nsa_cmp_select",
    )(q, kcmp, vcmp, gates, slope_tab, overlap, group_mat)


def _nsa_main_kernel(list_ref, cnt_ref, q_ref, sel_ref, oc_ref, gt_ref, sl_ref, ex_ref,
                     ksl_ref, vsl_ref, kw_ref, vw_ref, o_ref,
                     m_sc, acc_sc, s0_sc, s1_sc, *, n_ktiles):
    HG, DH, TQ, KT = NSA_HPG, NSA_DH, NSA_Q_TILE, NSA_KEY_TILE
    s_slots = (s0_sc, s1_sc)
    WK = WINDOW + TQ
    b, g, qi = pl.program_id(0), pl.program_id(1), pl.program_id(2)
    nq = pl.num_programs(2)
    s0 = qi * TQ
    qs = q_ref[0, 0].reshape(HG * TQ, DH)
    selb = sel_ref[0, 0]
    gt = gt_ref[0, 0]
    t_col = s0 + lax.broadcasted_iota(jnp.int32, (TQ, 1), 0)

    def biased(s, kpos, mask, blk, head=None):
        head = blk if head is None else head
        width = kpos.shape[1]
        slope_row = jnp.concatenate([sl_ref[0, head * TQ:head * TQ + 1, :]] * (width // LANES), axis=-1)
        sh = s[blk * TQ:(blk + 1) * TQ] + slope_row * (kpos - s0).astype(F32)
        return jnp.where(mask, sh, NEG_INF)

    step = (b * NSA_GROUPS + g) * nq + qi
    last = cnt_ref[step] - 1

    def key_start(j):
        return pl.multiple_of(list_ref[step * n_ktiles + j] * KT, KT)

    def scores(j, slot):
        k0 = key_start(j)
        kpos = k0 + lax.broadcasted_iota(jnp.int32, (1, KT), 1)
        s = _dot_nt(qs, ksl_ref[0, 0, pl.ds(k0, KT), :])
        picked = _dot(selb, ex_ref[:, pl.ds(k0, KT)])
        mask = (picked > 0.5) & (kpos <= t_col)
        for h in range(HG):
            s_slots[slot][h * TQ:(h + 1) * TQ] = biased(s, kpos, mask, h)

    def update(j, slot, first=False):
        k0 = key_start(j)
        ps, alphas = [], []
        for h in range(HG):
            rows = slice(h * TQ, (h + 1) * TQ)
            m_new = jnp.broadcast_to(jnp.max(s_slots[slot][rows], axis=-1, keepdims=True), (TQ, LANES))
            if not first:
                m_old = m_sc[rows]
                m_new = jnp.maximum(m_old, m_new)
                alphas.append(jnp.exp2(m_old - m_new))
            ps.append(jnp.exp2((s_slots[slot][rows]
                                - jnp.concatenate([m_new] * (KT // LANES), axis=-1)).astype(BF16)))
            m_sc[rows] = m_new
        pv = _dot(jnp.concatenate(ps, axis=0), vsl_ref[0, 0, pl.ds(k0, KT), :])
        acc_sc[...] = pv if first else jnp.concatenate(alphas, axis=0) * acc_sc[...] + pv

    def stage(j, slot):
        scores(j + 1, 1 - slot)
        update(j, slot)

    scores(0, 0)

    @pl.when(last == 0)
    def _():
        update(0, 0, first=True)

    @pl.when(last > 0)
    def _():
        scores(1, 1)
        update(0, 0, first=True)
        _two_slot_pipeline(1, last, stage, update)

    start = pl.multiple_of(jnp.maximum(s0 - WINDOW, 0), TQ)
    kpos_w = start + lax.broadcasted_iota(jnp.int32, (1, WK), 1)
    mask_w = (kpos_w <= t_col) & (kpos_w > t_col - WINDOW)
    acc_w = []
    for h0 in range(0, HG, PAIR):
        s_w = _dot_nt(qs[h0 * TQ:(h0 + PAIR) * TQ], kw_ref[0, 0, pl.ds(start, WK), :])
        pw = []
        for h in range(PAIR):
            sh = biased(s_w, kpos_w, mask_w, h, head=h0 + h)
            pw.append(jnp.exp2((sh - jnp.max(sh, axis=-1, keepdims=True)).astype(BF16)))
        acc_w.append(_dot(jnp.concatenate(pw, axis=0), vw_ref[0, 0, pl.ds(start, WK), :]))
    acc_w = jnp.concatenate(acc_w, axis=0)

    o_cmp = oc_ref[...].astype(F32)
    outs = []
    for h in range(HG):
        rows = slice(h * TQ, (h + 1) * TQ)
        a_s, a_w = acc_sc[rows], acc_w[rows]
        r_s, r_w = gt / a_s, gt / a_w
        outs.append(o_cmp[:, h * DH:(h + 1) * DH]
                    + r_s[:, DH + 3 * h + 1:DH + 3 * h + 2] * a_s[:, :DH]
                    + r_w[:, DH + 3 * h + 2:DH + 3 * h + 3] * a_w[:, :DH])
    o_ref[...] = jnp.concatenate(outs, axis=-1).astype(o_ref.dtype)


def _nsa_main(tile_list, tile_count, q, sel, oc, gates, slope_tab, expand, ksl, vsl, kw, vw, B, T):
    G, HG, DH, TQ = NSA_GROUPS, NSA_HPG, NSA_DH, NSA_Q_TILE
    nq = T // TQ
    n_slc = sel.shape[3]
    n_ktiles = T // NSA_KEY_TILE
    kv = lambda last: pl.BlockSpec((1, 1, T, last), lambda b, g, i, tl, tc: (b, g, 0, 0))
    grid_spec = pltpu.PrefetchScalarGridSpec(
        num_scalar_prefetch=2,
        grid=(B, G, nq),
        in_specs=[pl.BlockSpec((1, 1, HG, TQ, DH), lambda b, g, i, tl, tc: (b, g, 0, i, 0)),
                  pl.BlockSpec((1, 1, TQ, n_slc), lambda b, g, i, tl, tc: (b, g, i, 0)),
                  pl.BlockSpec((TQ, HG * DH), lambda b, g, i, tl, tc: (b * nq + i, g)),
                  pl.BlockSpec((1, 1, TQ, LANES), lambda b, g, i, tl, tc: (b, g, i, 0)),
                  pl.BlockSpec((1, HG * TQ, LANES), lambda b, g, i, tl, tc: (g, 0, 0)),
                  pl.BlockSpec((n_slc, T), lambda b, g, i, tl, tc: (0, 0)),
                  kv(DH), kv(LANES), kv(DH), kv(LANES)],
        out_specs=pl.BlockSpec((TQ, HG * DH), lambda b, g, i, tl, tc: (b * nq + i, g)),
        scratch_shapes=[pltpu.VMEM((HG * TQ, LANES), F32), pltpu.VMEM((HG * TQ, LANES), F32),
                        pltpu.VMEM((HG * TQ, NSA_KEY_TILE), F32), pltpu.VMEM((HG * TQ, NSA_KEY_TILE), F32)],
    )
    return pl.pallas_call(
        functools.partial(_nsa_main_kernel, n_ktiles=n_ktiles),
        grid_spec=grid_spec,
        out_shape=jax.ShapeDtypeStruct((B * T, NSA_HEADS * DH), BF16),
        compiler_params=_cparams("parallel", "parallel", "arbitrary"),
        name="nsa_select_window",
    )(tile_list, tile_count, q, sel, oc, gates, slope_tab, expand, ksl, vsl, kw, vw)


def _nsa_mixer(x, B, T, attn_norm, w_in, q_norm, kcmp_norm, kslc_norm, kwin_norm, pos_k, pos_v,
               k_w1, k_b1, k_w2, v_w1, v_b1, v_w2):
    G, HG, DH, TQ = NSA_GROUPS, NSA_HPG, NSA_DH, NSA_Q_TILE
    assert T % NSA_CMP_TILE == 0 and T % MLA_TILE == 0 and T >= WINDOW + TQ and NSA_KEY_TILE == TQ
    q, kc, vc, ksl, vsl, kw, vw, gates = _nsa_in(x, attn_norm, w_in, q_norm, kslc_norm, kwin_norm, B, T)

    nrow = T // CMP_STRIDE
    kcmp, vcmp = _compress(kc, vc, pos_k, pos_v, k_w1, k_b1, k_w2, v_w1, v_b1, v_w2, kcmp_norm)

    n_slc = T // SLC_BLOCK
    n_top = min(SLC_TOP, n_slc)
    n_slc_pad = max(n_slc, LANES)
    slopes = 2.0 ** (-8.0 * jnp.arange(1, NSA_HEADS + 1, dtype=F32) / NSA_HEADS) * LOG2E
    slope_tab = jnp.broadcast_to(slopes.reshape(G, HG, 1, 1), (G, HG, TQ, LANES)).reshape(G, HG * TQ, LANES)
    cj = jnp.arange(nrow)[:, None] * CMP_STRIDE
    si = jnp.arange(n_slc_pad)[None, :] * SLC_BLOCK
    overlap = ((cj <= si + SLC_BLOCK - 1) & (cj + CMP_BLOCK - 1 >= si)
               & (jnp.arange(nrow)[:, None] < nrow - 1)).astype(BF16)
    blocks_per_tile = NSA_KEY_TILE // SLC_BLOCK
    group_mat = (jnp.arange(n_slc_pad)[:, None] // blocks_per_tile
                 == jnp.arange(LANES)[None, :]).astype(BF16)
    expand = (jnp.arange(n_slc_pad)[:, None] == jnp.arange(T)[None, :] // SLC_BLOCK).astype(BF16)

    oc, sel, act = _nsa_cmp(q, kcmp, vcmp, gates, slope_tab, overlap, group_mat, B, T, n_top)
    n_ktiles = T // NSA_KEY_TILE
    tile = jnp.arange(n_ktiles)
    reach = tile[None, :] * NSA_KEY_TILE < (jnp.arange(T // TQ)[:, None] + 1) * TQ
    visit = ((act[:, :, :, 0, :n_ktiles] > 0.0) & reach) | (tile == 0)
    slot = jnp.cumsum(visit, axis=-1) - 1
    tile_list = jnp.sum(jnp.where(visit[..., :, None] & (slot[..., :, None] == tile), tile[:, None], 0), axis=-2)
    tile_count = jnp.sum(visit, axis=-1)
    return _nsa_main(tile_list.astype(jnp.int32).reshape(-1), tile_count.astype(jnp.int32).reshape(-1),
                     q, sel, oc, gates, slope_tab, expand, ksl, vsl, kw, vw, B, T)


def _rope_layout(w):
    half = QK_ROPE // 2
    z = jnp.zeros(w.shape[:-1] + (LANES // 2 - half,), w.dtype)
    return jnp.concatenate([w[..., :half], z, w[..., half:], z], axis=-1)


def _head_layout(w):
    w = w.reshape(w.shape[:-1] + (MLA_HEADS, MLA_QK))
    w = jnp.concatenate([w[..., :QK_NOPE], _rope_layout(w[..., QK_NOPE:])], axis=-1)
    return w.reshape(w.shape[:-2] + (MLA_HEADS * MLA_QK_PAD,))


def _roped(rope, gr_ref, cos_ref, sin_ref):
    r = rope * gr_ref[...]
    return r * cos_ref[...] + pltpu.roll(r, LANES // 2, 1) * sin_ref[...]


def _head_inv_rms(nope, rope_sq):
    ss = jnp.sum(nope * nope + rope_sq, axis=-1, keepdims=True)
    return lax.rsqrt(ss * (1.0 / MLA_QK) + NORM_EPS)


def _mla_q_kernel(x_ref, an_ref, wa_ref, qan_ref, wb_ref, gn_ref, gr_ref, cos_ref, sin_ref, o_ref):
    qa = _dot(_rms(x_ref[...], an_ref[...]).astype(BF16), wa_ref[...])
    q = _dot(_rms(qa, qan_ref[...]).astype(BF16), wb_ref[...])
    for h in range(MLA_HEADS):
        nope = q[:, h * MLA_QK_PAD:h * MLA_QK_PAD + QK_NOPE]
        rope = q[:, h * MLA_QK_PAD + QK_NOPE:(h + 1) * MLA_QK_PAD]
        inv = _head_inv_rms(nope, rope * rope)
        o_ref[0, h] = jnp.concatenate([nope * inv * gn_ref[...], _roped(rope * inv, gr_ref, cos_ref, sin_ref)],
                                      axis=-1).astype(BF16)


def _mla_kv_kernel(x_ref, n_ref, wa_ref, cn_ref, wb_ref, gn_ref, gr_ref, cos_ref, sin_ref, k_ref, v_ref):
    kv_a = _dot(_rms(x_ref[...], n_ref[...]).astype(BF16), wa_ref[...])
    kv = _dot(_rms(kv_a[:, :KV_LORA], cn_ref[...]).astype(BF16), wb_ref[...])
    rope = kv_a[:, KV_LORA:]
    rope_sq = rope * rope
    roped = _roped(rope, gr_ref, cos_ref, sin_ref)
    for h in range(MLA_HEADS):
        base = h * (QK_NOPE + MLA_V)
        nope = kv[:, base:base + QK_NOPE]
        inv = _head_inv_rms(nope, rope_sq)
        k_ref[0, h] = jnp.concatenate([nope * inv * gn_ref[...], inv * roped], axis=-1).astype(BF16)
        v = kv[:, base + QK_NOPE:base + QK_NOPE + MLA_V]
        v_ref[0, h] = jnp.concatenate([v, jnp.ones_like(v)], axis=-1).astype(BF16)


def _mla_specs(tm, nt, d, weights):
    full = lambda shape: pl.BlockSpec(shape, lambda i: (0, 0))
    row = lambda: pl.BlockSpec((1, LANES), lambda i: (0, 0))
    tab = lambda: pl.BlockSpec((tm, LANES), lambda i: (i % nt, 0))
    return ([pl.BlockSpec((tm, d), lambda i: (i, 0))] + [full(w.shape) for w in weights]
            + [row(), row(), tab(), tab()])


def _mla_q(x, attn_norm, w_q_a, q_a_norm, w_q_b, gain, cos_l, sin_l, B, T, *, tm=512):
    nt = T // tm
    d = x.shape[1]
    weights = (attn_norm.reshape(1, d), w_q_a, q_a_norm.reshape(1, -1), w_q_b)
    gain = gain * (MLA_QK ** -0.5 * LOG2E)
    return pl.pallas_call(
        _mla_q_kernel,
        grid=(B * nt,),
        in_specs=_mla_specs(tm, nt, d, weights),
        out_specs=pl.BlockSpec((1, MLA_HEADS, tm, MLA_QK_PAD), lambda i: (i // nt, 0, i % nt, 0)),
        out_shape=jax.ShapeDtypeStruct((B, MLA_HEADS, T, MLA_QK_PAD), BF16),
        compiler_params=_cparams("parallel"),
        name="mla_q_path",
    )(x, *weights, gain[:QK_NOPE].reshape(1, LANES), _rope_layout(gain[QK_NOPE:]).reshape(1, LANES),
      cos_l, sin_l)


def _mla_kv(x, kv_norm, w_kv_a, kv_c_norm, w_kv_b, gain, cos_l, sin_l, B, T, *, tm=512):
    nt = T // tm
    d = x.shape[1]
    weights = (kv_norm.reshape(1, d), w_kv_a, kv_c_norm.reshape(1, -1), w_kv_b)
    return pl.pallas_call(
        _mla_kv_kernel,
        grid=(B * nt,),
        in_specs=_mla_specs(tm, nt, d, weights),
        out_specs=[pl.BlockSpec((1, MLA_HEADS, tm, MLA_QK_PAD), lambda i: (i // nt, 0, i % nt, 0)),
                   pl.BlockSpec((1, MLA_HEADS, tm, 2 * MLA_V), lambda i: (i // nt, 0, i % nt, 0))],
        out_shape=[jax.ShapeDtypeStruct((B, MLA_HEADS, T, MLA_QK_PAD), BF16),
                   jax.ShapeDtypeStruct((B, MLA_HEADS, T, 2 * MLA_V), BF16)],
        compiler_params=_cparams("parallel"),
        name="mla_kv_path",
    )(x, *weights, gain[:QK_NOPE].reshape(1, LANES), _rope_layout(gain[QK_NOPE:]).reshape(1, LANES),
      cos_l, sin_l)


def _two_slot_pipeline(first, last, stage, finish):
    a = first % 2
    n = last - first

    def pair(j, carry):
        stage(first + 2 * j, a)
        stage(first + 2 * j + 1, 1 - a)
        return carry

    lax.fori_loop(0, n // 2, pair, 0)

    @pl.when(n % 2 == 1)
    def _():
        stage(last - 1, a)
        finish(last, 1 - a)

    @pl.when(n % 2 == 0)
    def _():
        finish(last, a)


def _mla_attn_kernel(q_ref, k_ref, v_ref, o_ref, m_sc, acc_sc, s0_sc, s1_sc):
    TQ = MLA_TILE
    qi = pl.program_id(2)
    q = q_ref[0, 0]
    s_slots = (s0_sc, s1_sc)
    m_sc[...] = jnp.full_like(m_sc, NEG_INF)
    acc_sc[...] = jnp.zeros_like(acc_sc)

    def scores(i, slot):
        k0 = pl.multiple_of(i * TQ, TQ)
        s_slots[slot][...] = _dot_nt(q, k_ref[0, 0, pl.ds(k0, TQ), :])

    def update(i, slot, masked):
        k0 = pl.multiple_of(i * TQ, TQ)

        def load():
            s = s_slots[slot][...]
            if masked:
                qq = lax.broadcasted_iota(jnp.int32, (TQ, TQ), 0)
                kk = lax.broadcasted_iota(jnp.int32, (TQ, TQ), 1)
                s = jnp.where(kk <= qq, s, NEG_INF)
            return s

        m_old = m_sc[...]
        m_new = jnp.maximum(m_old, jnp.max(load(), axis=-1, keepdims=True))
        p = jnp.exp2((load() - jnp.concatenate([m_new] * (TQ // LANES), axis=-1)).astype(BF16))
        alpha = jnp.exp2(m_old - m_new)
        acc_sc[...] = (jnp.concatenate([alpha] * (acc_sc.shape[1] // LANES), axis=-1) * acc_sc[...]
                       + _dot(p, v_ref[0, 0, pl.ds(k0, TQ), :]))
        m_sc[...] = m_new

    def stage(i, slot):
        scores(i + 1, 1 - slot)
        update(i, slot, False)

    scores(0, 0)
    _two_slot_pipeline(0, qi, stage, lambda i, slot: update(i, slot, True))
    o_ref[...] = (acc_sc[:, :MLA_V] / acc_sc[:, MLA_V:]).astype(o_ref.dtype)


def _mla_attn(q, k, v, B, T):
    H, TQ = MLA_HEADS, MLA_TILE
    nq = T // TQ
    return pl.pallas_call(
        _mla_attn_kernel,
        grid=(B, H, nq),
        in_specs=[pl.BlockSpec((1, 1, TQ, MLA_QK_PAD), lambda b, h, i: (b, h, i, 0)),
                  pl.BlockSpec((1, 1, T, MLA_QK_PAD), lambda b, h, i: (b, h, 0, 0)),
                  pl.BlockSpec((1, 1, T, 2 * MLA_V), lambda b, h, i: (b, h, 0, 0))],
        out_specs=pl.BlockSpec((TQ, MLA_V), lambda b, h, i: (b * nq + i, h)),
        out_shape=jax.ShapeDtypeStruct((B * T, H * MLA_V), BF16),
        scratch_shapes=[pltpu.VMEM((TQ, LANES), F32), pltpu.VMEM((TQ, 2 * MLA_V), F32),
                        pltpu.VMEM((TQ, TQ), F32), pltpu.VMEM((TQ, TQ), F32)],
        compiler_params=_cparams("parallel", "parallel", "arbitrary"),
        name="mla_flash_attn",
    )(q, k, v)


def _pad_cols(w, mult=LANES):
    pad = -w.shape[1] % mult
    return jnp.pad(w, ((0, 0), (0, pad))) if pad else w


def kernel(x, a_attn_norm, a_w_in, a_q_norm, a_kcmp_norm, a_kslc_norm, a_kwin_norm, a_cmp_pos_k, a_cmp_pos_v, a_cmp_k_w1, a_cmp_k_b1, a_cmp_k_w2, a_cmp_v_w1, a_cmp_v_b1, a_cmp_v_w2, a_w_out, kv_norm, kv_w_a, kv_c_norm, kv_w_b, kv_k_norm, b_attn_norm, b_w_q_a, b_q_a_norm, b_w_q_b, b_q_norm, b_w_out, ffn_norm, ffn_w_gate_up, ffn_w_down):
    B, T, D = x.shape
    n_a = a_w_in.shape[0]
    n_b = b_w_q_a.shape[0]
    xs = x.reshape(B * T, D)

    inv = ROPE_THETA ** (-jnp.arange(0, QK_ROPE, 2, dtype=F32) / QK_ROPE)
    ang = jnp.arange(T, dtype=F32)[:, None] * inv[None, :]
    cos, sin = jnp.cos(ang), jnp.sin(ang)
    cos2 = _rope_layout(jnp.concatenate([cos, cos], axis=-1))
    sin2 = _rope_layout(jnp.concatenate([-sin, sin], axis=-1))

    k_shared = v_shared = None
    for layer in range(n_a + n_b):
        if layer < n_a:
            i = layer
            o = _nsa_mixer(xs, B, T, a_attn_norm[i], _pad_cols(a_w_in[i]).astype(BF16),
                           a_q_norm[i], a_kcmp_norm[i], a_kslc_norm[i], a_kwin_norm[i],
                           a_cmp_pos_k[i], a_cmp_pos_v[i], a_cmp_k_w1[i], a_cmp_k_b1[i], a_cmp_k_w2[i],
                           a_cmp_v_w1[i], a_cmp_v_b1[i], a_cmp_v_w2[i])
            w_out = a_w_out[i]
        else:
            j = layer - n_a
            q = _mla_q(xs, b_attn_norm[j], b_w_q_a[j].astype(BF16), b_q_a_norm[j],
                       _head_layout(b_w_q_b[j]).astype(BF16), b_q_norm[j], cos2, sin2, B, T)
            o = _mla_attn(q, k_shared, v_shared, B, T)
            w_out = b_w_out[j]
        xs = _out_ffn(xs, o, w_out.astype(BF16), ffn_norm[layer],
                      ffn_w_gate_up[layer].astype(BF16), ffn_w_down[layer].astype(BF16))
        if layer == n_a - 1:
            w_kv_a = jnp.concatenate([kv_w_a[:, :KV_LORA], _rope_layout(kv_w_a[:, KV_LORA:])], axis=-1)
            k_shared, v_shared = _mla_kv(xs, kv_norm, w_kv_a.astype(BF16), kv_c_norm, kv_w_b.astype(BF16),
                                         kv_k_norm, cos2, sin2, B, T)
    return xs.reshape(B, T, D)
```

```python
import functools

import jax
import jax.numpy as jnp
from jax import lax
from jax.experimental import pallas as pl
from jax.experimental.pallas import tpu as pltpu

F32 = jnp.float32
BF16 = jnp.bfloat16

NORM_EPS = 1e-6
NEG_INF = -1e30
LANES = 128
SUBLANES = 8

NSA_HEADS = 16
NSA_GROUPS = 4
NSA_HPG = NSA_HEADS // NSA_GROUPS
NSA_DH = 64
CMP_BLOCK = 32
CMP_STRIDE = 16
CMP_HIDDEN = 256
SLC_BLOCK = 64
SLC_TOP = 16
WINDOW = 512
FORCE_SCORE = 1e4
NSA_Q_TILE = 256
NSA_CMP_TILE = 1024
NSA_KEY_TILE = 256
PAIR = 2
LOG2E = 1.4426950408889634

MLA_HEADS = 8
QK_NOPE = 128
QK_ROPE = 64
MLA_QK = QK_NOPE + QK_ROPE
MLA_QK_PAD = QK_NOPE + LANES
MLA_V = 128
Q_LORA = 384
KV_LORA = 256
ROPE_THETA = 10000.0
MLA_TILE = 1024

VMEM_LIMIT = 56 * 1024 * 1024


def _cparams(*sem):
    return pltpu.CompilerParams(dimension_semantics=sem, vmem_limit_bytes=VMEM_LIMIT)


def _rms(x, g):
    return x * lax.rsqrt(jnp.mean(x * x, axis=-1, keepdims=True) + NORM_EPS) * g


def _dot(a, b):
    return jnp.dot(a, b, preferred_element_type=F32)


def _dot_nt(a, b):
    return lax.dot_general(a, b, (((1,), (1,)), ((), ())), preferred_element_type=F32)


def _out_ffn_kernel(x_ref, a_ref, wo_ref, g_ref, wgu_ref, wd_ref, o_ref):
    hid = wd_ref.shape[0]
    x = x_ref[...] + _dot(a_ref[...], wo_ref[...])
    h = _rms(x, g_ref[...]).astype(BF16)
    gate = _dot(h, wgu_ref[:, :hid])
    up = _dot(h, wgu_ref[:, hid:])
    a = (gate * jax.nn.sigmoid(gate) * up).astype(BF16)
    o_ref[...] = x + _dot(a, wd_ref[...])


def _out_ffn(x, mixed, w_out, gain, w_gate_up, w_down, *, tm=512):
    n, d = x.shape
    hid = w_down.shape[0]
    dm = mixed.shape[1]
    assert n % tm == 0 and hid % LANES == 0
    resident = lambda shape: pl.BlockSpec(shape, lambda i: (0, 0), pipeline_mode=pl.Buffered(1))
    return pl.pallas_call(
        _out_ffn_kernel,
        grid=(n // tm,),
        in_specs=[pl.BlockSpec((tm, d), lambda i: (i, 0)),
                  pl.BlockSpec((tm, dm), lambda i: (i, 0)),
                  resident((dm, d)),
                  pl.BlockSpec((1, d), lambda i: (0, 0)),
                  resident((d, 2 * hid)), resident((hid, d))],
        out_specs=pl.BlockSpec((tm, d), lambda i: (i, 0)),
        out_shape=jax.ShapeDtypeStruct((n, d), F32),
        compiler_params=_cparams("parallel"),
        name="out_proj_swiglu_ffn",
    )(x, mixed, w_out, gain.reshape(1, d), w_gate_up, w_down)


def _split2(x):
    hi = x.astype(BF16)
    return hi, (x - hi.astype(F32)).astype(BF16)


def _nsa_in_kernel(x_ref, an_ref, w_ref, gain_ref, gsum_ref, gexp_ref,
                   q_ref, kc_ref, vc_ref, ksl_ref, vsl_ref, kw_ref, vw_ref, gt_ref, rows_sc):
    G, HG, DH = NSA_GROUPS, NSA_HPG, NSA_DH
    qw, kvw = NSA_HEADS * DH, G * DH
    tm = x_ref.shape[0]
    p = _dot(_rms(x_ref[...], an_ref[...]).astype(BF16), w_ref[...])
    ones_col = jnp.ones((tm, LANES - DH), F32)
    n_gate = HG * 3
    gate_pad = jnp.zeros((tm, LANES - DH - n_gate), F32)

    def piece(base, g):
        return p[:, base + g * DH: base + (g + 1) * DH]

    xn = jnp.concatenate([p[:, :qw], p[:, qw + 2 * kvw:qw + 3 * kvw], p[:, qw + 4 * kvw:qw + 5 * kvw]], axis=-1)
    hi, lo = _split2(xn * xn)
    ss = _dot(hi, gsum_ref[...]) + _dot(lo, gsum_ref[...])
    hi, lo = _split2(lax.rsqrt(ss * (1.0 / DH) + NORM_EPS))
    xn = xn * (_dot(hi, gexp_ref[...]) + _dot(lo, gexp_ref[...])) * gain_ref[...]

    for g in range(G):
        for h in range(HG):
            q_ref[0, g, h] = xn[:, (g * HG + h) * DH:(g * HG + h + 1) * DH].astype(BF16)
        for which, (base, out_ref) in enumerate(((qw, kc_ref), (qw + kvw, vc_ref))):
            stage_ref = rows_sc.at[2 * g + which]
            stage_ref[...] = piece(base, g)
            for m in range(0, CMP_STRIDE, 2):
                pair = [stage_ref[pl.ds(m + d, tm // CMP_STRIDE, stride=CMP_STRIDE), :] for d in range(2)]
                out_ref[0, g, :, m * DH:(m + 2) * DH] = jnp.concatenate(pair, axis=-1).astype(BF16)
        ksl_ref[0, g] = xn[:, qw + g * DH:qw + (g + 1) * DH].astype(BF16)
        vsl_ref[0, g] = jnp.concatenate([piece(qw + 3 * kvw, g), ones_col], axis=-1).astype(BF16)
        kw_ref[0, g] = xn[:, qw + kvw + g * DH:qw + kvw + (g + 1) * DH].astype(BF16)
        vw_ref[0, g] = jnp.concatenate([piece(qw + 5 * kvw, g), ones_col], axis=-1).astype(BF16)
        gb = qw + 6 * kvw + g * n_gate
        gt_ref[0, g] = jnp.concatenate([jnp.zeros((tm, DH), F32), jax.nn.sigmoid(p[:, gb:gb + n_gate]),
                                        gate_pad], axis=-1)


def _nsa_in(x, attn_norm, w_in, q_norm, kslc_norm, kwin_norm, B, T, *, tm=512):
    G, HG, DH = NSA_GROUPS, NSA_HPG, NSA_DH
    nt = T // tm
    d, width = w_in.shape
    n_norm = (NSA_HEADS + 2 * G) * DH
    gain = jnp.concatenate([jnp.tile(q_norm, NSA_HEADS) * (DH ** -0.5 * LOG2E),
                            jnp.tile(kslc_norm, G), jnp.tile(kwin_norm, G)]).reshape(1, n_norm)
    gsum = (jnp.arange(n_norm)[:, None] // DH == jnp.arange(LANES)[None, :]).astype(BF16)
    full = lambda shape: pl.BlockSpec(shape, lambda i: (0, 0))
    per_group = lambda last: pl.BlockSpec((1, G, tm, last), lambda i: (i // nt, 0, i % nt, 0))
    shp = lambda last, dt: jax.ShapeDtypeStruct((B, G, T, last), dt)
    half = CMP_STRIDE * DH
    cmp_spec = pl.BlockSpec((1, G, tm // CMP_STRIDE, half), lambda i: (i // nt, 0, i % nt, 0))
    cmp_shape = jax.ShapeDtypeStruct((B, G, T // CMP_STRIDE, half), BF16)
    return pl.pallas_call(
        _nsa_in_kernel,
        grid=(B * nt,),
        in_specs=[pl.BlockSpec((tm, d), lambda i: (i, 0)), full((1, d)), full((d, width)),
                  full((1, n_norm)), full((n_norm, LANES)), full((LANES, n_norm))],
        out_specs=[pl.BlockSpec((1, G, HG, tm, DH), lambda i: (i // nt, 0, 0, i % nt, 0)),
                   cmp_spec, cmp_spec, per_group(DH), per_group(LANES),
                   per_group(DH), per_group(LANES), per_group(LANES)],
        out_shape=[jax.ShapeDtypeStruct((B, G, HG, T, DH), BF16),
                   cmp_shape, cmp_shape, shp(DH, BF16), shp(LANES, BF16),
                   shp(DH, BF16), shp(LANES, BF16), shp(LANES, F32)],
        scratch_shapes=[pltpu.VMEM((2 * G, tm, DH), F32)],
        compiler_params=_cparams("parallel"),
        name="nsa_in_proj_split",
    )(x, attn_norm.reshape(1, d), w_in, gain, gsum, gsum.T)


def _compress_kernel(rk_ref, rv_ref, pek_ref, pev_ref, kw1_ref, kb1_ref, kw2_ref,
                     vw1_ref, vb1_ref, vw2_ref, kn_ref, kcmp_ref, vcmp_ref):
    half = CMP_STRIDE * NSA_DH
    nrow = rk_ref.shape[2]

    def mlp(r_ref, pe_ref, w1_ref, b1_ref, w2_ref):
        r = r_ref[0, 0].astype(BF16)
        ya = _dot(r, w1_ref[:half, :])
        yb = _dot(r, w1_ref[half:, :])
        pe = jnp.broadcast_to(pe_ref[...], (SUBLANES, 2 * half)).astype(BF16)
        c = _dot(pe, w1_ref[...])[0:1] + b1_ref[...]
        hid = ya + pltpu.roll(yb, nrow - 1, 0) + c
        return _dot(jax.nn.gelu(hid).astype(BF16), w2_ref[...])

    kcmp_ref[0, 0] = _rms(mlp(rk_ref, pek_ref, kw1_ref, kb1_ref, kw2_ref), kn_ref[...]).astype(BF16)
    vcmp_ref[0, 0] = mlp(rv_ref, pev_ref, vw1_ref, vb1_ref, vw2_ref).astype(BF16)


def _compress(rk, rv, pe_k, pe_v, k_w1, k_b1, k_w2, v_w1, v_b1, v_w2, kcmp_norm):
    B, G, nrow, half = rk.shape
    DH, HID = NSA_DH, CMP_HIDDEN
    full = lambda shape: pl.BlockSpec(shape, lambda b, g: (0,) * len(shape))
    r_spec = pl.BlockSpec((1, 1, nrow, half), lambda b, g: (b, g, 0, 0))
    o_spec = pl.BlockSpec((1, 1, nrow, DH), lambda b, g: (b, g, 0, 0))
    return pl.pallas_call(
        _compress_kernel,
        grid=(B, G),
        in_specs=[r_spec, r_spec, full((1, 2 * half)), full((1, 2 * half)),
                  full((2 * half, HID)), full((1, HID)), full((HID, DH)),
                  full((2 * half, HID)), full((1, HID)), full((HID, DH)), full((1, DH))],
        out_specs=[o_spec, o_spec],
        out_shape=[jax.ShapeDtypeStruct((B, G, nrow, DH), BF16)] * 2,
        compiler_params=_cparams("parallel", "parallel"),
        name="nsa_compress",
    )(rk, rv, pe_k.reshape(1, -1), pe_v.reshape(1, -1),
      k_w1.astype(BF16), k_b1.reshape(1, HID), k_w2.astype(BF16),
      v_w1.astype(BF16), v_b1.reshape(1, HID), v_w2.astype(BF16), kcmp_norm.reshape(1, DH))


def _split3(x):
    hi = x.astype(BF16)
    r1 = x - hi.astype(F32)
    mid = r1.astype(BF16)
    lo = (r1 - mid.astype(F32)).astype(BF16)
    return hi, mid, lo


def _nsa_cmp_kernel(q_ref, kc_ref, vc_ref, gt_ref, sl_ref, ov_ref, gm_ref,
                    oc_ref, sel_ref, act_ref, *, n_top):
    HG, DH, TQ, TM = NSA_HPG, NSA_DH, NSA_CMP_TILE, NSA_Q_TILE
    s0 = pl.program_id(2) * TQ
    ncmp = kc_ref.shape[2]
    n_slc = ov_ref.shape[1]
    qs = q_ref[0, 0].reshape(HG * TQ, DH)
    gt = gt_ref[0, 0]

    def attend(nc):
        j = lax.broadcasted_iota(jnp.int32, (1, nc), 1)
        t = s0 + lax.broadcasted_iota(jnp.int32, (TQ, 1), 0)
        mask = j * CMP_STRIDE + (CMP_BLOCK - 1) <= t
        mid = (j * CMP_STRIDE - s0).astype(F32) + 0.5 * (CMP_BLOCK - 1)
        has_key = t >= CMP_BLOCK - 1
        s = _dot_nt(qs, kc_ref[0, 0, :nc, :])
        psum = jnp.zeros((TQ, nc), F32)
        ps = []
        for h in range(HG):
            rows = slice(h * TQ, (h + 1) * TQ)
            slope_row = jnp.concatenate([sl_ref[0, h * TM:h * TM + 1, :]] * (nc // LANES), axis=-1)
            sh = jnp.where(mask, s[rows] + slope_row * mid, NEG_INF)
            m = jnp.max(sh, axis=-1, keepdims=True)
            e = jnp.exp2(sh - m)
            l = jnp.sum(e, axis=-1, keepdims=True)
            inv = jnp.where(has_key, 1.0 / l, 0.0)
            p = e * inv
            psum = psum + p
            ps.append(p.astype(BF16))
        o_all = _dot(jnp.concatenate(ps, axis=0), vc_ref[0, 0, :nc, :])
        oc_ref[...] = jnp.concatenate(
            [o_all[h * TQ:(h + 1) * TQ] * gt[:, DH + 3 * h:DH + 3 * h + 1] for h in range(HG)],
            axis=-1).astype(oc_ref.dtype)
        hi, md, lo = _split3(psum)
        ov = ov_ref[:nc, :]
        imp = _dot(hi, ov) + _dot(md, ov) + _dot(lo, ov)
        select(imp, nc * CMP_STRIDE // SLC_BLOCK)

    def select(imp, nb):
        imp_t = imp.T[:nb]
        blk = lax.broadcasted_iota(jnp.int32, (nb, TQ), 0)
        tq = s0 + lax.broadcasted_iota(jnp.int32, (nb, TQ), 1)
        cur = lax.shift_right_logical(tq, SLC_BLOCK.bit_length() - 1)
        forced = (blk == 0) | (blk == cur) | (blk == cur - 1)
        taken = -2.0
        score = jnp.where(forced, taken, jnp.where(blk * SLC_BLOCK <= tq, imp_t, -1.0))

        def pick(_, score):
            mx = jnp.max(score, axis=0, keepdims=True)
            first = jnp.min(jnp.where(score == mx, blk, nb), axis=0, keepdims=True)
            return jnp.where(blk == first, taken, score)

        score = lax.fori_loop(0, n_top - 3, pick, score, unroll=True)
        sel_t = jnp.where(score == taken, 1.0, 0.0)
        if nb < n_slc:
            sel_t = jnp.concatenate([sel_t, jnp.zeros((n_slc - nb, TQ), F32)], axis=0)
        sel = sel_t.T.astype(BF16)
        sel_ref[0, 0] = sel
        for sub in range(TQ // TM):
            cnt = _dot(jnp.ones((SUBLANES, TM), BF16), sel[sub * TM:(sub + 1) * TM])
            act_ref[0, 0, sub] = _dot((cnt > 0.0).astype(BF16), gm_ref[...])

    need = (s0 + TQ) // CMP_STRIDE
    widths = list(range(LANES, ncmp + 1, LANES))
    for idx, nc in enumerate(widths):
        lo_w = widths[idx - 1] if idx else 0
        pl.when((need > lo_w) & (need <= nc))(functools.partial(attend, nc))


def _nsa_cmp(q, kcmp, vcmp, gates, slope_tab, overlap, group_mat, B, T, n_top):
    G, HG, DH, TQ, TM = NSA_GROUPS, NSA_HPG, NSA_DH, NSA_CMP_TILE, NSA_Q_TILE
    nq = T // TQ
    ncmp = kcmp.shape[2]
    n_slc = overlap.shape[1]
    assert n_top >= 3 and ncmp % LANES == 0 and HG < FORCE_SCORE
    return pl.pallas_call(
        functools.partial(_nsa_cmp_kernel, n_top=n_top),
        grid=(B, G, nq),
        in_specs=[pl.BlockSpec((1, 1, HG, TQ, DH), lambda b, g, i: (b, g, 0, i, 0)),
                  pl.BlockSpec((1, 1, ncmp, DH), lambda b, g, i: (b, g, 0, 0)),
                  pl.BlockSpec((1, 1, ncmp, DH), lambda b, g, i: (b, g, 0, 0)),
                  pl.BlockSpec((1, 1, TQ, LANES), lambda b, g, i: (b, g, i, 0)),
                  pl.BlockSpec((1, HG * TM, LANES), lambda b, g, i: (g, 0, 0)),
                  pl.BlockSpec((ncmp, n_slc), lambda b, g, i: (0, 0)),
                  pl.BlockSpec((n_slc, LANES), lambda b, g, i: (0, 0))],
        out_specs=[pl.BlockSpec((TQ, HG * DH), lambda b, g, i: (b * nq + i, g)),
                   pl.BlockSpec((1, 1, TQ, n_slc), lambda b, g, i: (b, g, i, 0)),
                   pl.BlockSpec((1, 1, TQ // TM, SUBLANES, LANES), lambda b, g, i: (b, g, i, 0, 0))],
        out_shape=[jax.ShapeDtypeStruct((B * T, NSA_HEADS * DH), BF16),
                   jax.ShapeDtypeStruct((B, G, T, n_slc), BF16),
                   jax.ShapeDtypeStruct((B, G, T // TM, SUBLANES, LANES), F32)],
        compiler_params=_cparams("parallel", "parallel", "arbitrary"),
        name="nsa_cmp_select",
    )(q, kcmp, vcmp, gates, slope_tab, overlap, group_mat)


def _nsa_main_kernel(list_ref, cnt_ref, q_ref, sel_ref, oc_ref, gt_ref, sl_ref, ex_ref,
                     ksl_ref, vsl_ref, kw_ref, vw_ref, o_ref,
                     m_sc, acc_sc, mw_sc, accw_sc, s0_sc, s1_sc, *, n_ktiles):
    HG, DH, TQ, KT = NSA_HPG, NSA_DH, NSA_Q_TILE, NSA_KEY_TILE
    s_slots = (s0_sc, s1_sc)
    WK = WINDOW + TQ
    b, g, qi = pl.program_id(0), pl.program_id(1), pl.program_id(2)
    nq = pl.num_programs(2)
    s0 = qi * TQ
    qs = q_ref[0, 0].reshape(HG * TQ, DH)
    selb = sel_ref[0, 0]
    gt = gt_ref[0, 0]
    t_col = s0 + lax.broadcasted_iota(jnp.int32, (TQ, 1), 0)

    def biased(s, kpos, mask, blk, head=None):
        head = blk if head is None else head
        width = kpos.shape[1]
        slope_row = jnp.concatenate([sl_ref[0, head * TQ:head * TQ + 1, :]] * (width // LANES), axis=-1)
        sh = s[blk * TQ:(blk + 1) * TQ] + slope_row * (kpos - s0).astype(F32)
        return jnp.where(mask, sh, NEG_INF)

    step = (b * NSA_GROUPS + g) * nq + qi
    last = cnt_ref[step] - 1

    def key_start(j):
        return pl.multiple_of(list_ref[step * n_ktiles + j] * KT, KT)

    def scores(j, slot):
        k0 = key_start(j)
        kpos = k0 + lax.broadcasted_iota(jnp.int32, (1, KT), 1)
        s = _dot_nt(qs, ksl_ref[0, 0, pl.ds(k0, KT), :])
        picked = _dot(selb, ex_ref[:, pl.ds(k0, KT)])
        mask = (picked > 0.5) & (kpos <= t_col)
        for h in range(HG):
            s_slots[slot][h * TQ:(h + 1) * TQ] = biased(s, kpos, mask, h)

    def softmax_step(k0, slot, v_ref, m_ref, acc_ref, first):
        ps, alphas = [], []
        for h in range(HG):
            rows = slice(h * TQ, (h + 1) * TQ)
            m_new = jnp.broadcast_to(jnp.max(s_slots[slot][rows], axis=-1, keepdims=True), (TQ, LANES))
            if not first:
                m_old = m_ref[rows]
                m_new = jnp.maximum(m_old, m_new)
                alphas.append(jnp.exp2(m_old - m_new))
            ps.append(jnp.exp2((s_slots[slot][rows]
                                - jnp.concatenate([m_new] * (KT // LANES), axis=-1)).astype(BF16)))
            m_ref[rows] = m_new
        pv = _dot(jnp.concatenate(ps, axis=0), v_ref[0, 0, pl.ds(k0, KT), :])
        acc_ref[...] = pv if first else jnp.concatenate(alphas, axis=0) * acc_ref[...] + pv

    def update(j, slot, first=False):
        softmax_step(key_start(j), slot, vsl_ref, m_sc, acc_sc, first)

    def stage(j, slot):
        scores(j + 1, 1 - slot)
        update(j, slot)

    start = pl.multiple_of(jnp.maximum(s0 - WINDOW, 0), TQ)

    def win_start(c):
        return pl.multiple_of(start + c * KT, KT)

    def scores_w(c, slot):
        k0 = win_start(c)
        kpos = k0 + lax.broadcasted_iota(jnp.int32, (1, KT), 1)
        s = _dot_nt(qs, kw_ref[0, 0, pl.ds(k0, KT), :])
        mask = (kpos <= t_col) & (kpos > t_col - WINDOW)
        for h in range(HG):
            s_slots[slot][h * TQ:(h + 1) * TQ] = biased(s, kpos, mask, h)

    def finish(j, slot, first=False):
        n_win = WK // KT
        scores_w(0, 1 - slot)
        update(j, slot, first)
        for c in range(n_win):
            cur = (slot + 1 + c) % 2
            if c + 1 < n_win:
                scores_w(c + 1, 1 - cur)
            softmax_step(win_start(c), cur, vw_ref, mw_sc, accw_sc, c == 0)

    scores(0, 0)

    @pl.when(last == 0)
    def _():
        finish(0, 0, first=True)

    @pl.when(last > 0)
    def _():
        scores(1, 1)
        update(0, 0, first=True)
        _two_slot_pipeline(1, last, stage, finish)

    acc_w = accw_sc

    o_cmp = oc_ref[...].astype(F32)
    outs = []
    for h in range(HG):
        rows = slice(h * TQ, (h + 1) * TQ)
        a_s, a_w = acc_sc[rows], acc_w[rows]
        r_s, r_w = gt / a_s, gt / a_w
        outs.append(o_cmp[:, h * DH:(h + 1) * DH]
                    + r_s[:, DH + 3 * h + 1:DH + 3 * h + 2] * a_s[:, :DH]
                    + r_w[:, DH + 3 * h + 2:DH + 3 * h + 3] * a_w[:, :DH])
    o_ref[...] = jnp.concatenate(outs, axis=-1).astype(o_ref.dtype)


def _nsa_main(tile_list, tile_count, q, sel, oc, gates, slope_tab, expand, ksl, vsl, kw, vw, B, T):
    G, HG, DH, TQ = NSA_GROUPS, NSA_HPG, NSA_DH, NSA_Q_TILE
    nq = T // TQ
    n_slc = sel.shape[3]
    n_ktiles = T // NSA_KEY_TILE
    kv = lambda last: pl.BlockSpec((1, 1, T, last), lambda b, g, i, tl, tc: (b, g, 0, 0))
    grid_spec = pltpu.PrefetchScalarGridSpec(
        num_scalar_prefetch=2,
        grid=(B, G, nq),
        in_specs=[pl.BlockSpec((1, 1, HG, TQ, DH), lambda b, g, i, tl, tc: (b, g, 0, i, 0)),
                  pl.BlockSpec((1, 1, TQ, n_slc), lambda b, g, i, tl, tc: (b, g, i, 0)),
                  pl.BlockSpec((TQ, HG * DH), lambda b, g, i, tl, tc: (b * nq + i, g)),
                  pl.BlockSpec((1, 1, TQ, LANES), lambda b, g, i, tl, tc: (b, g, i, 0)),
                  pl.BlockSpec((1, HG * TQ, LANES), lambda b, g, i, tl, tc: (g, 0, 0)),
                  pl.BlockSpec((n_slc, T), lambda b, g, i, tl, tc: (0, 0)),
                  kv(DH), kv(LANES), kv(DH), kv(LANES)],
        out_specs=pl.BlockSpec((TQ, HG * DH), lambda b, g, i, tl, tc: (b * nq + i, g)),
        scratch_shapes=[pltpu.VMEM((HG * TQ, LANES), F32), pltpu.VMEM((HG * TQ, LANES), F32),
                        pltpu.VMEM((HG * TQ, LANES), F32), pltpu.VMEM((HG * TQ, LANES), F32),
                        pltpu.VMEM((HG * TQ, NSA_KEY_TILE), F32), pltpu.VMEM((HG * TQ, NSA_KEY_TILE), F32)],
    )
    return pl.pallas_call(
        functools.partial(_nsa_main_kernel, n_ktiles=n_ktiles),
        grid_spec=grid_spec,
        out_shape=jax.ShapeDtypeStruct((B * T, NSA_HEADS * DH), BF16),
        compiler_params=_cparams("parallel", "parallel", "arbitrary"),
        name="nsa_select_window",
    )(tile_list, tile_count, q, sel, oc, gates, slope_tab, expand, ksl, vsl, kw, vw)


def _nsa_mixer(x, B, T, attn_norm, w_in, q_norm, kcmp_norm, kslc_norm, kwin_norm, pos_k, pos_v,
               k_w1, k_b1, k_w2, v_w1, v_b1, v_w2):
    G, HG, DH, TQ = NSA_GROUPS, NSA_HPG, NSA_DH, NSA_Q_TILE
    assert T % NSA_CMP_TILE == 0 and T % MLA_TILE == 0 and T >= WINDOW + TQ and NSA_KEY_TILE == TQ
    q, kc, vc, ksl, vsl, kw, vw, gates = _nsa_in(x, attn_norm, w_in, q_norm, kslc_norm, kwin_norm, B, T)

    nrow = T // CMP_STRIDE
    kcmp, vcmp = _compress(kc, vc, pos_k, pos_v, k_w1, k_b1, k_w2, v_w1, v_b1, v_w2, kcmp_norm)

    n_slc = T // SLC_BLOCK
    n_top = min(SLC_TOP, n_slc)
    n_slc_pad = max(n_slc, LANES)
    slopes = 2.0 ** (-8.0 * jnp.arange(1, NSA_HEADS + 1, dtype=F32) / NSA_HEADS) * LOG2E
    slope_tab = jnp.broadcast_to(slopes.reshape(G, HG, 1, 1), (G, HG, TQ, LANES)).reshape(G, HG * TQ, LANES)
    cj = jnp.arange(nrow)[:, None] * CMP_STRIDE
    si = jnp.arange(n_slc_pad)[None, :] * SLC_BLOCK
    overlap = ((cj <= si + SLC_BLOCK - 1) & (cj + CMP_BLOCK - 1 >= si)
               & (jnp.arange(nrow)[:, None] < nrow - 1)).astype(BF16)
    blocks_per_tile = NSA_KEY_TILE // SLC_BLOCK
    group_mat = (jnp.arange(n_slc_pad)[:, None] // blocks_per_tile
                 == jnp.arange(LANES)[None, :]).astype(BF16)
    expand = (jnp.arange(n_slc_pad)[:, None] == jnp.arange(T)[None, :] // SLC_BLOCK).astype(BF16)

    oc, sel, act = _nsa_cmp(q, kcmp, vcmp, gates, slope_tab, overlap, group_mat, B, T, n_top)
    n_ktiles = T // NSA_KEY_TILE
    tile = jnp.arange(n_ktiles)
    reach = tile[None, :] * NSA_KEY_TILE < (jnp.arange(T // TQ)[:, None] + 1) * TQ
    visit = ((act[:, :, :, 0, :n_ktiles] > 0.0) & reach) | (tile == 0)
    slot = jnp.cumsum(visit, axis=-1) - 1
    tile_list = jnp.sum(jnp.where(visit[..., :, None] & (slot[..., :, None] == tile), tile[:, None], 0), axis=-2)
    tile_count = jnp.sum(visit, axis=-1)
    return _nsa_main(tile_list.astype(jnp.int32).reshape(-1), tile_count.astype(jnp.int32).reshape(-1),
                     q, sel, oc, gates, slope_tab, expand, ksl, vsl, kw, vw, B, T)


def _rope_layout(w):
    half = QK_ROPE // 2
    z = jnp.zeros(w.shape[:-1] + (LANES // 2 - half,), w.dtype)
    return jnp.concatenate([w[..., :half], z, w[..., half:], z], axis=-1)


def _head_layout(w):
    w = w.reshape(w.shape[:-1] + (MLA_HEADS, MLA_QK))
    w = jnp.concatenate([w[..., :QK_NOPE], _rope_layout(w[..., QK_NOPE:])], axis=-1)
    return w.reshape(w.shape[:-2] + (MLA_HEADS * MLA_QK_PAD,))


def _roped(rope, gr_ref, cos_ref, sin_ref):
    r = rope * gr_ref[...]
    return r * cos_ref[...] + pltpu.roll(r, LANES // 2, 1) * sin_ref[...]


def _head_inv_rms(nope, rope_sq):
    ss = jnp.sum(nope * nope + rope_sq, axis=-1, keepdims=True)
    return lax.rsqrt(ss * (1.0 / MLA_QK) + NORM_EPS)


def _mla_q_kernel(x_ref, an_ref, wa_ref, qan_ref, wb_ref, gn_ref, gr_ref, cos_ref, sin_ref, o_ref):
    qa = _dot(_rms(x_ref[...], an_ref[...]).astype(BF16), wa_ref[...])
    q = _dot(_rms(qa, qan_ref[...]).astype(BF16), wb_ref[...])
    for h in range(MLA_HEADS):
        nope = q[:, h * MLA_QK_PAD:h * MLA_QK_PAD + QK_NOPE]
        rope = q[:, h * MLA_QK_PAD + QK_NOPE:(h + 1) * MLA_QK_PAD]
        inv = _head_inv_rms(nope, rope * rope)
        o_ref[0, h] = jnp.concatenate([nope * inv * gn_ref[...], _roped(rope * inv, gr_ref, cos_ref, sin_ref)],
                                      axis=-1).astype(BF16)


def _mla_kv_kernel(x_ref, n_ref, wa_ref, cn_ref, wb_ref, gn_ref, gr_ref, cos_ref, sin_ref, k_ref, v_ref):
    kv_a = _dot(_rms(x_ref[...], n_ref[...]).astype(BF16), wa_ref[...])
    kv = _dot(_rms(kv_a[:, :KV_LORA], cn_ref[...]).astype(BF16), wb_ref[...])
    rope = kv_a[:, KV_LORA:]
    rope_sq = rope * rope
    roped = _roped(rope, gr_ref, cos_ref, sin_ref)
    for h in range(MLA_HEADS):
        base = h * (QK_NOPE + MLA_V)
        nope = kv[:, base:base + QK_NOPE]
        inv = _head_inv_rms(nope, rope_sq)
        k_ref[0, h] = jnp.concatenate([nope * inv * gn_ref[...], inv * roped], axis=-1).astype(BF16)
        v = kv[:, base + QK_NOPE:base + QK_NOPE + MLA_V]
        v_ref[0, h] = jnp.concatenate([v, jnp.ones_like(v)], axis=-1).astype(BF16)


def _mla_specs(tm, nt, d, weights):
    full = lambda shape: pl.BlockSpec(shape, lambda i: (0, 0))
    row = lambda: pl.BlockSpec((1, LANES), lambda i: (0, 0))
    tab = lambda: pl.BlockSpec((tm, LANES), lambda i: (i % nt, 0))
    return ([pl.BlockSpec((tm, d), lambda i: (i, 0))] + [full(w.shape) for w in weights]
            + [row(), row(), tab(), tab()])


def _mla_q(x, attn_norm, w_q_a, q_a_norm, w_q_b, gain, cos_l, sin_l, B, T, *, tm=512):
    nt = T // tm
    d = x.shape[1]
    weights = (attn_norm.reshape(1, d), w_q_a, q_a_norm.reshape(1, -1), w_q_b)
    gain = gain * (MLA_QK ** -0.5 * LOG2E)
    return pl.pallas_call(
        _mla_q_kernel,
        grid=(B * nt,),
        in_specs=_mla_specs(tm, nt, d, weights),
        out_specs=pl.BlockSpec((1, MLA_HEADS, tm, MLA_QK_PAD), lambda i: (i // nt, 0, i % nt, 0)),
        out_shape=jax.ShapeDtypeStruct((B, MLA_HEADS, T, MLA_QK_PAD), BF16),
        compiler_params=_cparams("parallel"),
        name="mla_q_path",
    )(x, *weights, gain[:QK_NOPE].reshape(1, LANES), _rope_layout(gain[QK_NOPE:]).reshape(1, LANES),
      cos_l, sin_l)


def _mla_kv(x, kv_norm, w_kv_a, kv_c_norm, w_kv_b, gain, cos_l, sin_l, B, T, *, tm=512):
    nt = T // tm
    d = x.shape[1]
    weights = (kv_norm.reshape(1, d), w_kv_a, kv_c_norm.reshape(1, -1), w_kv_b)
    return pl.pallas_call(
        _mla_kv_kernel,
        grid=(B * nt,),
        in_specs=_mla_specs(tm, nt, d, weights),
        out_specs=[pl.BlockSpec((1, MLA_HEADS, tm, MLA_QK_PAD), lambda i: (i // nt, 0, i % nt, 0)),
                   pl.BlockSpec((1, MLA_HEADS, tm, 2 * MLA_V), lambda i: (i // nt, 0, i % nt, 0))],
        out_shape=[jax.ShapeDtypeStruct((B, MLA_HEADS, T, MLA_QK_PAD), BF16),
                   jax.ShapeDtypeStruct((B, MLA_HEADS, T, 2 * MLA_V), BF16)],
        compiler_params=_cparams("parallel"),
        name="mla_kv_path",
    )(x, *weights, gain[:QK_NOPE].reshape(1, LANES), _rope_layout(gain[QK_NOPE:]).reshape(1, LANES),
      cos_l, sin_l)


def _two_slot_pipeline(first, last, stage, finish):
    a = first % 2
    n = last - first

    def pair(j, carry):
        stage(first + 2 * j, a)
        stage(first + 2 * j + 1, 1 - a)
        return carry

    lax.fori_loop(0, n // 2, pair, 0)

    @pl.when(n % 2 == 1)
    def _():
        stage(last - 1, a)
        finish(last, 1 - a)

    @pl.when(n % 2 == 0)
    def _():
        finish(last, a)


def _mla_attn_kernel(q_ref, k_ref, v_ref, o_ref, m_sc, acc_sc, s0_sc, s1_sc):
    TQ = MLA_TILE
    qi = pl.program_id(2)
    q = q_ref[0, 0]
    s_slots = (s0_sc, s1_sc)
    m_sc[...] = jnp.full_like(m_sc, NEG_INF)
    acc_sc[...] = jnp.zeros_like(acc_sc)

    def scores(i, slot):
        k0 = pl.multiple_of(i * TQ, TQ)
        s_slots[slot][...] = _dot_nt(q, k_ref[0, 0, pl.ds(k0, TQ), :])

    def update(i, slot, masked):
        k0 = pl.multiple_of(i * TQ, TQ)

        def load():
            s = s_slots[slot][...]
            if masked:
                qq = lax.broadcasted_iota(jnp.int32, (TQ, TQ), 0)
                kk = lax.broadcasted_iota(jnp.int32, (TQ, TQ), 1)
                s = jnp.where(kk <= qq, s, NEG_INF)
            return s

        m_old = m_sc[...]
        m_new = jnp.maximum(m_old, jnp.max(load(), axis=-1, keepdims=True))
        p = jnp.exp2((load() - jnp.concatenate([m_new] * (TQ // LANES), axis=-1)).astype(BF16))
        alpha = jnp.exp2(m_old - m_new)
        acc_sc[...] = (jnp.concatenate([alpha] * (acc_sc.shape[1] // LANES), axis=-1) * acc_sc[...]
                       + _dot(p, v_ref[0, 0, pl.ds(k0, TQ), :]))
        m_sc[...] = m_new

    def stage(i, slot):
        scores(i + 1, 1 - slot)
        update(i, slot, False)

    scores(0, 0)
    _two_slot_pipeline(0, qi, stage, lambda i, slot: update(i, slot, True))
    o_ref[...] = (acc_sc[:, :MLA_V] / acc_sc[:, MLA_V:]).astype(o_ref.dtype)


def _mla_attn(q, k, v, B, T):
    H, TQ = MLA_HEADS, MLA_TILE
    nq = T // TQ
    return pl.pallas_call(
        _mla_attn_kernel,
        grid=(B, H, nq),
        in_specs=[pl.BlockSpec((1, 1, TQ, MLA_QK_PAD), lambda b, h, i: (b, h, i, 0)),
                  pl.BlockSpec((1, 1, T, MLA_QK_PAD), lambda b, h, i: (b, h, 0, 0)),
                  pl.BlockSpec((1, 1, T, 2 * MLA_V), lambda b, h, i: (b, h, 0, 0))],
        out_specs=pl.BlockSpec((TQ, MLA_V), lambda b, h, i: (b * nq + i, h)),
        out_shape=jax.ShapeDtypeStruct((B * T, H * MLA_V), BF16),
        scratch_shapes=[pltpu.VMEM((TQ, LANES), F32), pltpu.VMEM((TQ, 2 * MLA_V), F32),
                        pltpu.VMEM((TQ, TQ), F32), pltpu.VMEM((TQ, TQ), F32)],
        compiler_params=_cparams("parallel", "parallel", "arbitrary"),
        name="mla_flash_attn",
    )(q, k, v)


def _pad_cols(w, mult=LANES):
    pad = -w.shape[1] % mult
    return jnp.pad(w, ((0, 0), (0, pad))) if pad else w


def kernel(x, a_attn_norm, a_w_in, a_q_norm, a_kcmp_norm, a_kslc_norm, a_kwin_norm, a_cmp_pos_k, a_cmp_pos_v, a_cmp_k_w1, a_cmp_k_b1, a_cmp_k_w2, a_cmp_v_w1, a_cmp_v_b1, a_cmp_v_w2, a_w_out, kv_norm, kv_w_a, kv_c_norm, kv_w_b, kv_k_norm, b_attn_norm, b_w_q_a, b_q_a_norm, b_w_q_b, b_q_norm, b_w_out, ffn_norm, ffn_w_gate_up, ffn_w_down):
    B, T, D = x.shape
    n_a = a_w_in.shape[0]
    n_b = b_w_q_a.shape[0]
    xs = x.reshape(B * T, D)

    inv = ROPE_THETA ** (-jnp.arange(0, QK_ROPE, 2, dtype=F32) / QK_ROPE)
    ang = jnp.arange(T, dtype=F32)[:, None] * inv[None, :]
    cos, sin = jnp.cos(ang), jnp.sin(ang)
    cos2 = _rope_layout(jnp.concatenate([cos, cos], axis=-1))
    sin2 = _rope_layout(jnp.concatenate([-sin, sin], axis=-1))

    k_shared = v_shared = None
    for layer in range(n_a + n_b):
        if layer < n_a:
            i = layer
            o = _nsa_mixer(xs, B, T, a_attn_norm[i], _pad_cols(a_w_in[i]).astype(BF16),
                           a_q_norm[i], a_kcmp_norm[i], a_kslc_norm[i], a_kwin_norm[i],
                           a_cmp_pos_k[i], a_cmp_pos_v[i], a_cmp_k_w1[i], a_cmp_k_b1[i], a_cmp_k_w2[i],
                           a_cmp_v_w1[i], a_cmp_v_b1[i], a_cmp_v_w2[i])
            w_out = a_w_out[i]
        else:
            j = layer - n_a
            q = _mla_q(xs, b_attn_norm[j], b_w_q_a[j].astype(BF16), b_q_a_norm[j],
                       _head_layout(b_w_q_b[j]).astype(BF16), b_q_norm[j], cos2, sin2, B, T)
            o = _mla_attn(q, k_shared, v_shared, B, T)
            w_out = b_w_out[j]
        xs = _out_ffn(xs, o, w_out.astype(BF16), ffn_norm[layer],
                      ffn_w_gate_up[layer].astype(BF16), ffn_w_down[layer].astype(BF16))
        if layer == n_a - 1:
            w_kv_a = jnp.concatenate([kv_w_a[:, :KV_LORA], _rope_layout(kv_w_a[:, KV_LORA:])], axis=-1)
            k_shared, v_shared = _mla_kv(xs, kv_norm, w_kv_a.astype(BF16), kv_c_norm, kv_w_b.astype(BF16),
                                         kv_k_norm, cos2, sin2, B, T)
    return xs.reshape(B, T, D)
```
